```python
import jax, jax.numpy as jnp
from jax import lax
import numpy as np

D_MODEL = 1024
BATCH = 4
SEQ = 4096
DEPTH = 4

GRID_W = 64
NA_HEADS = 8
NA_HEAD_DIM = 64
NA_WIN_ROWS = 8
NA_WIN_COLS = 16
D_NA = NA_HEADS * NA_HEAD_DIM
SGU_GROUPS = 8
SGU_GROUP_DIM = 64
SGU_CHUNK = 128
D_SGU = SGU_GROUPS * SGU_GROUP_DIM
D_MIX = D_NA + D_SGU
D_IN = 3 * D_NA + 2 * D_SGU
MOE_GROUPS = 4
MOE_EXPERTS_PER_GROUP = 4
MOE_EXPERTS = MOE_GROUPS * MOE_EXPERTS_PER_GROUP
MOE_TOP_K = 2
D_EXPERT = 256
DEEPNORM_ALPHA = (2 * DEPTH) ** 0.25
DEEPNORM_BETA = (8 * DEPTH) ** -0.25
LN_EPS = 1e-5

kernel_name = "hybrid_natten_sgu_hmoe_deepnorm"


def layer_norm(x, g, b):
    xf = x.astype(jnp.float32)
    mu = jnp.mean(xf, axis=-1, keepdims=True)
    xc = xf - mu
    var = jnp.mean(jnp.square(xc), axis=-1, keepdims=True)
    y = xc * lax.rsqrt(var + LN_EPS) * g.astype(jnp.float32) + b.astype(jnp.float32)
    return y.astype(x.dtype)


def rms_norm(x, g):
    xf = x.astype(jnp.float32)
    y = xf * lax.rsqrt(jnp.mean(jnp.square(xf), axis=-1, keepdims=True) + LN_EPS)
    return (y * g.astype(jnp.float32)).astype(x.dtype)


def neighbourhood_attention(q, k, v, rel_bias):
    B, T, H, Dh = q.shape
    rows = T // GRID_W
    kh = min(NA_WIN_ROWS, rows)
    kw = NA_WIN_COLS
    qg = q.reshape(B, rows, GRID_W, H, Dh)
    kg = k.reshape(B, rows, GRID_W, H, Dh)
    vg = v.reshape(B, rows, GRID_W, H, Dh)
    cols = np.arange(GRID_W)
    col_start = np.clip(cols - kw // 2, 0, GRID_W - kw)
    col_idx = col_start[:, None] + np.arange(kw)[None, :]
    dc = jnp.asarray(col_idx - cols[:, None], jnp.int32)
    row_start = np.clip(np.arange(rows) - kh // 2, 0, rows - kh)

    def one_row(args):
        r, rs = args
        q_r = lax.dynamic_index_in_dim(qg, r, axis=1, keepdims=False)
        k_r = lax.dynamic_slice_in_dim(kg, rs, kh, axis=1)
        v_r = lax.dynamic_slice_in_dim(vg, rs, kh, axis=1)
        k_n = k_r[:, :, col_idx]
        v_n = v_r[:, :, col_idx]
        dr = rs + jnp.arange(kh, dtype=jnp.int32) - r
        bias = rel_bias[:, dr[None, :, None] + (NA_WIN_ROWS - 1),
                        dc[:, None, :] + (NA_WIN_COLS - 1)]
        s = jnp.einsum('bchd,bicjhd->bhcij', q_r, k_n)
        logits = s.astype(jnp.float32) + bias.astype(jnp.float32)
        p = jax.nn.softmax(logits.reshape(B, H, GRID_W, kh * kw), axis=-1)
        p = p.reshape(B, H, GRID_W, kh, kw).astype(v.dtype)
        return jnp.einsum('bhcij,bicjhd->bchd', p, v_n)

    out = lax.map(one_row, (jnp.arange(rows, dtype=jnp.int32),
                            jnp.asarray(row_start, jnp.int32)))
    return jnp.transpose(out, (1, 0, 2, 3, 4)).reshape(B, T, H * Dh)


def spatial_gating(u, vs, ln_g, ln_b, w_s, b_s):
    B, T, _ = u.shape
    n_chunks = T // SGU_CHUNK
    vg = vs.reshape(B, T, SGU_GROUPS, SGU_GROUP_DIM)
    vg = layer_norm(vg, ln_g.reshape(SGU_GROUPS, SGU_GROUP_DIM), ln_b.reshape(SGU_GROUPS, SGU_GROUP_DIM))
    vg = vg.reshape(B, n_chunks, SGU_CHUNK, SGU_GROUPS, SGU_GROUP_DIM)
    mixed = jnp.einsum('gpq,bnqgd->bnpgd', w_s, vg) + jnp.transpose(b_s)[:, :, None]
    return u * mixed.reshape(B, T, D_SGU)


def hierarchical_moe(h, w_rg, b_rg, w_re, b_re, w_gate, w_up, w_down):
    B, T, D = h.shape
    hf = h.reshape(B * T, D)
    g_logits = (hf @ w_rg).astype(jnp.float32) + b_rg.astype(jnp.float32)
    p_group = jax.nn.softmax(g_logits, axis=-1)
    g_star = jnp.argmax(g_logits, axis=-1)
    gate_group = jnp.take_along_axis(p_group, g_star[:, None], axis=1)
    e_logits = ((hf @ w_re).astype(jnp.float32) + b_re.astype(jnp.float32)
                ).reshape(B * T, MOE_GROUPS, MOE_EXPERTS_PER_GROUP)
    e_sel = jnp.take_along_axis(e_logits, g_star[:, None, None], axis=1)[:, 0]
    top_vals, top_idx = lax.top_k(e_sel, MOE_TOP_K)
    top_w = jax.nn.softmax(top_vals, axis=-1)
    within = jnp.sum(jax.nn.one_hot(top_idx, MOE_EXPERTS_PER_GROUP, dtype=jnp.float32)
                     * top_w[..., None], axis=1)
    combine = (jax.nn.one_hot(g_star, MOE_GROUPS, dtype=jnp.float32)[:, :, None]
               * within[:, None, :] * gate_group[:, :, None]).reshape(B * T, MOE_EXPERTS)
    combine = combine.astype(h.dtype)
    y = jnp.zeros_like(hf)
    for e in range(MOE_EXPERTS):
        act = jax.nn.silu(hf @ w_gate[e]) * (hf @ w_up[e])
        y = y + combine[:, e:e + 1] * (act @ w_down[e])
    return y.reshape(B, T, D)


def setup_inputs(seed: int = 0) -> dict:
    key = jax.random.key(seed)
    ks = jax.random.split(key, 20)
    f32 = jnp.float32
    nrm = lambda k, shape, s: jax.random.normal(k, shape, f32) * s
    x = jax.random.normal(ks[0], (BATCH, SEQ, D_MODEL), f32)
    col_scale = jnp.concatenate([jnp.ones((2 * D_NA,), f32),
                                 jnp.full((D_NA,), DEEPNORM_BETA, f32),
                                 jnp.ones((2 * D_SGU,), f32)])
    w_in = nrm(ks[1], (DEPTH, D_MODEL, D_IN), D_MODEL ** -0.5) * col_scale
    w_out = nrm(ks[2], (DEPTH, D_MIX, D_MODEL), DEEPNORM_BETA * D_MIX ** -0.5)
    na_rel_bias = nrm(ks[3], (DEPTH, NA_HEADS, 2 * NA_WIN_ROWS - 1, 2 * NA_WIN_COLS - 1), 0.02)
    sgu_ln_g = 1.0 + nrm(ks[4], (DEPTH, D_SGU), 0.02)
    sgu_ln_b = nrm(ks[5], (DEPTH, D_SGU), 0.02)
    sgu_w = nrm(ks[6], (DEPTH, SGU_GROUPS, SGU_CHUNK, SGU_CHUNK), SGU_CHUNK ** -0.5)
    sgu_b = 1.0 + nrm(ks[7], (DEPTH, SGU_GROUPS, SGU_CHUNK), 0.02)
    mix_norm_g = 1.0 + nrm(ks[8], (DEPTH, D_MIX), 0.02)
    ln1_g = 1.0 + nrm(ks[9], (DEPTH, D_MODEL), 0.02)
    ln1_b = nrm(ks[10], (DEPTH, D_MODEL), 0.02)
    router_group_w = nrm(ks[11], (DEPTH, D_MODEL, MOE_GROUPS), D_MODEL ** -0.5)
    router_group_b = nrm(ks[12], (DEPTH, MOE_GROUPS), 0.01)
    router_expert_w = nrm(ks[13], (DEPTH, D_MODEL, MOE_EXPERTS), D_MODEL ** -0.5)
    router_expert_b = nrm(ks[14], (DEPTH, MOE_EXPERTS), 0.01)
    expert_w_gate = nrm(ks[15], (DEPTH, MOE_EXPERTS, D_MODEL, D_EXPERT), DEEPNORM_BETA * D_MODEL ** -0.5)
    expert_w_up = nrm(ks[16], (DEPTH, MOE_EXPERTS, D_MODEL, D_EXPERT), DEEPNORM_BETA * D_MODEL ** -0.5)
    expert_w_down = nrm(ks[17], (DEPTH, MOE_EXPERTS, D_EXPERT, D_MODEL), DEEPNORM_BETA * D_EXPERT ** -0.5)
    ln2_g = 1.0 + nrm(ks[18], (DEPTH, D_MODEL), 0.02)
    ln2_b = nrm(ks[19], (DEPTH, D_MODEL), 0.02)
    return {"x": x, "w_in": w_in, "w_out": w_out, "na_rel_bias": na_rel_bias,
            "sgu_ln_g": sgu_ln_g, "sgu_ln_b": sgu_ln_b, "sgu_w": sgu_w, "sgu_b": sgu_b,
            "mix_norm_g": mix_norm_g, "ln1_g": ln1_g, "ln1_b": ln1_b,
            "router_group_w": router_group_w, "router_group_b": router_group_b,
            "router_expert_w": router_expert_w, "router_expert_b": router_expert_b,
            "expert_w_gate": expert_w_gate, "expert_w_up": expert_w_up,
            "expert_w_down": expert_w_down, "ln2_g": ln2_g, "ln2_b": ln2_b}


def reference(x, w_in, w_out, na_rel_bias, sgu_ln_g, sgu_ln_b, sgu_w, sgu_b, mix_norm_g,
              ln1_g, ln1_b, router_group_w, router_group_b, router_expert_w, router_expert_b,
              expert_w_gate, expert_w_up, expert_w_down, ln2_g, ln2_b):
    B, T, _ = x.shape
    splits = [D_NA, 2 * D_NA, 3 * D_NA, 3 * D_NA + D_SGU]
    for l in range(DEPTH):
        proj = jnp.einsum('btd,de->bte', x, w_in[l])
        q, k, v, u, vs = jnp.split(proj, splits, axis=-1)
        q = q.reshape(B, T, NA_HEADS, NA_HEAD_DIM) * (NA_HEAD_DIM ** -0.5)
        k = k.reshape(B, T, NA_HEADS, NA_HEAD_DIM)
        v = v.reshape(B, T, NA_HEADS, NA_HEAD_DIM)
        a_out = neighbourhood_attention(q, k, v, na_rel_bias[l])
        s_out = spatial_gating(jax.nn.gelu(u), jax.nn.gelu(vs), sgu_ln_g[l], sgu_ln_b[l],
                               sgu_w[l], sgu_b[l])
        mixed = jnp.concatenate([rms_norm(a_out, mix_norm_g[l, :D_NA]),
                                 rms_norm(s_out, mix_norm_g[l, D_NA:])], axis=-1)
        mix_out = jnp.einsum('bte,ed->btd', mixed, w_out[l])
        x = layer_norm(DEEPNORM_ALPHA * x + mix_out, ln1_g[l], ln1_b[l])
        moe_out = hierarchical_moe(x, router_group_w[l], router_group_b[l],
                                   router_expert_w[l], router_expert_b[l],
                                   expert_w_gate[l], expert_w_up[l], expert_w_down[l])
        x = layer_norm(DEEPNORM_ALPHA * x + moe_out, ln2_g[l], ln2_b[l])
    return x
```

```python
import functools
import math

import numpy as np
import jax
import jax.numpy as jnp
from jax import lax
from jax.experimental import pallas as pl
from jax.experimental.pallas import tpu as pltpu

F32 = jnp.float32
BF16 = jnp.bfloat16

D_MODEL = 1024
DEPTH = 4
GRID_W = 64
NA_HEADS = 8
NA_HEAD_DIM = 64
NA_WIN_ROWS = 8
NA_WIN_COLS = 16
D_NA = NA_HEADS * NA_HEAD_DIM
SGU_GROUPS = 8
SGU_GROUP_DIM = 64
SGU_CHUNK = 128
D_SGU = SGU_GROUPS * SGU_GROUP_DIM
D_IN = 3 * D_NA + 2 * D_SGU
MOE_GROUPS = 4
MOE_EXPERTS_PER_GROUP = 4
MOE_EXPERTS = MOE_GROUPS * MOE_EXPERTS_PER_GROUP
D_EXPERT = 256
DEEPNORM_ALPHA = (2 * DEPTH) ** 0.25
LN_EPS = 1e-5

LANES = 128
HEAD_PAIRS = NA_HEADS // 2
NEG_BIG = -1e30
BAND_KEYS = NA_WIN_ROWS * GRID_W
BIAS_TILES = 7
ROUTER_ROWS = 32

TM_PROJ = 512
NA_ROWS_PER_STEP = 8
TM_SGU = 512
TM_OUT = 512
TM_MOE = 1024
VMEM_LIMIT = 56 * 1024 * 1024


def _gelu_tanh(x):
    c = math.sqrt(2.0 / math.pi)
    return x * (0.5 * (1.0 + jnp.tanh(c * (x + 0.044715 * (x * x * x)))))


def _layer_norm_rows(h, g, b):
    mu = jnp.mean(h, axis=-1, keepdims=True)
    hc = h - mu
    var = jnp.mean(hc * hc, axis=-1, keepdims=True)
    return hc * lax.rsqrt(var + LN_EPS) * g + b


def _in_proj_kernel(x_ref, w_ref, q_ref, k_ref, v_ref, u_ref, vs_ref):
    xb = x_ref[...].astype(BF16)

    def mm(j):
        return jnp.dot(xb, w_ref[:, j * D_NA:(j + 1) * D_NA], preferred_element_type=F32)

    q_ref[...] = (mm(0) * (NA_HEAD_DIM ** -0.5)).astype(BF16)
    k_ref[...] = mm(1).astype(BF16)
    v_ref[...] = mm(2).astype(BF16)
    u_ref[...] = _gelu_tanh(mm(3))
    vs_ref[...] = _gelu_tanh(mm(4))


def _in_proj(x, w_in_bf16):
    n = x.shape[0]
    tok = lambda i: (i, 0)
    return pl.pallas_call(
        _in_proj_kernel,
        grid=(n // TM_PROJ,),
        in_specs=[pl.BlockSpec((TM_PROJ, D_MODEL), tok),
                  pl.BlockSpec((D_MODEL, D_IN), lambda i: (0, 0))],
        out_specs=[pl.BlockSpec((TM_PROJ, D_NA), tok)] * 5,
        out_shape=[jax.ShapeDtypeStruct((n, D_NA), BF16)] * 3
                  + [jax.ShapeDtypeStruct((n, D_SGU), F32)] * 2,
        compiler_params=pltpu.CompilerParams(dimension_semantics=("parallel",),
                                             vmem_limit_bytes=VMEM_LIMIT),
        name="in_proj",
    )(x, w_in_bf16)


def _na_bias_table(rel_bias):
    cols = np.arange(GRID_W)
    col_start = np.clip(cols - NA_WIN_COLS // 2, 0, GRID_W - NA_WIN_COLS)
    kc = np.arange(GRID_W)
    valid = (kc[None, :] >= col_start[:, None]) & (kc[None, :] < col_start[:, None] + NA_WIN_COLS)
    dc_idx = np.clip(kc[None, :] - cols[:, None] + NA_WIN_COLS - 1, 0, 2 * NA_WIN_COLS - 2)
    f = rel_bias.astype(F32)[:, :, dc_idx]
    f = jnp.where(valid[None, None], f, NEG_BIG)
    n_off = 2 * NA_WIN_ROWS - 1
    f = jnp.transpose(f, (0, 2, 1, 3)).reshape(NA_HEADS, GRID_W, n_off * GRID_W)
    g = f.reshape(HEAD_PAIRS, 2 * GRID_W, n_off * GRID_W)
    width = BIAS_TILES * LANES
    tabs = jnp.stack([g[:, :, :width], g[:, :, GRID_W:GRID_W + width]])
    tabs = tabs.reshape(2, HEAD_PAIRS, 2 * GRID_W, BIAS_TILES, LANES)
    return jnp.transpose(tabs, (0, 1, 3, 2, 4))


def _na_kernel(q_ref, k_ref, v_ref, t_ref, g_ref, o_ref):
    rows = k_ref.shape[0] // GRID_W
    blk = pl.program_id(1)
    lane = lax.broadcasted_iota(jnp.int32, (GRID_W, LANES), 1)
    first_head = lane < NA_HEAD_DIM
    gain = g_ref[...]

    def one_row(rr, carry):
        r = blk * NA_ROWS_PER_STEP + rr
        rs = jnp.clip(r - NA_WIN_ROWS // 2, 0, rows - NA_WIN_ROWS)
        a0 = rs - r + (NA_WIN_ROWS - 1)
        par = a0 % 2
        j0 = a0 // 2
        q_off = pl.multiple_of(rr * GRID_W, GRID_W)
        k_off = pl.multiple_of(rs * GRID_W, GRID_W)
        outs = []
        for hp in range(HEAD_PAIRS):
            cs = slice(hp * LANES, (hp + 1) * LANES)
            qp = q_ref[pl.ds(q_off, GRID_W), cs]
            kp = k_ref[pl.ds(k_off, BAND_KEYS), cs]
            vp = v_ref[pl.ds(k_off, BAND_KEYS), cs]
            zero = jnp.zeros_like(qp)
            q2 = jnp.concatenate([jnp.where(first_head, qp, zero),
                                  jnp.where(first_head, zero, qp)], axis=0)
            s = lax.dot_general(q2, kp, (((1,), (1,)), ((), ())),
                                preferred_element_type=F32)
            bias = jnp.concatenate([t_ref[par, hp, j0 + t] for t in range(BAND_KEYS // LANES)],
                                   axis=1)
            s = s + bias
            m = jnp.max(s, axis=1, keepdims=True)
            p = jnp.exp(s - m)
            l = jnp.sum(p, axis=1, keepdims=True)
            o = jnp.dot(p.astype(BF16), vp, preferred_element_type=F32) / l
            outs.append(jnp.where(first_head, o[:GRID_W], o[GRID_W:]))
        a = jnp.concatenate(outs, axis=1)
        ms = jnp.mean(a * a, axis=-1, keepdims=True)
        o_ref[pl.ds(q_off, GRID_W), :] = (a * lax.rsqrt(ms + LN_EPS) * gain).astype(o_ref.dtype)
        return carry

    lax.fori_loop(0, NA_ROWS_PER_STEP, one_row, 0)


def _na_attention(q, k, v, table, gain, batch, seq):
    n = q.shape[0]
    steps = seq // (GRID_W * NA_ROWS_PER_STEP)
    tq = NA_ROWS_PER_STEP * GRID_W
    return pl.pallas_call(
        _na_kernel,
        grid=(batch, steps),
        in_specs=[pl.BlockSpec((tq, D_NA), lambda b, i: (b * steps + i, 0)),
                  pl.BlockSpec((seq, D_NA), lambda b, i: (b, 0)),
                  pl.BlockSpec((seq, D_NA), lambda b, i: (b, 0)),
                  pl.BlockSpec(table.shape, lambda b, i: (0, 0, 0, 0, 0)),
                  pl.BlockSpec((1, D_NA), lambda b, i: (0, 0))],
        out_specs=pl.BlockSpec((tq, D_NA), lambda b, i: (b * steps + i, 0)),
        out_shape=jax.ShapeDtypeStruct((n, D_NA), BF16),
        compiler_params=pltpu.CompilerParams(dimension_semantics=("parallel", "parallel"),
                                             vmem_limit_bytes=VMEM_LIMIT),
        name="na_attention",
    )(q, k, v, table, gain)


def _sgu_kernel(u_ref, vs_ref, lng_ref, lnb_ref, ws_ref, bs_ref, g_ref, o_ref):
    lane = lax.broadcasted_iota(jnp.int32, (SGU_CHUNK, LANES), 1)
    first = lane < SGU_GROUP_DIM
    inv = 1.0 / SGU_GROUP_DIM

    def seg_mean(t):
        lo = jnp.sum(jnp.where(first, t, 0.0), axis=-1, keepdims=True)
        hi = jnp.sum(jnp.where(first, 0.0, t), axis=-1, keepdims=True)
        return jnp.where(first, lo, hi) * inv

    for c in range(TM_SGU // SGU_CHUNK):
        rs = slice(c * SGU_CHUNK, (c + 1) * SGU_CHUNK)
        tiles = []
        for j in range(D_SGU // LANES):
            cs = slice(j * LANES, (j + 1) * LANES)
            xt = vs_ref[rs, cs]
            xc = xt - seg_mean(xt)
            var = seg_mean(xc * xc)
            y = (xc * lax.rsqrt(var + LN_EPS) * lng_ref[:, cs] + lnb_ref[:, cs]).astype(BF16)
            m0 = jnp.dot(ws_ref[2 * j], y, preferred_element_type=F32)
            m1 = jnp.dot(ws_ref[2 * j + 1], y, preferred_element_type=F32)
            mixed = jnp.where(first, m0, m1) + bs_ref[:, cs]
            tiles.append(u_ref[rs, cs] * mixed)
        so = jnp.concatenate(tiles, axis=1)
        ms = jnp.mean(so * so, axis=-1, keepdims=True)
        o_ref[rs, :] = (so * lax.rsqrt(ms + LN_EPS) * g_ref[...]).astype(o_ref.dtype)


def _sgu(gu, gv, ln_g, ln_b, w_s_bf16, bs_full, gain):
    n = gu.shape[0]
    tok = lambda i: (i, 0)
    row = lambda i: (0, 0)
    return pl.pallas_call(
        _sgu_kernel,
        grid=(n // TM_SGU,),
        in_specs=[pl.BlockSpec((TM_SGU, D_SGU), tok),
                  pl.BlockSpec((TM_SGU, D_SGU), tok),
                  pl.BlockSpec((1, D_SGU), row),
                  pl.BlockSpec((1, D_SGU), row),
                  pl.BlockSpec((SGU_GROUPS, SGU_CHUNK, SGU_CHUNK), lambda i: (0, 0, 0)),
                  pl.BlockSpec((SGU_CHUNK, D_SGU), row),
                  pl.BlockSpec((1, D_SGU), row)],
        out_specs=pl.BlockSpec((TM_SGU, D_SGU), tok),
        out_shape=jax.ShapeDtypeStruct((n, D_SGU), BF16),
        compiler_params=pltpu.CompilerParams(dimension_semantics=("parallel",),
                                             vmem_limit_bytes=VMEM_LIMIT),
        name="spatial_gating",
    )(gu, gv, ln_g, ln_b, w_s_bf16, bs_full, gain)


def _split_bf16(a):
    hi = a.astype(BF16)
    lo = (a - hi.astype(F32)).astype(BF16)
    return hi, lo


def _out_proj_kernel(a_ref, s_ref, x_ref, w_ref, g_ref, b_ref, rwh_ref, rwl_ref, rb_ref,
                     x1_ref, comb_ref):
    mix = jnp.dot(a_ref[...], w_ref[:D_NA, :], preferred_element_type=F32)
    mix = mix + jnp.dot(s_ref[...], w_ref[D_NA:, :], preferred_element_type=F32)
    x1 = _layer_norm_rows(DEEPNORM_ALPHA * x_ref[...] + mix, g_ref[...], b_ref[...])
    x1_ref[...] = x1

    xh, xl = _split_bf16(x1)
    nt = (((1,), (1,)), ((), ()))
    lg = lax.dot_general(rwh_ref[...], xh, nt, preferred_element_type=F32)
    lg = lg + lax.dot_general(rwh_ref[...], xl, nt, preferred_element_type=F32)
    lg = lg + lax.dot_general(rwl_ref[...], xh, nt, preferred_element_type=F32)
    lg = lg + rb_ref[:, 0:1]

    gl = [lg[g:g + 1, :] for g in range(MOE_GROUPS)]
    gmax = functools.reduce(jnp.maximum, gl)
    gidx = jnp.full(gmax.shape, MOE_GROUPS - 1, jnp.int32)
    for g in range(MOE_GROUPS - 2, -1, -1):
        gidx = jnp.where(gl[g] == gmax, g, gidx)
    denom = functools.reduce(lambda a, b: a + b, [jnp.exp(t - gmax) for t in gl])
    gate = 1.0 / denom

    def expert_logit(i):
        rows_ = [lg[8 + MOE_EXPERTS_PER_GROUP * g + i:9 + MOE_EXPERTS_PER_GROUP * g + i, :]
                 for g in range(MOE_GROUPS)]
        sel = rows_[MOE_GROUPS - 1]
        for g in range(MOE_GROUPS - 2, -1, -1):
            sel = jnp.where(gidx == g, rows_[g], sel)
        return sel

    el = [expert_logit(i) for i in range(MOE_EXPERTS_PER_GROUP)]
    v1 = functools.reduce(jnp.maximum, el)
    i1 = jnp.full(v1.shape, MOE_EXPERTS_PER_GROUP - 1, jnp.int32)
    for i in range(MOE_EXPERTS_PER_GROUP - 2, -1, -1):
        i1 = jnp.where(el[i] == v1, i, i1)
    rest = [jnp.where(i1 == i, -jnp.inf, el[i]) for i in range(MOE_EXPERTS_PER_GROUP)]
    v2 = functools.reduce(jnp.maximum, rest)
    i2 = jnp.full(v2.shape, MOE_EXPERTS_PER_GROUP - 1, jnp.int32)
    for i in range(MOE_EXPERTS_PER_GROUP - 2, -1, -1):
        i2 = jnp.where((rest[i] == v2) & (i1 != i), i, i2)
    e2 = jnp.exp(v2 - v1)
    w1 = 1.0 / (1.0 + e2)
    w2 = e2 * w1
    within = [jnp.where(i1 == i, w1, 0.0) + jnp.where(i2 == i, w2, 0.0)
              for i in range(MOE_EXPERTS_PER_GROUP)]
    rows_out = []
    for g in range(MOE_GROUPS):
        for i in range(MOE_EXPERTS_PER_GROUP):
            rows_out.append(jnp.where(gidx == g, within[i], 0.0) * gate)
    tm = x1.shape[0]
    rows_out.append(jnp.zeros((LANES - MOE_EXPERTS, tm), F32))
    comb_t = jnp.concatenate(rows_out, axis=0)
    comb_ref[...] = comb_t.T


def _out_proj(a, s, x, w_out_bf16, g1, b1, rw_hi, rw_lo, rb):
    n = x.shape[0]
    tok = lambda i: (i, 0)
    row = lambda i: (0, 0)
    return pl.pallas_call(
        _out_proj_kernel,
        grid=(n // TM_OUT,),
        in_specs=[pl.BlockSpec((TM_OUT, D_NA), tok),
                  pl.BlockSpec((TM_OUT, D_SGU), tok),
                  pl.BlockSpec((TM_OUT, D_MODEL), tok),
                  pl.BlockSpec((D_MODEL, D_MODEL), row),
                  pl.BlockSpec((1, D_MODEL), row),
                  pl.BlockSpec((1, D_MODEL), row),
                  pl.BlockSpec((ROUTER_ROWS, D_MODEL), row),
                  pl.BlockSpec((ROUTER_ROWS, D_MODEL), row),
                  pl.BlockSpec((ROUTER_ROWS, LANES), row)],
        out_specs=[pl.BlockSpec((TM_OUT, D_MODEL), tok),
                   pl.BlockSpec((TM_OUT, LANES), tok)],
        out_shape=[jax.ShapeDtypeStruct((n, D_MODEL), F32),
                   jax.ShapeDtypeStruct((n, LANES), F32)],
        compiler_params=pltpu.CompilerParams(dimension_semantics=("parallel",),
                                             vmem_limit_bytes=VMEM_LIMIT),
        name="out_proj_ln_router",
    )(a, s, x, w_out_bf16, g1, b1, rw_hi, rw_lo, rb)


def _moe_kernel(x_ref, c_ref, wg_ref, wu_ref, wd_ref, g_ref, b_ref, o_ref, xb_ref, acc_ref):
    e = pl.program_id(1)

    @pl.when(e == 0)
    def _():
        xb_ref[...] = x_ref[...].astype(BF16)
        acc_ref[...] = jnp.zeros_like(acc_ref)

    xb = xb_ref[...]
    hg = jnp.dot(xb, wg_ref[0], preferred_element_type=F32)
    hu = jnp.dot(xb, wu_ref[0], preferred_element_type=F32)
    lane = lax.broadcasted_iota(jnp.int32, c_ref.shape, 1)
    c = jnp.sum(jnp.where(lane == e, c_ref[...], 0.0), axis=-1, keepdims=True)
    act = hg * (1.0 / (1.0 + jnp.exp(-hg))) * hu
    acc_ref[...] += jnp.dot((act * c).astype(BF16), wd_ref[0], preferred_element_type=F32)

    @pl.when(e == MOE_EXPERTS - 1)
    def _():
        h = DEEPNORM_ALPHA * x_ref[...] + acc_ref[...]
        o_ref[...] = _layer_norm_rows(h, g_ref[...], b_ref[...])


def _moe(x1, comb, wg, wu, wd, g2, b2):
    n = x1.shape[0]
    tok = lambda i, e: (i, 0)
    row = lambda i, e: (0, 0)
    return pl.pallas_call(
        _moe_kernel,
        grid=(n // TM_MOE, MOE_EXPERTS),
        in_specs=[pl.BlockSpec((TM_MOE, D_MODEL), tok),
                  pl.BlockSpec((TM_MOE, LANES), tok),
                  pl.BlockSpec((1, D_MODEL, D_EXPERT), lambda i, e: (e, 0, 0)),
                  pl.BlockSpec((1, D_MODEL, D_EXPERT), lambda i, e: (e, 0, 0)),
                  pl.BlockSpec((1, D_EXPERT, D_MODEL), lambda i, e: (e, 0, 0)),
                  pl.BlockSpec((1, D_MODEL), row),
                  pl.BlockSpec((1, D_MODEL), row)],
        out_specs=pl.BlockSpec((TM_MOE, D_MODEL), tok),
        out_shape=jax.ShapeDtypeStruct((n, D_MODEL), F32),
        scratch_shapes=[pltpu.VMEM((TM_MOE, D_MODEL), BF16),
                        pltpu.VMEM((TM_MOE, D_MODEL), F32)],
        compiler_params=pltpu.CompilerParams(dimension_semantics=("parallel", "arbitrary"),
                                             vmem_limit_bytes=VMEM_LIMIT),
        name="moe_experts_ln",
    )(x1, comb, wg, wu, wd, g2, b2)


def _router_weights(w_rg, b_rg, w_re, b_re):
    wt = jnp.zeros((ROUTER_ROWS, D_MODEL), F32)
    wt = wt.at[:MOE_GROUPS].set(w_rg.T.astype(F32))
    wt = wt.at[8:8 + MOE_EXPERTS].set(w_re.T.astype(F32))
    bt = jnp.zeros((ROUTER_ROWS,), F32)
    bt = bt.at[:MOE_GROUPS].set(b_rg.astype(F32))
    bt = bt.at[8:8 + MOE_EXPERTS].set(b_re.astype(F32))
    hi, lo = _split_bf16(wt)
    return hi, lo, jnp.broadcast_to(bt[:, None], (ROUTER_ROWS, LANES))


def kernel(x, w_in, w_out, na_rel_bias, sgu_ln_g, sgu_ln_b, sgu_w, sgu_b, mix_norm_g, ln1_g, ln1_b, router_group_w, router_group_b, router_expert_w, router_expert_b, expert_w_gate, expert_w_up, expert_w_down, ln2_g, ln2_b):
    batch, seq, d = x.shape
    n = batch * seq
    xf = x.reshape(n, d).astype(F32)
    row = lambda a: a.astype(F32).reshape(1, -1)
    for l in range(DEPTH):
        q, k, v, gu, gv = _in_proj(xf, w_in[l].astype(BF16))
        a = _na_attention(q, k, v, _na_bias_table(na_rel_bias[l]), row(mix_norm_g[l, :D_NA]),
                          batch, seq)
        bs_full = jnp.repeat(sgu_b[l].astype(F32).T, SGU_GROUP_DIM, axis=1)
        s = _sgu(gu, gv, row(sgu_ln_g[l]), row(sgu_ln_b[l]), sgu_w[l].astype(BF16), bs_full,
                 row(mix_norm_g[l, D_NA:]))
        rw_hi, rw_lo, rb = _router_weights(router_group_w[l], router_group_b[l],
                                           router_expert_w[l], router_expert_b[l])
        x1, comb = _out_proj(a, s, xf, w_out[l].astype(BF16), row(ln1_g[l]), row(ln1_b[l]),
                             rw_hi, rw_lo, rb)
        xf = _moe(x1, comb, expert_w_gate[l].astype(BF16), expert_w_up[l].astype(BF16),
                  expert_w_down[l].astype(BF16), row(ln2_g[l]), row(ln2_b[l]))
    return xf.reshape(batch, seq, d).astype(x.dtype)
```

```python
import functools
import math

import numpy as np
import jax
import jax.numpy as jnp
from jax import lax
from jax.experimental import pallas as pl
from jax.experimental.pallas import tpu as pltpu

F32 = jnp.float32
BF16 = jnp.bfloat16

D_MODEL = 1024
DEPTH = 4
GRID_W = 64
NA_HEADS = 8
NA_HEAD_DIM = 64
NA_WIN_ROWS = 8
NA_WIN_COLS = 16
D_NA = NA_HEADS * NA_HEAD_DIM
SGU_GROUPS = 8
SGU_GROUP_DIM = 64
SGU_CHUNK = 128
D_SGU = SGU_GROUPS * SGU_GROUP_DIM
D_IN = 3 * D_NA + 2 * D_SGU
MOE_GROUPS = 4
MOE_EXPERTS_PER_GROUP = 4
MOE_EXPERTS = MOE_GROUPS * MOE_EXPERTS_PER_GROUP
D_EXPERT = 256
DEEPNORM_ALPHA = (2 * DEPTH) ** 0.25
LN_EPS = 1e-5

LANES = 128
HEAD_PAIRS = NA_HEADS // 2
NEG_BIG = -1e30
BAND_KEYS = NA_WIN_ROWS * GRID_W
BIAS_TILES = 7
ROUTER_ROWS = 32

TM_PROJ = 512
NA_ROWS_PER_STEP = 8
NA_ROWS_IN_FLIGHT = 4
NA_STRIP = 16
TM_SGU = 512
TM_OUT = 512
TM_MOE = 1024
VMEM_LIMIT = 56 * 1024 * 1024


def _gelu_tanh(x):
    c = math.sqrt(2.0 / math.pi)
    return x * (0.5 * (1.0 + jnp.tanh(c * (x + 0.044715 * (x * x * x)))))


def _layer_norm_rows(h, g, b):
    mu = jnp.mean(h, axis=-1, keepdims=True)
    hc = h - mu
    var = jnp.mean(hc * hc, axis=-1, keepdims=True)
    return hc * lax.rsqrt(var + LN_EPS) * g + b


def _in_proj_kernel(x_ref, w_ref, q_ref, k_ref, v_ref, u_ref, vs_ref):
    xb = x_ref[...].astype(BF16)

    def mm(j):
        return jnp.dot(xb, w_ref[:, j * D_NA:(j + 1) * D_NA], preferred_element_type=F32)

    q_ref[...] = (mm(0) * (NA_HEAD_DIM ** -0.5)).astype(BF16)
    k_ref[...] = mm(1).astype(BF16)
    v_ref[...] = mm(2).astype(BF16)
    u_ref[...] = _gelu_tanh(mm(3))
    vs_ref[...] = _gelu_tanh(mm(4))


def _in_proj(x, w_in_bf16):
    n = x.shape[0]
    tok = lambda i: (i, 0)
    return pl.pallas_call(
        _in_proj_kernel,
        grid=(n // TM_PROJ,),
        in_specs=[pl.BlockSpec((TM_PROJ, D_MODEL), tok),
                  pl.BlockSpec((D_MODEL, D_IN), lambda i: (0, 0))],
        out_specs=[pl.BlockSpec((TM_PROJ, D_NA), tok)] * 5,
        out_shape=[jax.ShapeDtypeStruct((n, D_NA), BF16)] * 3
                  + [jax.ShapeDtypeStruct((n, D_SGU), F32)] * 2,
        compiler_params=pltpu.CompilerParams(dimension_semantics=("parallel",),
                                             vmem_limit_bytes=VMEM_LIMIT),
        name="in_proj",
    )(x, w_in_bf16)


def _na_bias_table(rel_bias):
    cols = np.arange(GRID_W)
    col_start = np.clip(cols - NA_WIN_COLS // 2, 0, GRID_W - NA_WIN_COLS)
    kc = np.arange(GRID_W)
    valid = (kc[None, :] >= col_start[:, None]) & (kc[None, :] < col_start[:, None] + NA_WIN_COLS)
    dc = kc[None, :] - cols[:, None] + NA_WIN_COLS - 1
    onehot = (dc[None] == np.arange(2 * NA_WIN_COLS - 1)[:, None, None]) & valid[None]
    f = jnp.einsum('had,dck->hack', rel_bias.astype(F32), jnp.asarray(onehot, F32),
                   precision=lax.Precision.HIGHEST)
    f = jnp.where(valid[None, None], f, NEG_BIG)
    n_off = 2 * NA_WIN_ROWS - 1
    f = jnp.transpose(f, (0, 2, 1, 3)).reshape(NA_HEADS, GRID_W, n_off * GRID_W)
    g = f.reshape(HEAD_PAIRS, 2 * GRID_W, n_off * GRID_W)
    width = BIAS_TILES * LANES
    tabs = jnp.stack([g[:, :, :width], g[:, :, GRID_W:GRID_W + width]])
    tabs = tabs.reshape(2, HEAD_PAIRS, 2 * GRID_W, BIAS_TILES, LANES)
    return jnp.transpose(tabs, (0, 1, 3, 2, 4))


def _na_kernel(q_ref, k_ref, v_ref, t_ref, g_ref, o_ref, s_all, p_all, a_ref):
    rows = k_ref.shape[0] // GRID_W
    blk = pl.program_id(1)
    lane = lax.broadcasted_iota(jnp.int32, (GRID_W, LANES), 1)
    first_head = lane < NA_HEAD_DIM
    key_tiles = BAND_KEYS // LANES

    def one_row(rr, slot):
        s_ref = s_all.at[slot]
        p_ref = p_all.at[slot]
        r = blk * NA_ROWS_PER_STEP + rr
        rs = jnp.clip(r - NA_WIN_ROWS // 2, 0, rows - NA_WIN_ROWS)
        a0 = rs - r + (NA_WIN_ROWS - 1)
        par = a0 % 2
        j0 = a0 // 2
        q_off = pl.multiple_of(rr * GRID_W, GRID_W)
        k_off = pl.multiple_of(rs * GRID_W, GRID_W)
        for hp in range(HEAD_PAIRS):
            cs = slice(hp * LANES, (hp + 1) * LANES)
            qp = q_ref[pl.ds(q_off, GRID_W), cs]
            kp = k_ref[pl.ds(k_off, BAND_KEYS), cs]
            zero = jnp.zeros_like(qp)
            q2 = jnp.concatenate([jnp.where(first_head, qp, zero),
                                  jnp.where(first_head, zero, qp)], axis=0)
            s_ref[hp] = lax.dot_general(q2, kp, (((1,), (1,)), ((), ())),
                                        preferred_element_type=F32)
        for hp in range(HEAD_PAIRS):
            cs = slice(hp * LANES, (hp + 1) * LANES)
            inv_l = []
            for ch in range(2 * GRID_W // NA_STRIP):
                rsl = slice(ch * NA_STRIP, (ch + 1) * NA_STRIP)
                bias = jnp.concatenate([t_ref[par, hp, j0 + t, rsl, :] for t in range(key_tiles)],
                                       axis=1)
                sc = s_ref[hp, rsl, :] + bias
                m = jnp.max(sc, axis=1, keepdims=True)
                p = jnp.exp(sc - m)
                inv_l.append(1.0 / jnp.sum(p, axis=1, keepdims=True))
                p_ref[hp, rsl, :] = p.astype(BF16)
            vp = v_ref[pl.ds(k_off, BAND_KEYS), cs]
            o = jnp.dot(p_ref[hp], vp, preferred_element_type=F32)
            o = jnp.concatenate([o[ch * NA_STRIP:(ch + 1) * NA_STRIP] * inv_l[ch]
                                 for ch in range(len(inv_l))], axis=0)
            a_ref[pl.ds(q_off, GRID_W), cs] = jnp.where(first_head, o[:GRID_W], o[GRID_W:])

    def row_group(i, carry):
        for slot in range(NA_ROWS_IN_FLIGHT):
            one_row(i * NA_ROWS_IN_FLIGHT + slot, slot)
        return carry

    lax.fori_loop(0, NA_ROWS_PER_STEP // NA_ROWS_IN_FLIGHT, row_group, 0)

    a = a_ref[...]
    ms = jnp.mean(a * a, axis=-1, keepdims=True)
    o_ref[...] = (a * lax.rsqrt(ms + LN_EPS) * g_ref[...]).astype(o_ref.dtype)


def _na_attention(q, k, v, table, gain, batch, seq):
    n = q.shape[0]
    steps = seq // (GRID_W * NA_ROWS_PER_STEP)
    tq = NA_ROWS_PER_STEP * GRID_W
    return pl.pallas_call(
        _na_kernel,
        grid=(batch, steps),
        in_specs=[pl.BlockSpec((tq, D_NA), lambda b, i: (b * steps + i, 0)),
                  pl.BlockSpec((seq, D_NA), lambda b, i: (b, 0)),
                  pl.BlockSpec((seq, D_NA), lambda b, i: (b, 0)),
                  pl.BlockSpec(table.shape, lambda b, i: (0, 0, 0, 0, 0)),
                  pl.BlockSpec((1, D_NA), lambda b, i: (0, 0))],
        out_specs=pl.BlockSpec((tq, D_NA), lambda b, i: (b * steps + i, 0)),
        out_shape=jax.ShapeDtypeStruct((n, D_NA), BF16),
        scratch_shapes=[pltpu.VMEM((NA_ROWS_IN_FLIGHT, HEAD_PAIRS, 2 * GRID_W, BAND_KEYS), F32),
                        pltpu.VMEM((NA_ROWS_IN_FLIGHT, HEAD_PAIRS, 2 * GRID_W, BAND_KEYS), BF16),
                        pltpu.VMEM((tq, D_NA), F32)],
        compiler_params=pltpu.CompilerParams(dimension_semantics=("parallel", "parallel"),
                                             vmem_limit_bytes=VMEM_LIMIT),
        name="na_attention",
    )(q, k, v, table, gain)


def _sgu_kernel(u_ref, vs_ref, lng_ref, lnb_ref, ws_ref, bs_ref, g_ref, o_ref):
    lane = lax.broadcasted_iota(jnp.int32, (SGU_CHUNK, LANES), 1)
    first = lane < SGU_GROUP_DIM
    inv = 1.0 / SGU_GROUP_DIM

    def seg_mean(t):
        lo = jnp.sum(jnp.where(first, t, 0.0), axis=-1, keepdims=True)
        hi = jnp.sum(jnp.where(first, 0.0, t), axis=-1, keepdims=True)
        return jnp.where(first, lo, hi) * inv

    for c in range(TM_SGU // SGU_CHUNK):
        rs = slice(c * SGU_CHUNK, (c + 1) * SGU_CHUNK)
        tiles = []
        for j in range(D_SGU // LANES):
            cs = slice(j * LANES, (j + 1) * LANES)
            xt = vs_ref[rs, cs]
            xc = xt - seg_mean(xt)
            var = seg_mean(xc * xc)
            y = (xc * lax.rsqrt(var + LN_EPS) * lng_ref[:, cs] + lnb_ref[:, cs]).astype(BF16)
            m0 = jnp.dot(ws_ref[2 * j], y, preferred_element_type=F32)
            m1 = jnp.dot(ws_ref[2 * j + 1], y, preferred_element_type=F32)
            mixed = jnp.where(first, m0, m1) + bs_ref[:, cs]
            tiles.append(u_ref[rs, cs] * mixed)
        so = jnp.concatenate(tiles, axis=1)
        ms = jnp.mean(so * so, axis=-1, keepdims=True)
        o_ref[rs, :] = (so * lax.rsqrt(ms + LN_EPS) * g_ref[...]).astype(o_ref.dtype)


def _sgu(gu, gv, ln_g, ln_b, w_s_bf16, bs_full, gain):
    n = gu.shape[0]
    tok = lambda i: (i, 0)
    row = lambda i: (0, 0)
    return pl.pallas_call(
        _sgu_kernel,
        grid=(n // TM_SGU,),
        in_specs=[pl.BlockSpec((TM_SGU, D_SGU), tok),
                  pl.BlockSpec((TM_SGU, D_SGU), tok),
                  pl.BlockSpec((1, D_SGU), row),
                  pl.BlockSpec((1, D_SGU), row),
                  pl.BlockSpec((SGU_GROUPS, SGU_CHUNK, SGU_CHUNK), lambda i: (0, 0, 0)),
                  pl.BlockSpec((SGU_CHUNK, D_SGU), row),
                  pl.BlockSpec((1, D_SGU), row)],
        out_specs=pl.BlockSpec((TM_SGU, D_SGU), tok),
        out_shape=jax.ShapeDtypeStruct((n, D_SGU), BF16),
        compiler_params=pltpu.CompilerParams(dimension_semantics=("parallel",),
                                             vmem_limit_bytes=VMEM_LIMIT),
        name="spatial_gating",
    )(gu, gv, ln_g, ln_b, w_s_bf16, bs_full, gain)


def _split_bf16(a):
    hi = a.astype(BF16)
    lo = (a - hi.astype(F32)).astype(BF16)
    return hi, lo


def _out_proj_kernel(a_ref, s_ref, x_ref, w_ref, g_ref, b_ref, rwh_ref, rwl_ref, rb_ref,
                     x1_ref, comb_ref):
    mix = jnp.dot(a_ref[...], w_ref[:D_NA, :], preferred_element_type=F32)
    mix = mix + jnp.dot(s_ref[...], w_ref[D_NA:, :], preferred_element_type=F32)
    x1 = _layer_norm_rows(DEEPNORM_ALPHA * x_ref[...] + mix, g_ref[...], b_ref[...])
    x1_ref[...] = x1

    xh, xl = _split_bf16(x1)
    nt = (((1,), (1,)), ((), ()))
    lg = lax.dot_general(rwh_ref[...], xh, nt, preferred_element_type=F32)
    lg = lg + lax.dot_general(rwh_ref[...], xl, nt, preferred_element_type=F32)
    lg = lg + lax.dot_general(rwl_ref[...], xh, nt, preferred_element_type=F32)
    lg = lg + rb_ref[:, 0:1]

    gl = [lg[g:g + 1, :] for g in range(MOE_GROUPS)]
    gmax = functools.reduce(jnp.maximum, gl)
    gidx = jnp.full(gmax.shape, MOE_GROUPS - 1, jnp.int32)
    for g in range(MOE_GROUPS - 2, -1, -1):
        gidx = jnp.where(gl[g] == gmax, g, gidx)
    denom = functools.reduce(lambda a, b: a + b, [jnp.exp(t - gmax) for t in gl])
    gate = 1.0 / denom

    def expert_logit(i):
        rows_ = [lg[8 + MOE_EXPERTS_PER_GROUP * g + i:9 + MOE_EXPERTS_PER_GROUP * g + i, :]
                 for g in range(MOE_GROUPS)]
        sel = rows_[MOE_GROUPS - 1]
        for g in range(MOE_GROUPS - 2, -1, -1):
            sel = jnp.where(gidx == g, rows_[g], sel)
        return sel

    el = [expert_logit(i) for i in range(MOE_EXPERTS_PER_GROUP)]
    v1 = functools.reduce(jnp.maximum, el)
    i1 = jnp.full(v1.shape, MOE_EXPERTS_PER_GROUP - 1, jnp.int32)
    for i in range(MOE_EXPERTS_PER_GROUP - 2, -1, -1):
        i1 = jnp.where(el[i] == v1, i, i1)
    rest = [jnp.where(i1 == i, -jnp.inf, el[i]) for i in range(MOE_EXPERTS_PER_GROUP)]
    v2 = functools.reduce(jnp.maximum, rest)
    i2 = jnp.full(v2.shape, MOE_EXPERTS_PER_GROUP - 1, jnp.int32)
    for i in range(MOE_EXPERTS_PER_GROUP - 2, -1, -1):
        i2 = jnp.where((rest[i] == v2) & (i1 != i), i, i2)
    e2 = jnp.exp(v2 - v1)
    w1 = 1.0 / (1.0 + e2)
    w2 = e2 * w1
    within = [jnp.where(i1 == i, w1, 0.0) + jnp.where(i2 == i, w2, 0.0)
              for i in range(MOE_EXPERTS_PER_GROUP)]
    rows_out = []
    for g in range(MOE_GROUPS):
        for i in range(MOE_EXPERTS_PER_GROUP):
            rows_out.append(jnp.where(gidx == g, within[i], 0.0) * gate)
    tm = x1.shape[0]
    rows_out.append(jnp.zeros((LANES - MOE_EXPERTS, tm), F32))
    comb_t = jnp.concatenate(rows_out, axis=0)
    comb_ref[...] = comb_t.T


def _out_proj(a, s, x, w_out_bf16, g1, b1, rw_hi, rw_lo, rb):
    n = x.shape[0]
    tok = lambda i: (i, 0)
    row = lambda i: (0, 0)
    return pl.pallas_call(
        _out_proj_kernel,
        grid=(n // TM_OUT,),
        in_specs=[pl.BlockSpec((TM_OUT, D_NA), tok),
                  pl.BlockSpec((TM_OUT, D_SGU), tok),
                  pl.BlockSpec((TM_OUT, D_MODEL), tok),
                  pl.BlockSpec((D_MODEL, D_MODEL), row),
                  pl.BlockSpec((1, D_MODEL), row),
                  pl.BlockSpec((1, D_MODEL), row),
                  pl.BlockSpec((ROUTER_ROWS, D_MODEL), row),
                  pl.BlockSpec((ROUTER_ROWS, D_MODEL), row),
                  pl.BlockSpec((ROUTER_ROWS, LANES), row)],
        out_specs=[pl.BlockSpec((TM_OUT, D_MODEL), tok),
                   pl.BlockSpec((TM_OUT, LANES), tok)],
        out_shape=[jax.ShapeDtypeStruct((n, D_MODEL), F32),
                   jax.ShapeDtypeStruct((n, LANES), F32)],
        compiler_params=pltpu.CompilerParams(dimension_semantics=("parallel",),
                                             vmem_limit_bytes=VMEM_LIMIT),
        name="out_proj_ln_router",
    )(a, s, x, w_out_bf16, g1, b1, rw_hi, rw_lo, rb)


def _moe_kernel(x_ref, c_ref, wg_ref, wu_ref, wd_ref, g_ref, b_ref, o_ref, xb_ref, acc_ref):
    e = pl.program_id(1)

    @pl.when(e == 0)
    def _():
        xb_ref[...] = x_ref[...].astype(BF16)
        acc_ref[...] = jnp.zeros_like(acc_ref)

    xb = xb_ref[...]
    hg = jnp.dot(xb, wg_ref[0], preferred_element_type=F32)
    hu = jnp.dot(xb, wu_ref[0], preferred_element_type=F32)
    lane = lax.broadcasted_iota(jnp.int32, c_ref.shape, 1)
    c = jnp.sum(jnp.where(lane == e, c_ref[...], 0.0), axis=-1, keepdims=True)
    act = hg * (1.0 / (1.0 + jnp.exp(-hg))) * hu
    acc_ref[...] += jnp.dot((act * c).astype(BF16), wd_ref[0], preferred_element_type=F32)

    @pl.when(e == MOE_EXPERTS - 1)
    def _():
        h = DEEPNORM_ALPHA * x_ref[...] + acc_ref[...]
        o_ref[...] = _layer_norm_rows(h, g_ref[...], b_ref[...])


def _moe(x1, comb, wg, wu, wd, g2, b2):
    n = x1.shape[0]
    tok = lambda i, e: (i, 0)
    row = lambda i, e: (0, 0)
    return pl.pallas_call(
        _moe_kernel,
        grid=(n // TM_MOE, MOE_EXPERTS),
        in_specs=[pl.BlockSpec((TM_MOE, D_MODEL), tok),
                  pl.BlockSpec((TM_MOE, LANES), tok),
                  pl.BlockSpec((1, D_MODEL, D_EXPERT), lambda i, e: (e, 0, 0)),
                  pl.BlockSpec((1, D_MODEL, D_EXPERT), lambda i, e: (e, 0, 0)),
                  pl.BlockSpec((1, D_EXPERT, D_MODEL), lambda i, e: (e, 0, 0)),
                  pl.BlockSpec((1, D_MODEL), row),
                  pl.BlockSpec((1, D_MODEL), row)],
        out_specs=pl.BlockSpec((TM_MOE, D_MODEL), tok),
        out_shape=jax.ShapeDtypeStruct((n, D_MODEL), F32),
        scratch_shapes=[pltpu.VMEM((TM_MOE, D_MODEL), BF16),
                        pltpu.VMEM((TM_MOE, D_MODEL), F32)],
        compiler_params=pltpu.CompilerParams(dimension_semantics=("parallel", "arbitrary"),
                                             vmem_limit_bytes=VMEM_LIMIT),
        name="moe_experts_ln",
    )(x1, comb, wg, wu, wd, g2, b2)


def _router_weights(w_rg, b_rg, w_re, b_re):
    wt = jnp.zeros((ROUTER_ROWS, D_MODEL), F32)
    wt = wt.at[:MOE_GROUPS].set(w_rg.T.astype(F32))
    wt = wt.at[8:8 + MOE_EXPERTS].set(w_re.T.astype(F32))
    bt = jnp.zeros((ROUTER_ROWS,), F32)
    bt = bt.at[:MOE_GROUPS].set(b_rg.astype(F32))
    bt = bt.at[8:8 + MOE_EXPERTS].set(b_re.astype(F32))
    hi, lo = _split_bf16(wt)
    return hi, lo, jnp.broadcast_to(bt[:, None], (ROUTER_ROWS, LANES))


def kernel(x, w_in, w_out, na_rel_bias, sgu_ln_g, sgu_ln_b, sgu_w, sgu_b, mix_norm_g, ln1_g, ln1_b, router_group_w, router_group_b, router_expert_w, router_expert_b, expert_w_gate, expert_w_up, expert_w_down, ln2_g, ln2_b):
    batch, seq, d = x.shape
    n = batch * seq
    xf = x.reshape(n, d).astype(F32)
    row = lambda a: a.astype(F32).reshape(1, -1)
    for l in range(DEPTH):
        q, k, v, gu, gv = _in_proj(xf, w_in[l].astype(BF16))
        a = _na_attention(q, k, v, _na_bias_table(na_rel_bias[l]), row(mix_norm_g[l, :D_NA]),
                          batch, seq)
        bs_full = jnp.repeat(sgu_b[l].astype(F32).T, SGU_GROUP_DIM, axis=1)
        s = _sgu(gu, gv, row(sgu_ln_g[l]), row(sgu_ln_b[l]), sgu_w[l].astype(BF16), bs_full,
                 row(mix_norm_g[l, D_NA:]))
        rw_hi, rw_lo, rb = _router_weights(router_group_w[l], router_group_b[l],
                                           router_expert_w[l], router_expert_b[l])
        x1, comb = _out_proj(a, s, xf, w_out[l].astype(BF16), row(ln1_g[l]), row(ln1_b[l]),
                             rw_hi, rw_lo, rb)
        xf = _moe(x1, comb, expert_w_gate[l].astype(BF16), expert_w_up[l].astype(BF16),
                  expert_w_down[l].astype(BF16), row(ln2_g[l]), row(ln2_b[l]))
    return xf.reshape(batch, seq, d).astype(x.dtype)
```

```python
import functools
import math

import numpy as np
import jax
import jax.numpy as jnp
from jax import lax
from jax.experimental import pallas as pl
from jax.experimental.pallas import tpu as pltpu

F32 = jnp.float32
BF16 = jnp.bfloat16

D_MODEL = 1024
DEPTH = 4
GRID_W = 64
NA_HEADS = 8
NA_HEAD_DIM = 64
NA_WIN_ROWS = 8
NA_WIN_COLS = 16
D_NA = NA_HEADS * NA_HEAD_DIM
SGU_GROUPS = 8
SGU_GROUP_DIM = 64
SGU_CHUNK = 128
D_SGU = SGU_GROUPS * SGU_GROUP_DIM
D_IN = 3 * D_NA + 2 * D_SGU
MOE_GROUPS = 4
MOE_EXPERTS_PER_GROUP = 4
MOE_EXPERTS = MOE_GROUPS * MOE_EXPERTS_PER_GROUP
D_EXPERT = 256
DEEPNORM_ALPHA = (2 * DEPTH) ** 0.25
LN_EPS = 1e-5

LANES = 128
HEAD_PAIRS = NA_HEADS // 2
NEG_BIG = -1e30
BAND_KEYS = NA_WIN_ROWS * GRID_W
BIAS_TILES = 7
ROUTER_ROWS = 32

TM_PROJ = 512
NA_ROWS_PER_STEP = 8
NA_ROWS_IN_FLIGHT = 4
NA_STRIP = 16
TM_SGU = 512
MOE_BLOCK = 512
MOE_CHUNK = 128
MOE_MAX_CHUNKS = MOE_BLOCK // MOE_CHUNK + MOE_GROUPS - 1
MOE_SORTED_ROWS = MOE_MAX_CHUNKS * MOE_CHUNK
VMEM_LIMIT = 56 * 1024 * 1024


def _gelu_tanh(x):
    c = math.sqrt(2.0 / math.pi)
    return x * (0.5 * (1.0 + jnp.tanh(c * (x + 0.044715 * (x * x * x)))))


def _layer_norm_rows(h, g, b):
    mu = jnp.mean(h, axis=-1, keepdims=True)
    hc = h - mu
    var = jnp.mean(hc * hc, axis=-1, keepdims=True)
    return hc * lax.rsqrt(var + LN_EPS) * g + b


def _in_proj_kernel(x_ref, w_ref, q_ref, k_ref, v_ref, u_ref, vs_ref):
    xb = x_ref[...].astype(BF16)

    def mm(j):
        return jnp.dot(xb, w_ref[:, j * D_NA:(j + 1) * D_NA], preferred_element_type=F32)

    q_ref[...] = (mm(0) * (NA_HEAD_DIM ** -0.5)).astype(BF16)
    k_ref[...] = mm(1).astype(BF16)
    v_ref[...] = mm(2).astype(BF16)
    u_ref[...] = _gelu_tanh(mm(3))
    vs_ref[...] = _gelu_tanh(mm(4))


def _in_proj(x, w_in_bf16):
    n = x.shape[0]
    tok = lambda i: (i, 0)
    return pl.pallas_call(
        _in_proj_kernel,
        grid=(n // TM_PROJ,),
        in_specs=[pl.BlockSpec((TM_PROJ, D_MODEL), tok),
                  pl.BlockSpec((D_MODEL, D_IN), lambda i: (0, 0))],
        out_specs=[pl.BlockSpec((TM_PROJ, D_NA), tok)] * 5,
        out_shape=[jax.ShapeDtypeStruct((n, D_NA), BF16)] * 3
                  + [jax.ShapeDtypeStruct((n, D_SGU), F32)] * 2,
        compiler_params=pltpu.CompilerParams(dimension_semantics=("parallel",),
                                             vmem_limit_bytes=VMEM_LIMIT),
        name="in_proj",
    )(x, w_in_bf16)


def _na_bias_table(rel_bias):
    cols = np.arange(GRID_W)
    col_start = np.clip(cols - NA_WIN_COLS // 2, 0, GRID_W - NA_WIN_COLS)
    kc = np.arange(GRID_W)
    valid = (kc[None, :] >= col_start[:, None]) & (kc[None, :] < col_start[:, None] + NA_WIN_COLS)
    dc = kc[None, :] - cols[:, None] + NA_WIN_COLS - 1
    onehot = (dc[None] == np.arange(2 * NA_WIN_COLS - 1)[:, None, None]) & valid[None]
    f = jnp.einsum('had,dck->hack', rel_bias.astype(F32), jnp.asarray(onehot, F32),
                   precision=lax.Precision.HIGHEST)
    f = jnp.where(valid[None, None], f, NEG_BIG)
    n_off = 2 * NA_WIN_ROWS - 1
    f = jnp.transpose(f, (0, 2, 1, 3)).reshape(NA_HEADS, GRID_W, n_off * GRID_W)
    g = f.reshape(HEAD_PAIRS, 2 * GRID_W, n_off * GRID_W)
    width = BIAS_TILES * LANES
    tabs = jnp.stack([g[:, :, :width], g[:, :, GRID_W:GRID_W + width]])
    tabs = tabs.reshape(2, HEAD_PAIRS, 2 * GRID_W, BIAS_TILES, LANES)
    return jnp.transpose(tabs, (0, 1, 3, 2, 4))


def _na_kernel(q_ref, k_ref, v_ref, t_ref, g_ref, o_ref, s_all, p_all, a_ref):
    rows = k_ref.shape[0] // GRID_W
    blk = pl.program_id(1)
    lane = lax.broadcasted_iota(jnp.int32, (GRID_W, LANES), 1)
    first_head = lane < NA_HEAD_DIM
    key_tiles = BAND_KEYS // LANES

    def one_row(rr, slot):
        s_ref = s_all.at[slot]
        p_ref = p_all.at[slot]
        r = blk * NA_ROWS_PER_STEP + rr
        rs = jnp.clip(r - NA_WIN_ROWS // 2, 0, rows - NA_WIN_ROWS)
        a0 = rs - r + (NA_WIN_ROWS - 1)
        par = a0 % 2
        j0 = a0 // 2
        q_off = pl.multiple_of(rr * GRID_W, GRID_W)
        k_off = pl.multiple_of(rs * GRID_W, GRID_W)
        for hp in range(HEAD_PAIRS):
            cs = slice(hp * LANES, (hp + 1) * LANES)
            qp = q_ref[pl.ds(q_off, GRID_W), cs]
            kp = k_ref[pl.ds(k_off, BAND_KEYS), cs]
            zero = jnp.zeros_like(qp)
            q2 = jnp.concatenate([jnp.where(first_head, qp, zero),
                                  jnp.where(first_head, zero, qp)], axis=0)
            s_ref[hp] = lax.dot_general(q2, kp, (((1,), (1,)), ((), ())),
                                        preferred_element_type=F32)
        for hp in range(HEAD_PAIRS):
            cs = slice(hp * LANES, (hp + 1) * LANES)
            inv_l = []
            for ch in range(2 * GRID_W // NA_STRIP):
                rsl = slice(ch * NA_STRIP, (ch + 1) * NA_STRIP)
                bias = jnp.concatenate([t_ref[par, hp, j0 + t, rsl, :] for t in range(key_tiles)],
                                       axis=1)
                sc = s_ref[hp, rsl, :] + bias
                m = jnp.max(sc, axis=1, keepdims=True)
                p = jnp.exp(sc - m)
                inv_l.append(1.0 / jnp.sum(p, axis=1, keepdims=True))
                p_ref[hp, rsl, :] = p.astype(BF16)
            vp = v_ref[pl.ds(k_off, BAND_KEYS), cs]
            o = jnp.dot(p_ref[hp], vp, preferred_element_type=F32)
            o = jnp.concatenate([o[ch * NA_STRIP:(ch + 1) * NA_STRIP] * inv_l[ch]
                                 for ch in range(len(inv_l))], axis=0)
            a_ref[pl.ds(q_off, GRID_W), cs] = jnp.where(first_head, o[:GRID_W], o[GRID_W:])

    def row_group(i, carry):
        for slot in range(NA_ROWS_IN_FLIGHT):
            one_row(i * NA_ROWS_IN_FLIGHT + slot, slot)
        return carry

    lax.fori_loop(0, NA_ROWS_PER_STEP // NA_ROWS_IN_FLIGHT, row_group, 0)

    a = a_ref[...]
    ms = jnp.mean(a * a, axis=-1, keepdims=True)
    o_ref[...] = (a * lax.rsqrt(ms + LN_EPS) * g_ref[...]).astype(o_ref.dtype)


def _na_attention(q, k, v, table, gain, batch, seq):
    n = q.shape[0]
    steps = seq // (GRID_W * NA_ROWS_PER_STEP)
    tq = NA_ROWS_PER_STEP * GRID_W
    return pl.pallas_call(
        _na_kernel,
        grid=(batch, steps),
        in_specs=[pl.BlockSpec((tq, D_NA), lambda b, i: (b * steps + i, 0)),
                  pl.BlockSpec((seq, D_NA), lambda b, i: (b, 0)),
                  pl.BlockSpec((seq, D_NA), lambda b, i: (b, 0)),
                  pl.BlockSpec(table.shape, lambda b, i: (0, 0, 0, 0, 0)),
                  pl.BlockSpec((1, D_NA), lambda b, i: (0, 0))],
        out_specs=pl.BlockSpec((tq, D_NA), lambda b, i: (b * steps + i, 0)),
        out_shape=jax.ShapeDtypeStruct((n, D_NA), BF16),
        scratch_shapes=[pltpu.VMEM((NA_ROWS_IN_FLIGHT, HEAD_PAIRS, 2 * GRID_W, BAND_KEYS), F32),
                        pltpu.VMEM((NA_ROWS_IN_FLIGHT, HEAD_PAIRS, 2 * GRID_W, BAND_KEYS), BF16),
                        pltpu.VMEM((tq, D_NA), F32)],
        compiler_params=pltpu.CompilerParams(dimension_semantics=("parallel", "parallel"),
                                             vmem_limit_bytes=VMEM_LIMIT),
        name="na_attention",
    )(q, k, v, table, gain)


def _sgu_kernel(u_ref, vs_ref, lng_ref, lnb_ref, ws_ref, bs_ref, g_ref, o_ref):
    lane = lax.broadcasted_iota(jnp.int32, (SGU_CHUNK, LANES), 1)
    first = lane < SGU_GROUP_DIM
    inv = 1.0 / SGU_GROUP_DIM

    def seg_mean(t):
        lo = jnp.sum(jnp.where(first, t, 0.0), axis=-1, keepdims=True)
        hi = jnp.sum(jnp.where(first, 0.0, t), axis=-1, keepdims=True)
        return jnp.where(first, lo, hi) * inv

    for c in range(TM_SGU // SGU_CHUNK):
        rs = slice(c * SGU_CHUNK, (c + 1) * SGU_CHUNK)
        tiles = []
        for j in range(D_SGU // LANES):
            cs = slice(j * LANES, (j + 1) * LANES)
            xt = vs_ref[rs, cs]
            xc = xt - seg_mean(xt)
            var = seg_mean(xc * xc)
            y = (xc * lax.rsqrt(var + LN_EPS) * lng_ref[:, cs] + lnb_ref[:, cs]).astype(BF16)
            m0 = jnp.dot(ws_ref[2 * j], y, preferred_element_type=F32)
            m1 = jnp.dot(ws_ref[2 * j + 1], y, preferred_element_type=F32)
            mixed = jnp.where(first, m0, m1) + bs_ref[:, cs]
            tiles.append(u_ref[rs, cs] * mixed)
        so = jnp.concatenate(tiles, axis=1)
        ms = jnp.mean(so * so, axis=-1, keepdims=True)
        o_ref[rs, :] = (so * lax.rsqrt(ms + LN_EPS) * g_ref[...]).astype(o_ref.dtype)


def _sgu(gu, gv, ln_g, ln_b, w_s_bf16, bs_full, gain):
    n = gu.shape[0]
    tok = lambda i: (i, 0)
    row = lambda i: (0, 0)
    return pl.pallas_call(
        _sgu_kernel,
        grid=(n // TM_SGU,),
        in_specs=[pl.BlockSpec((TM_SGU, D_SGU), tok),
                  pl.BlockSpec((TM_SGU, D_SGU), tok),
                  pl.BlockSpec((1, D_SGU), row),
                  pl.BlockSpec((1, D_SGU), row),
                  pl.BlockSpec((SGU_GROUPS, SGU_CHUNK, SGU_CHUNK), lambda i: (0, 0, 0)),
                  pl.BlockSpec((SGU_CHUNK, D_SGU), row),
                  pl.BlockSpec((1, D_SGU), row)],
        out_specs=pl.BlockSpec((TM_SGU, D_SGU), tok),
        out_shape=jax.ShapeDtypeStruct((n, D_SGU), BF16),
        compiler_params=pltpu.CompilerParams(dimension_semantics=("parallel",),
                                             vmem_limit_bytes=VMEM_LIMIT),
        name="spatial_gating",
    )(gu, gv, ln_g, ln_b, w_s_bf16, bs_full, gain)


def _split_bf16(a):
    hi = a.astype(BF16)
    lo = (a - hi.astype(F32)).astype(BF16)
    return hi, lo


def _out_proj_kernel(a_ref, s_ref, x_ref, w_ref, g_ref, b_ref, rwh_ref, rwl_ref, rb_ref, tri_ref,
                     x1_ref, rt_ref, r_ref, tab_ref):
    mix = jnp.dot(a_ref[...], w_ref[:D_NA, :], preferred_element_type=F32)
    mix = mix + jnp.dot(s_ref[...], w_ref[D_NA:, :], preferred_element_type=F32)
    x1 = _layer_norm_rows(DEEPNORM_ALPHA * x_ref[...] + mix, g_ref[...], b_ref[...])
    x1_ref[...] = x1

    xh, xl = _split_bf16(x1)
    nt = (((1,), (1,)), ((), ()))
    lg = lax.dot_general(rwh_ref[...], xh, nt, preferred_element_type=F32)
    lg = lg + lax.dot_general(rwh_ref[...], xl, nt, preferred_element_type=F32)
    lg = lg + lax.dot_general(rwl_ref[...], xh, nt, preferred_element_type=F32)
    lg = lg + rb_ref[:, 0:1]

    gl = [lg[g:g + 1, :] for g in range(MOE_GROUPS)]
    gmax = functools.reduce(jnp.maximum, gl)
    gidx = jnp.full(gmax.shape, MOE_GROUPS - 1, jnp.int32)
    for g in range(MOE_GROUPS - 2, -1, -1):
        gidx = jnp.where(gl[g] == gmax, g, gidx)
    denom = functools.reduce(lambda a, b: a + b, [jnp.exp(t - gmax) for t in gl])
    gate = 1.0 / denom

    def expert_logit(i):
        rows_ = [lg[8 + MOE_EXPERTS_PER_GROUP * g + i:9 + MOE_EXPERTS_PER_GROUP * g + i, :]
                 for g in range(MOE_GROUPS)]
        sel = rows_[MOE_GROUPS - 1]
        for g in range(MOE_GROUPS - 2, -1, -1):
            sel = jnp.where(gidx == g, rows_[g], sel)
        return sel

    el = [expert_logit(i) for i in range(MOE_EXPERTS_PER_GROUP)]
    v1 = functools.reduce(jnp.maximum, el)
    i1 = jnp.full(v1.shape, MOE_EXPERTS_PER_GROUP - 1, jnp.int32)
    for i in range(MOE_EXPERTS_PER_GROUP - 2, -1, -1):
        i1 = jnp.where(el[i] == v1, i, i1)
    rest = [jnp.where(i1 == i, -jnp.inf, el[i]) for i in range(MOE_EXPERTS_PER_GROUP)]
    v2 = functools.reduce(jnp.maximum, rest)
    i2 = jnp.full(v2.shape, MOE_EXPERTS_PER_GROUP - 1, jnp.int32)
    for i in range(MOE_EXPERTS_PER_GROUP - 2, -1, -1):
        i2 = jnp.where((rest[i] == v2) & (i1 != i), i, i2)
    e2 = jnp.exp(v2 - v1)
    w1 = 1.0 / (1.0 + e2)
    w2 = e2 * w1
    within = [jnp.where(i1 == i, w1, 0.0) + jnp.where(i2 == i, w2, 0.0)
              for i in range(MOE_EXPERTS_PER_GROUP)]
    tm = x1.shape[0]
    cw = [within[i] * gate for i in range(MOE_EXPERTS_PER_GROUP)]

    onehot = [jnp.where(gidx == g, 1.0, 0.0) for g in range(MOE_GROUPS)]
    oh_mat = jnp.concatenate(onehot + [jnp.zeros((16 - MOE_GROUPS, tm), F32)], axis=0)
    before = jnp.dot(oh_mat.astype(BF16), tri_ref[...], preferred_element_type=F32)
    rank = functools.reduce(lambda a, b: a + b,
                            [onehot[g] * before[g:g + 1, :] for g in range(MOE_GROUPS)])
    ends = []
    start_tok = jnp.zeros_like(rank)
    end = jnp.zeros((1, 1), F32)
    for g in range(MOE_GROUPS):
        count = jnp.sum(onehot[g], axis=1, keepdims=True)
        start_tok = start_tok + onehot[g] * end
        end = end + jnp.floor((count + (MOE_CHUNK - 1)) * (1.0 / MOE_CHUNK))
        ends.append(end)
    dest = rank + MOE_CHUNK * start_tok
    rout_t = jnp.concatenate([dest] + cw + [gidx.astype(F32), jnp.zeros((2, tm), F32)], axis=0)
    rt_ref[...] = rout_t
    r_ref[...] = jnp.concatenate([rout_t, jnp.zeros((LANES - 8, tm), F32)], axis=0).T

    c = lax.broadcasted_iota(jnp.int32, (8, LANES), 1).astype(F32)
    grp = functools.reduce(lambda a, b: a + b,
                           [jnp.where(c >= ends[g], 1.0, 0.0) for g in range(MOE_GROUPS - 1)])
    tab = jnp.where(c == MOE_MAX_CHUNKS, ends[MOE_GROUPS - 1], grp)
    tab_ref[0] = tab.astype(jnp.int32)


def _out_proj(a, s, x, w_out_bf16, g1, b1, rw_hi, rw_lo, rb, tri):
    n = x.shape[0]
    tok = lambda i: (i, 0)
    row = lambda i: (0, 0)
    return pl.pallas_call(
        _out_proj_kernel,
        grid=(n // MOE_BLOCK,),
        in_specs=[pl.BlockSpec((MOE_BLOCK, D_NA), tok),
                  pl.BlockSpec((MOE_BLOCK, D_SGU), tok),
                  pl.BlockSpec((MOE_BLOCK, D_MODEL), tok),
                  pl.BlockSpec((D_MODEL, D_MODEL), row),
                  pl.BlockSpec((1, D_MODEL), row),
                  pl.BlockSpec((1, D_MODEL), row),
                  pl.BlockSpec((ROUTER_ROWS, D_MODEL), row),
                  pl.BlockSpec((ROUTER_ROWS, D_MODEL), row),
                  pl.BlockSpec((ROUTER_ROWS, LANES), row),
                  pl.BlockSpec((MOE_BLOCK, MOE_BLOCK), row)],
        out_specs=[pl.BlockSpec((MOE_BLOCK, D_MODEL), tok),
                   pl.BlockSpec((8, MOE_BLOCK), lambda i: (0, i)),
                   pl.BlockSpec((MOE_BLOCK, LANES), tok),
                   pl.BlockSpec((1, 8, LANES), lambda i: (i, 0, 0))],
        out_shape=[jax.ShapeDtypeStruct((n, D_MODEL), F32),
                   jax.ShapeDtypeStruct((8, n), F32),
                   jax.ShapeDtypeStruct((n, LANES), F32),
                   jax.ShapeDtypeStruct((n // MOE_BLOCK, 8, LANES), jnp.int32)],
        compiler_params=pltpu.CompilerParams(dimension_semantics=("parallel",),
                                             vmem_limit_bytes=VMEM_LIMIT),
        name="out_proj_ln_router",
    )(a, s, x, w_out_bf16, g1, b1, rw_hi, rw_lo, rb, tri)


def _moe_kernel(tab_ref, x_ref, rt_ref, r_ref, wg_ref, wu_ref, wd_ref, g_ref, b_ref, o_ref,
                xs_ref, ys_ref, cw_ref):
    blk = pl.program_id(0)
    x1 = x_ref[...]
    dest_row = rt_ref[0:1, :].astype(jnp.int32)
    rows_i = lax.broadcasted_iota(jnp.int32, (MOE_SORTED_ROWS, MOE_BLOCK), 0)
    p_in = jnp.where(rows_i == dest_row, 1.0, 0.0).astype(BF16)
    xs_ref[...] = jnp.dot(p_in, x1.astype(BF16), preferred_element_type=F32).astype(BF16)
    r_hi, r_lo = _split_bf16(r_ref[...])
    cw_ref[...] = (jnp.dot(p_in, r_hi, preferred_element_type=F32)
                   + jnp.dot(p_in, r_lo, preferred_element_type=F32))
    ys_ref[...] = jnp.zeros_like(ys_ref)

    def chunk(c, carry):
        g = tab_ref[blk, c]
        off = pl.multiple_of(c * MOE_CHUNK, MOE_CHUNK)
        xs = xs_ref[pl.ds(off, MOE_CHUNK), :]
        cw = cw_ref[pl.ds(off, MOE_CHUNK), :]
        y = jnp.zeros((MOE_CHUNK, D_MODEL), F32)
        for i in range(MOE_EXPERTS_PER_GROUP):
            e = g * MOE_EXPERTS_PER_GROUP + i
            hg = jnp.dot(xs, wg_ref[e], preferred_element_type=F32)
            hu = jnp.dot(xs, wu_ref[e], preferred_element_type=F32)
            act = hg * (1.0 / (1.0 + jnp.exp(-hg))) * hu * cw[:, 1 + i:2 + i]
            y = y + jnp.dot(act.astype(BF16), wd_ref[e], preferred_element_type=F32)
        ys_ref[pl.ds(off, MOE_CHUNK), :] = y.astype(BF16)
        return carry

    lax.fori_loop(0, tab_ref[blk, MOE_MAX_CHUNKS], chunk, 0)

    dest_col = r_ref[:, 0:1].astype(jnp.int32)
    cols_i = lax.broadcasted_iota(jnp.int32, (MOE_BLOCK, MOE_SORTED_ROWS), 1)
    p_out = jnp.where(cols_i == dest_col, 1.0, 0.0).astype(BF16)
    y_tok = jnp.dot(p_out, ys_ref[...], preferred_element_type=F32)
    o_ref[...] = _layer_norm_rows(DEEPNORM_ALPHA * x1 + y_tok, g_ref[...], b_ref[...])


def _moe(tab, x1, rout_t, rout, wg, wu, wd, g2, b2):
    n = x1.shape[0]
    tok = lambda i, t: (i, 0)
    row = lambda i, t: (0, 0)
    whole = lambda i, t: (0, 0, 0)
    resident = pl.Buffered(1)
    return pl.pallas_call(
        _moe_kernel,
        grid_spec=pltpu.PrefetchScalarGridSpec(
            num_scalar_prefetch=1,
            grid=(n // MOE_BLOCK,),
            in_specs=[pl.BlockSpec((MOE_BLOCK, D_MODEL), tok),
                      pl.BlockSpec((8, MOE_BLOCK), lambda i, t: (0, i)),
                      pl.BlockSpec((MOE_BLOCK, LANES), tok),
                      pl.BlockSpec((MOE_EXPERTS, D_MODEL, D_EXPERT), whole, pipeline_mode=resident),
                      pl.BlockSpec((MOE_EXPERTS, D_MODEL, D_EXPERT), whole, pipeline_mode=resident),
                      pl.BlockSpec((MOE_EXPERTS, D_EXPERT, D_MODEL), whole, pipeline_mode=resident),
                      pl.BlockSpec((1, D_MODEL), row),
                      pl.BlockSpec((1, D_MODEL), row)],
            out_specs=pl.BlockSpec((MOE_BLOCK, D_MODEL), tok),
            scratch_shapes=[pltpu.VMEM((MOE_SORTED_ROWS, D_MODEL), BF16),
                            pltpu.VMEM((MOE_SORTED_ROWS, D_MODEL), BF16),
                            pltpu.VMEM((MOE_SORTED_ROWS, LANES), F32)]),
        out_shape=jax.ShapeDtypeStruct((n, D_MODEL), F32),
        compiler_params=pltpu.CompilerParams(dimension_semantics=("parallel",),
                                             vmem_limit_bytes=VMEM_LIMIT),
        name="moe_experts_ln",
    )(tab, x1, rout_t, rout, wg, wu, wd, g2, b2)


def _router_weights(w_rg, b_rg, w_re, b_re):
    wt = jnp.zeros((ROUTER_ROWS, D_MODEL), F32)
    wt = wt.at[:MOE_GROUPS].set(w_rg.T.astype(F32))
    wt = wt.at[8:8 + MOE_EXPERTS].set(w_re.T.astype(F32))
    bt = jnp.zeros((ROUTER_ROWS,), F32)
    bt = bt.at[:MOE_GROUPS].set(b_rg.astype(F32))
    bt = bt.at[8:8 + MOE_EXPERTS].set(b_re.astype(F32))
    hi, lo = _split_bf16(wt)
    return hi, lo, jnp.broadcast_to(bt[:, None], (ROUTER_ROWS, LANES))


def kernel(x, w_in, w_out, na_rel_bias, sgu_ln_g, sgu_ln_b, sgu_w, sgu_b, mix_norm_g, ln1_g, ln1_b, router_group_w, router_group_b, router_expert_w, router_expert_b, expert_w_gate, expert_w_up, expert_w_down, ln2_g, ln2_b):
    batch, seq, d = x.shape
    n = batch * seq
    xf = x.reshape(n, d).astype(F32)
    row = lambda a: a.astype(F32).reshape(1, -1)
    tri = jnp.asarray(np.triu(np.ones((MOE_BLOCK, MOE_BLOCK), np.float32), k=1), BF16)
    for l in range(DEPTH):
        q, k, v, gu, gv = _in_proj(xf, w_in[l].astype(BF16))
        a = _na_attention(q, k, v, _na_bias_table(na_rel_bias[l]), row(mix_norm_g[l, :D_NA]),
                          batch, seq)
        bs_full = jnp.repeat(sgu_b[l].astype(F32).T, SGU_GROUP_DIM, axis=1)
        s = _sgu(gu, gv, row(sgu_ln_g[l]), row(sgu_ln_b[l]), sgu_w[l].astype(BF16), bs_full,
                 row(mix_norm_g[l, D_NA:]))
        rw_hi, rw_lo, rb = _router_weights(router_group_w[l], router_group_b[l],
                                           router_expert_w[l], router_expert_b[l])
        x1, rout_t, rout, tab = _out_proj(a, s, xf, w_out[l].astype(BF16), row(ln1_g[l]),
                                          row(ln1_b[l]), rw_hi, rw_lo, rb, tri)
        xf = _moe(tab[:, 0, :8], x1, rout_t, rout, expert_w_gate[l].astype(BF16),
                  expert_w_up[l].astype(BF16), expert_w_down[l].astype(BF16),
                  row(ln2_g[l]), row(ln2_b[l]))
    return xf.reshape(batch, seq, d).astype(x.dtype)
```

```python
import functools
import math

import numpy as np
import jax
import jax.numpy as jnp
from jax import lax
from jax.experimental import pallas as pl
from jax.experimental.pallas import tpu as pltpu

F32 = jnp.float32
BF16 = jnp.bfloat16

D_MODEL = 1024
DEPTH = 4
GRID_W = 64
NA_HEADS = 8
NA_HEAD_DIM = 64
NA_WIN_ROWS = 8
NA_WIN_COLS = 16
D_NA = NA_HEADS * NA_HEAD_DIM
SGU_GROUPS = 8
SGU_GROUP_DIM = 64
SGU_CHUNK = 128
D_SGU = SGU_GROUPS * SGU_GROUP_DIM
D_IN = 3 * D_NA + 2 * D_SGU
MOE_GROUPS = 4
MOE_EXPERTS_PER_GROUP = 4
MOE_EXPERTS = MOE_GROUPS * MOE_EXPERTS_PER_GROUP
D_EXPERT = 256
DEEPNORM_ALPHA = (2 * DEPTH) ** 0.25
LN_EPS = 1e-5

LANES = 128
HEAD_PAIRS = NA_HEADS // 2
NEG_BIG = -1e30
BAND_KEYS = NA_WIN_ROWS * GRID_W
BIAS_TILES = 7
ROUTER_ROWS = 32

TM_PROJ = 512
NA_ROWS_PER_STEP = 8
NA_ROWS_IN_FLIGHT = 4
NA_STRIP = 16
TM_SGU = 512
MOE_BLOCK = 512
MOE_CHUNK = 128
MOE_MAX_CHUNKS = MOE_BLOCK // MOE_CHUNK + MOE_GROUPS - 1
MOE_SORTED_ROWS = MOE_MAX_CHUNKS * MOE_CHUNK
TAB_START = 0
TAB_COUNT = MOE_GROUPS
VMEM_LIMIT = 56 * 1024 * 1024


def _gelu_tanh(x):
    c = math.sqrt(2.0 / math.pi)
    return x * (0.5 * (1.0 + jnp.tanh(c * (x + 0.044715 * (x * x * x)))))


def _layer_norm_rows(h, g, b):
    mu = jnp.mean(h, axis=-1, keepdims=True)
    hc = h - mu
    var = jnp.mean(hc * hc, axis=-1, keepdims=True)
    return hc * lax.rsqrt(var + LN_EPS) * g + b


def _in_proj_kernel(x_ref, w_ref, q_ref, k_ref, v_ref, u_ref, vs_ref):
    xb = x_ref[...].astype(BF16)

    def mm(j):
        return jnp.dot(xb, w_ref[:, j * D_NA:(j + 1) * D_NA], preferred_element_type=F32)

    q_ref[...] = (mm(0) * (NA_HEAD_DIM ** -0.5)).astype(BF16)
    k_ref[...] = mm(1).astype(BF16)
    v_ref[...] = mm(2).astype(BF16)
    u_ref[...] = _gelu_tanh(mm(3))
    vs_ref[...] = _gelu_tanh(mm(4))


def _in_proj(x, w_in_bf16):
    n = x.shape[0]
    tok = lambda i: (i, 0)
    return pl.pallas_call(
        _in_proj_kernel,
        grid=(n // TM_PROJ,),
        in_specs=[pl.BlockSpec((TM_PROJ, D_MODEL), tok),
                  pl.BlockSpec((D_MODEL, D_IN), lambda i: (0, 0))],
        out_specs=[pl.BlockSpec((TM_PROJ, D_NA), tok)] * 5,
        out_shape=[jax.ShapeDtypeStruct((n, D_NA), BF16)] * 3
                  + [jax.ShapeDtypeStruct((n, D_SGU), F32)] * 2,
        compiler_params=pltpu.CompilerParams(dimension_semantics=("parallel",),
                                             vmem_limit_bytes=VMEM_LIMIT),
        name="in_proj",
    )(x, w_in_bf16)


def _na_bias_table(rel_bias):
    cols = np.arange(GRID_W)
    col_start = np.clip(cols - NA_WIN_COLS // 2, 0, GRID_W - NA_WIN_COLS)
    kc = np.arange(GRID_W)
    valid = (kc[None, :] >= col_start[:, None]) & (kc[None, :] < col_start[:, None] + NA_WIN_COLS)
    dc = kc[None, :] - cols[:, None] + NA_WIN_COLS - 1
    onehot = (dc[None] == np.arange(2 * NA_WIN_COLS - 1)[:, None, None]) & valid[None]
    f = jnp.einsum('had,dck->hack', rel_bias.astype(F32), jnp.asarray(onehot, F32),
                   precision=lax.Precision.HIGHEST)
    f = jnp.where(valid[None, None], f, NEG_BIG)
    n_off = 2 * NA_WIN_ROWS - 1
    f = jnp.transpose(f, (0, 2, 1, 3)).reshape(NA_HEADS, GRID_W, n_off * GRID_W)
    g = f.reshape(HEAD_PAIRS, 2 * GRID_W, n_off * GRID_W)
    width = BIAS_TILES * LANES
    tabs = jnp.stack([g[:, :, :width], g[:, :, GRID_W:GRID_W + width]])
    tabs = tabs.reshape(2, HEAD_PAIRS, 2 * GRID_W, BIAS_TILES, LANES)
    return jnp.transpose(tabs, (0, 1, 3, 2, 4))


def _na_kernel(q_ref, k_ref, v_ref, t_ref, g_ref, o_ref, s_all, p_all, a_ref):
    rows = k_ref.shape[0] // GRID_W
    blk = pl.program_id(1)
    lane = lax.broadcasted_iota(jnp.int32, (GRID_W, LANES), 1)
    first_head = lane < NA_HEAD_DIM
    key_tiles = BAND_KEYS // LANES

    def one_row(rr, slot):
        s_ref = s_all.at[slot]
        p_ref = p_all.at[slot]
        r = blk * NA_ROWS_PER_STEP + rr
        rs = jnp.clip(r - NA_WIN_ROWS // 2, 0, rows - NA_WIN_ROWS)
        a0 = rs - r + (NA_WIN_ROWS - 1)
        par = a0 % 2
        j0 = a0 // 2
        q_off = pl.multiple_of(rr * GRID_W, GRID_W)
        k_off = pl.multiple_of(rs * GRID_W, GRID_W)
        for hp in range(HEAD_PAIRS):
            cs = slice(hp * LANES, (hp + 1) * LANES)
            qp = q_ref[pl.ds(q_off, GRID_W), cs]
            kp = k_ref[pl.ds(k_off, BAND_KEYS), cs]
            zero = jnp.zeros_like(qp)
            q2 = jnp.concatenate([jnp.where(first_head, qp, zero),
                                  jnp.where(first_head, zero, qp)], axis=0)
            s_ref[hp] = lax.dot_general(q2, kp, (((1,), (1,)), ((), ())),
                                        preferred_element_type=F32)
        for hp in range(HEAD_PAIRS):
            cs = slice(hp * LANES, (hp + 1) * LANES)
            inv_l = []
            for ch in range(2 * GRID_W // NA_STRIP):
                rsl = slice(ch * NA_STRIP, (ch + 1) * NA_STRIP)
                bias = jnp.concatenate([t_ref[par, hp, j0 + t, rsl, :] for t in range(key_tiles)],
                                       axis=1)
                sc = s_ref[hp, rsl, :] + bias
                m = jnp.max(sc, axis=1, keepdims=True)
                p = jnp.exp(sc - m)
                inv_l.append(1.0 / jnp.sum(p, axis=1, keepdims=True))
                p_ref[hp, rsl, :] = p.astype(BF16)
            vp = v_ref[pl.ds(k_off, BAND_KEYS), cs]
            o = jnp.dot(p_ref[hp], vp, preferred_element_type=F32)
            o = jnp.concatenate([o[ch * NA_STRIP:(ch + 1) * NA_STRIP] * inv_l[ch]
                                 for ch in range(len(inv_l))], axis=0)
            a_ref[pl.ds(q_off, GRID_W), cs] = jnp.where(first_head, o[:GRID_W], o[GRID_W:])

    def row_group(i, carry):
        for slot in range(NA_ROWS_IN_FLIGHT):
            one_row(i * NA_ROWS_IN_FLIGHT + slot, slot)
        return carry

    lax.fori_loop(0, NA_ROWS_PER_STEP // NA_ROWS_IN_FLIGHT, row_group, 0)

    a = a_ref[...]
    ms = jnp.mean(a * a, axis=-1, keepdims=True)
    o_ref[...] = (a * lax.rsqrt(ms + LN_EPS) * g_ref[...]).astype(o_ref.dtype)


def _na_attention(q, k, v, table, gain, batch, seq):
    n = q.shape[0]
    steps = seq // (GRID_W * NA_ROWS_PER_STEP)
    tq = NA_ROWS_PER_STEP * GRID_W
    return pl.pallas_call(
        _na_kernel,
        grid=(batch, steps),
        in_specs=[pl.BlockSpec((tq, D_NA), lambda b, i: (b * steps + i, 0)),
                  pl.BlockSpec((seq, D_NA), lambda b, i: (b, 0)),
                  pl.BlockSpec((seq, D_NA), lambda b, i: (b, 0)),
                  pl.BlockSpec(table.shape, lambda b, i: (0, 0, 0, 0, 0)),
                  pl.BlockSpec((1, D_NA), lambda b, i: (0, 0))],
        out_specs=pl.BlockSpec((tq, D_NA), lambda b, i: (b * steps + i, 0)),
        out_shape=jax.ShapeDtypeStruct((n, D_NA), BF16),
        scratch_shapes=[pltpu.VMEM((NA_ROWS_IN_FLIGHT, HEAD_PAIRS, 2 * GRID_W, BAND_KEYS), F32),
                        pltpu.VMEM((NA_ROWS_IN_FLIGHT, HEAD_PAIRS, 2 * GRID_W, BAND_KEYS), BF16),
                        pltpu.VMEM((tq, D_NA), F32)],
        compiler_params=pltpu.CompilerParams(dimension_semantics=("parallel", "parallel"),
                                             vmem_limit_bytes=VMEM_LIMIT),
        name="na_attention",
    )(q, k, v, table, gain)


def _sgu_kernel(u_ref, vs_ref, lng_ref, lnb_ref, ws_ref, bs_ref, g_ref, seg_ref, o_ref):
    lane = lax.broadcasted_iota(jnp.int32, (SGU_CHUNK, LANES), 1)
    first = lane < SGU_GROUP_DIM
    inv = 1.0 / SGU_GROUP_DIM

    n_tiles = D_SGU // LANES
    n_rows = TM_SGU * n_tiles

    def seg_mean(t):
        hi, lo = _split_bf16(t)
        s = jnp.dot(jnp.concatenate([hi, lo], axis=0), seg_ref[...], preferred_element_type=F32)
        return (s[:n_rows] + s[n_rows:]) * inv

    x = jnp.concatenate([vs_ref[:, j * LANES:(j + 1) * LANES] for j in range(n_tiles)], axis=0)
    xc = x - seg_mean(x)
    xn = xc * lax.rsqrt(seg_mean(xc * xc) + LN_EPS)

    for c in range(TM_SGU // SGU_CHUNK):
        rs = slice(c * SGU_CHUNK, (c + 1) * SGU_CHUNK)
        tiles = []
        for j in range(n_tiles):
            cs = slice(j * LANES, (j + 1) * LANES)
            r0 = j * TM_SGU + c * SGU_CHUNK
            y = (xn[r0:r0 + SGU_CHUNK] * lng_ref[:, cs] + lnb_ref[:, cs]).astype(BF16)
            m = jnp.dot(jnp.concatenate([ws_ref[2 * j], ws_ref[2 * j + 1]], axis=0), y,
                        preferred_element_type=F32)
            mixed = jnp.where(first, m[:SGU_CHUNK], m[SGU_CHUNK:]) + bs_ref[:, cs]
            tiles.append(u_ref[rs, cs] * mixed)
        so = jnp.concatenate(tiles, axis=1)
        ms = jnp.mean(so * so, axis=-1, keepdims=True)
        o_ref[rs, :] = (so * lax.rsqrt(ms + LN_EPS) * g_ref[...]).astype(o_ref.dtype)


def _sgu(gu, gv, ln_g, ln_b, w_s_bf16, bs_full, gain):
    n = gu.shape[0]
    same_group = np.arange(LANES)[:, None] // SGU_GROUP_DIM == np.arange(LANES)[None, :] // SGU_GROUP_DIM
    seg_ones = jnp.asarray(same_group, BF16)
    tok = lambda i: (i, 0)
    row = lambda i: (0, 0)
    return pl.pallas_call(
        _sgu_kernel,
        grid=(n // TM_SGU,),
        in_specs=[pl.BlockSpec((TM_SGU, D_SGU), tok),
                  pl.BlockSpec((TM_SGU, D_SGU), tok),
                  pl.BlockSpec((1, D_SGU), row),
                  pl.BlockSpec((1, D_SGU), row),
                  pl.BlockSpec((SGU_GROUPS, SGU_CHUNK, SGU_CHUNK), lambda i: (0, 0, 0)),
                  pl.BlockSpec((SGU_CHUNK, D_SGU), row),
                  pl.BlockSpec((1, D_SGU), row),
                  pl.BlockSpec((LANES, LANES), row)],
        out_specs=pl.BlockSpec((TM_SGU, D_SGU), tok),
        out_shape=jax.ShapeDtypeStruct((n, D_SGU), BF16),
        compiler_params=pltpu.CompilerParams(dimension_semantics=("parallel",),
                                             vmem_limit_bytes=VMEM_LIMIT),
        name="spatial_gating",
    )(gu, gv, ln_g, ln_b, w_s_bf16, bs_full, gain, seg_ones)


def _split_bf16(a):
    hi = a.astype(BF16)
    lo = (a - hi.astype(F32)).astype(BF16)
    return hi, lo


def _out_proj_kernel(a_ref, s_ref, x_ref, w_ref, g_ref, b_ref, rwh_ref, rwl_ref, rb_ref, tri_ref,
                     x1_ref, rt_ref, r_ref, tab_ref):
    mix = jnp.dot(a_ref[...], w_ref[:D_NA, :], preferred_element_type=F32)
    mix = mix + jnp.dot(s_ref[...], w_ref[D_NA:, :], preferred_element_type=F32)
    x1 = _layer_norm_rows(DEEPNORM_ALPHA * x_ref[...] + mix, g_ref[...], b_ref[...])
    x1_ref[...] = x1

    xh, xl = _split_bf16(x1)
    nt = (((1,), (1,)), ((), ()))
    lg = lax.dot_general(rwh_ref[...], xh, nt, preferred_element_type=F32)
    lg = lg + lax.dot_general(rwh_ref[...], xl, nt, preferred_element_type=F32)
    lg = lg + lax.dot_general(rwl_ref[...], xh, nt, preferred_element_type=F32)
    lg = lg + rb_ref[:, 0:1]

    gl = [lg[g:g + 1, :] for g in range(MOE_GROUPS)]
    gmax = functools.reduce(jnp.maximum, gl)
    gidx = jnp.full(gmax.shape, MOE_GROUPS - 1, jnp.int32)
    for g in range(MOE_GROUPS - 2, -1, -1):
        gidx = jnp.where(gl[g] == gmax, g, gidx)
    denom = functools.reduce(lambda a, b: a + b, [jnp.exp(t - gmax) for t in gl])
    gate = 1.0 / denom

    def expert_logit(i):
        rows_ = [lg[8 + MOE_EXPERTS_PER_GROUP * g + i:9 + MOE_EXPERTS_PER_GROUP * g + i, :]
                 for g in range(MOE_GROUPS)]
        sel = rows_[MOE_GROUPS - 1]
        for g in range(MOE_GROUPS - 2, -1, -1):
            sel = jnp.where(gidx == g, rows_[g], sel)
        return sel

    el = [expert_logit(i) for i in range(MOE_EXPERTS_PER_GROUP)]
    v1 = functools.reduce(jnp.maximum, el)
    i1 = jnp.full(v1.shape, MOE_EXPERTS_PER_GROUP - 1, jnp.int32)
    for i in range(MOE_EXPERTS_PER_GROUP - 2, -1, -1):
        i1 = jnp.where(el[i] == v1, i, i1)
    rest = [jnp.where(i1 == i, -jnp.inf, el[i]) for i in range(MOE_EXPERTS_PER_GROUP)]
    v2 = functools.reduce(jnp.maximum, rest)
    i2 = jnp.full(v2.shape, MOE_EXPERTS_PER_GROUP - 1, jnp.int32)
    for i in range(MOE_EXPERTS_PER_GROUP - 2, -1, -1):
        i2 = jnp.where((rest[i] == v2) & (i1 != i), i, i2)
    e2 = jnp.exp(v2 - v1)
    w1 = 1.0 / (1.0 + e2)
    w2 = e2 * w1
    within = [jnp.where(i1 == i, w1, 0.0) + jnp.where(i2 == i, w2, 0.0)
              for i in range(MOE_EXPERTS_PER_GROUP)]
    tm = x1.shape[0]
    cw = [within[i] * gate for i in range(MOE_EXPERTS_PER_GROUP)]

    onehot = [jnp.where(gidx == g, 1.0, 0.0) for g in range(MOE_GROUPS)]
    oh_mat = jnp.concatenate(onehot + [jnp.zeros((16 - MOE_GROUPS, tm), F32)], axis=0)
    before = jnp.dot(oh_mat.astype(BF16), tri_ref[...], preferred_element_type=F32)
    rank = functools.reduce(lambda a, b: a + b,
                            [onehot[g] * before[g:g + 1, :] for g in range(MOE_GROUPS)])
    starts, n_chunks = [], []
    start_tok = jnp.zeros_like(rank)
    end = jnp.zeros((1, 1), F32)
    for g in range(MOE_GROUPS):
        count = jnp.sum(onehot[g], axis=1, keepdims=True)
        start_tok = start_tok + onehot[g] * end
        starts.append(end)
        n_chunks.append(jnp.floor((count + (MOE_CHUNK - 1)) * (1.0 / MOE_CHUNK)))
        end = end + n_chunks[g]
    dest = rank + MOE_CHUNK * start_tok
    rout_t = jnp.concatenate([dest] + cw + [gidx.astype(F32), jnp.zeros((2, tm), F32)], axis=0)
    rt_ref[...] = rout_t
    r_ref[...] = jnp.concatenate([rout_t, jnp.zeros((LANES - 8, tm), F32)], axis=0).T

    c = lax.broadcasted_iota(jnp.int32, (8, LANES), 1)
    tab = jnp.zeros((8, LANES), F32)
    for g in range(MOE_GROUPS):
        tab = jnp.where(c == TAB_START + g, starts[g], tab)
        tab = jnp.where(c == TAB_COUNT + g, n_chunks[g], tab)
    tab_ref[0] = tab.astype(jnp.int32)


def _out_proj(a, s, x, w_out_bf16, g1, b1, rw_hi, rw_lo, rb, tri):
    n = x.shape[0]
    tok = lambda i: (i, 0)
    row = lambda i: (0, 0)
    return pl.pallas_call(
        _out_proj_kernel,
        grid=(n // MOE_BLOCK,),
        in_specs=[pl.BlockSpec((MOE_BLOCK, D_NA), tok),
                  pl.BlockSpec((MOE_BLOCK, D_SGU), tok),
                  pl.BlockSpec((MOE_BLOCK, D_MODEL), tok),
                  pl.BlockSpec((D_MODEL, D_MODEL), row),
                  pl.BlockSpec((1, D_MODEL), row),
                  pl.BlockSpec((1, D_MODEL), row),
                  pl.BlockSpec((ROUTER_ROWS, D_MODEL), row),
                  pl.BlockSpec((ROUTER_ROWS, D_MODEL), row),
                  pl.BlockSpec((ROUTER_ROWS, LANES), row),
                  pl.BlockSpec((MOE_BLOCK, MOE_BLOCK), row)],
        out_specs=[pl.BlockSpec((MOE_BLOCK, D_MODEL), tok),
                   pl.BlockSpec((8, MOE_BLOCK), lambda i: (0, i)),
                   pl.BlockSpec((MOE_BLOCK, LANES), tok),
                   pl.BlockSpec((1, 8, LANES), lambda i: (i, 0, 0))],
        out_shape=[jax.ShapeDtypeStruct((n, D_MODEL), F32),
                   jax.ShapeDtypeStruct((8, n), F32),
                   jax.ShapeDtypeStruct((n, LANES), F32),
                   jax.ShapeDtypeStruct((n // MOE_BLOCK, 8, LANES), jnp.int32)],
        compiler_params=pltpu.CompilerParams(dimension_semantics=("parallel",),
                                             vmem_limit_bytes=VMEM_LIMIT),
        name="out_proj_ln_router",
    )(a, s, x, w_out_bf16, g1, b1, rw_hi, rw_lo, rb, tri)


def _moe_kernel(tab_ref, x_ref, rt_ref, r_ref, wg_ref, wu_ref, wd_ref, g_ref, b_ref, o_ref,
                xs_ref, ys_ref, cw_ref):
    blk = pl.program_id(0)
    x1 = x_ref[...]
    dest_row = rt_ref[0:1, :].astype(jnp.int32)
    rows_i = lax.broadcasted_iota(jnp.int32, (MOE_SORTED_ROWS, MOE_BLOCK), 0)
    p_in = jnp.where(rows_i == dest_row, 1.0, 0.0).astype(BF16)
    xs_ref[...] = jnp.dot(p_in, x1.astype(BF16), preferred_element_type=F32).astype(BF16)
    r_hi, r_lo = _split_bf16(r_ref[...])
    cw2 = jnp.dot(p_in, jnp.concatenate([r_hi, r_lo], axis=1), preferred_element_type=F32)
    cw_ref[...] = cw2[:, :LANES] + cw2[:, LANES:]
    ys_ref[...] = jnp.zeros_like(ys_ref)

    def run_experts(g, first_chunk, n_rows):
        off = pl.multiple_of(first_chunk * MOE_CHUNK, MOE_CHUNK)
        xs = xs_ref[pl.ds(off, n_rows), :]
        cw = cw_ref[pl.ds(off, n_rows), :]
        y = None
        for i in range(MOE_EXPERTS_PER_GROUP):
            e = g * MOE_EXPERTS_PER_GROUP + i
            hg = jnp.dot(xs, wg_ref[e], preferred_element_type=F32)
            hu = jnp.dot(xs, wu_ref[e], preferred_element_type=F32)
            act = hg * (1.0 / (1.0 + jnp.exp(-hg))) * hu * cw[:, 1 + i:2 + i]
            d = jnp.dot(act.astype(BF16), wd_ref[e], preferred_element_type=F32)
            y = d if y is None else y + d
        ys_ref[pl.ds(off, n_rows), :] = y.astype(BF16)

    def group(g, carry):
        start = tab_ref[blk, TAB_START + g]
        n_chunks = tab_ref[blk, TAB_COUNT + g]

        def pair(j, c):
            run_experts(g, start + 2 * j, 2 * MOE_CHUNK)
            return c

        lax.fori_loop(0, n_chunks // 2, pair, 0)

        @pl.when(n_chunks % 2 == 1)
        def _():
            run_experts(g, start + n_chunks - 1, MOE_CHUNK)

        return carry

    lax.fori_loop(0, MOE_GROUPS, group, 0)

    dest_col = r_ref[:, 0:1].astype(jnp.int32)
    cols_i = lax.broadcasted_iota(jnp.int32, (MOE_BLOCK, MOE_SORTED_ROWS), 1)
    p_out = jnp.where(cols_i == dest_col, 1.0, 0.0).astype(BF16)
    y_tok = jnp.dot(p_out, ys_ref[...], preferred_element_type=F32)
    o_ref[...] = _layer_norm_rows(DEEPNORM_ALPHA * x1 + y_tok, g_ref[...], b_ref[...])


def _moe(tab, x1, rout_t, rout, wg, wu, wd, g2, b2):
    n = x1.shape[0]
    tok = lambda i, t: (i, 0)
    row = lambda i, t: (0, 0)
    whole = lambda i, t: (0, 0, 0)
    resident = pl.Buffered(1)
    return pl.pallas_call(
        _moe_kernel,
        grid_spec=pltpu.PrefetchScalarGridSpec(
            num_scalar_prefetch=1,
            grid=(n // MOE_BLOCK,),
            in_specs=[pl.BlockSpec((MOE_BLOCK, D_MODEL), tok),
                      pl.BlockSpec((8, MOE_BLOCK), lambda i, t: (0, i)),
                      pl.BlockSpec((MOE_BLOCK, LANES), tok),
                      pl.BlockSpec((MOE_EXPERTS, D_MODEL, D_EXPERT), whole, pipeline_mode=resident),
                      pl.BlockSpec((MOE_EXPERTS, D_MODEL, D_EXPERT), whole, pipeline_mode=resident),
                      pl.BlockSpec((MOE_EXPERTS, D_EXPERT, D_MODEL), whole, pipeline_mode=resident),
                      pl.BlockSpec((1, D_MODEL), row),
                      pl.BlockSpec((1, D_MODEL), row)],
            out_specs=pl.BlockSpec((MOE_BLOCK, D_MODEL), tok),
            scratch_shapes=[pltpu.VMEM((MOE_SORTED_ROWS, D_MODEL), BF16),
                            pltpu.VMEM((MOE_SORTED_ROWS, D_MODEL), BF16),
                            pltpu.VMEM((MOE_SORTED_ROWS, LANES), F32)]),
        out_shape=jax.ShapeDtypeStruct((n, D_MODEL), F32),
        compiler_params=pltpu.CompilerParams(dimension_semantics=("parallel",),
                                             vmem_limit_bytes=VMEM_LIMIT),
        name="moe_experts_ln",
    )(tab, x1, rout_t, rout, wg, wu, wd, g2, b2)


def _router_weights(w_rg, b_rg, w_re, b_re):
    wt = jnp.zeros((ROUTER_ROWS, D_MODEL), F32)
    wt = wt.at[:MOE_GROUPS].set(w_rg.T.astype(F32))
    wt = wt.at[8:8 + MOE_EXPERTS].set(w_re.T.astype(F32))
    bt = jnp.zeros((ROUTER_ROWS,), F32)
    bt = bt.at[:MOE_GROUPS].set(b_rg.astype(F32))
    bt = bt.at[8:8 + MOE_EXPERTS].set(b_re.astype(F32))
    hi, lo = _split_bf16(wt)
    return hi, lo, jnp.broadcast_to(bt[:, None], (ROUTER_ROWS, LANES))


def kernel(x, w_in, w_out, na_rel_bias, sgu_ln_g, sgu_ln_b, sgu_w, sgu_b, mix_norm_g, ln1_g, ln1_b, router_group_w, router_group_b, router_expert_w, router_expert_b, expert_w_gate, expert_w_up, expert_w_down, ln2_g, ln2_b):
    batch, seq, d = x.shape
    n = batch * seq
    xf = x.reshape(n, d).astype(F32)
    row = lambda a: a.astype(F32).reshape(1, -1)
    tri = jnp.asarray(np.triu(np.ones((MOE_BLOCK, MOE_BLOCK), np.float32), k=1), BF16)
    for l in range(DEPTH):
        q, k, v, gu, gv = _in_proj(xf, w_in[l].astype(BF16))
        a = _na_attention(q, k, v, _na_bias_table(na_rel_bias[l]), row(mix_norm_g[l, :D_NA]),
                          batch, seq)
        bs_full = jnp.repeat(sgu_b[l].astype(F32).T, SGU_GROUP_DIM, axis=1)
        s = _sgu(gu, gv, row(sgu_ln_g[l]), row(sgu_ln_b[l]), sgu_w[l].astype(BF16), bs_full,
                 row(mix_norm_g[l, D_NA:]))
        rw_hi, rw_lo, rb = _router_weights(router_group_w[l], router_group_b[l],
                                           router_expert_w[l], router_expert_b[l])
        x1, rout_t, rout, tab = _out_proj(a, s, xf, w_out[l].astype(BF16), row(ln1_g[l]),
                                          row(ln1_b[l]), rw_hi, rw_lo, rb, tri)
        xf = _moe(tab[:, 0, :8], x1, rout_t, rout, expert_w_gate[l].astype(BF16),
                  expert_w_up[l].astype(BF16), expert_w_down[l].astype(BF16),
                  row(ln2_g[l]), row(ln2_b[l]))
    return xf.reshape(batch, seq, d).astype(x.dtype)
```

```python
import functools
import math

import numpy as np
import jax
import jax.numpy as jnp
from jax import lax
from jax.experimental import pallas as pl
from jax.experimental.pallas import tpu as pltpu

F32 = jnp.float32
BF16 = jnp.bfloat16

D_MODEL = 1024
DEPTH = 4
GRID_W = 64
NA_HEADS = 8
NA_HEAD_DIM = 64
NA_WIN_ROWS = 8
NA_WIN_COLS = 16
D_NA = NA_HEADS * NA_HEAD_DIM
SGU_GROUPS = 8
SGU_GROUP_DIM = 64
SGU_CHUNK = 128
D_SGU = SGU_GROUPS * SGU_GROUP_DIM
D_IN = 3 * D_NA + 2 * D_SGU
MOE_GROUPS = 4
MOE_EXPERTS_PER_GROUP = 4
MOE_EXPERTS = MOE_GROUPS * MOE_EXPERTS_PER_GROUP
D_EXPERT = 256
DEEPNORM_ALPHA = (2 * DEPTH) ** 0.25
LN_EPS = 1e-5

LANES = 128
HEAD_PAIRS = NA_HEADS // 2
NEG_BIG = -1e30
BAND_KEYS = NA_WIN_ROWS * GRID_W
BIAS_TILES = 7
ROUTER_ROWS = 32

TM_PROJ = 512
NA_ROWS_PER_STEP = 8
NA_ROWS_IN_FLIGHT = 4
NA_STRIP = 16
TM_SGU = 512
MOE_BLOCK = 512
MOE_CHUNK = 128
MOE_MAX_CHUNKS = MOE_BLOCK // MOE_CHUNK + MOE_GROUPS - 1
MOE_SORTED_ROWS = MOE_MAX_CHUNKS * MOE_CHUNK
OUT_BLOCKS_PER_STEP = 2
TAB_START = 0
TAB_COUNT = MOE_GROUPS
VMEM_LIMIT = 56 * 1024 * 1024


def _gelu_tanh(x):
    c = math.sqrt(2.0 / math.pi)
    return x * (0.5 * (1.0 + jnp.tanh(c * (x + 0.044715 * (x * x * x)))))


def _layer_norm_rows(h, g, b):
    mu = jnp.mean(h, axis=-1, keepdims=True)
    hc = h - mu
    var = jnp.mean(hc * hc, axis=-1, keepdims=True)
    return hc * lax.rsqrt(var + LN_EPS) * g + b


def _in_proj_kernel(x_ref, w_ref, q_ref, k_ref, v_ref, u_ref, vs_ref, wb_ref):
    @pl.when(pl.program_id(0) == 0)
    def _():
        wb_ref[...] = w_ref[0].astype(BF16)

    xb = x_ref[...].astype(BF16)

    def mm(j):
        return jnp.dot(xb, wb_ref[:, j * D_NA:(j + 1) * D_NA], preferred_element_type=F32)

    q_ref[...] = (mm(0) * (NA_HEAD_DIM ** -0.5)).astype(BF16)
    k_ref[...] = mm(1).astype(BF16)
    v_ref[...] = mm(2).astype(BF16)
    u_ref[...] = _gelu_tanh(mm(3)).astype(BF16)
    vs_ref[...] = _gelu_tanh(mm(4)).astype(BF16)


def _in_proj(x, w_in, layer):
    n = x.shape[0]
    tok = lambda i: (i, 0)
    return pl.pallas_call(
        _in_proj_kernel,
        grid=(n // TM_PROJ,),
        in_specs=[pl.BlockSpec((TM_PROJ, D_MODEL), tok),
                  pl.BlockSpec((1, D_MODEL, D_IN), lambda i: (layer, 0, 0),
                               pipeline_mode=pl.Buffered(1))],
        out_specs=[pl.BlockSpec((TM_PROJ, D_NA), tok)] * 5,
        out_shape=[jax.ShapeDtypeStruct((n, D_NA), BF16)] * 5,
        scratch_shapes=[pltpu.VMEM((D_MODEL, D_IN), BF16)],
        compiler_params=pltpu.CompilerParams(dimension_semantics=("arbitrary",),
                                             vmem_limit_bytes=VMEM_LIMIT),
        name="in_proj",
    )(x, w_in)


def _na_bias_table(rel_bias):
    cols = np.arange(GRID_W)
    col_start = np.clip(cols - NA_WIN_COLS // 2, 0, GRID_W - NA_WIN_COLS)
    kc = np.arange(GRID_W)
    valid = (kc[None, :] >= col_start[:, None]) & (kc[None, :] < col_start[:, None] + NA_WIN_COLS)
    dc = kc[None, :] - cols[:, None] + NA_WIN_COLS - 1
    onehot = (dc[None] == np.arange(2 * NA_WIN_COLS - 1)[:, None, None]) & valid[None]
    f = jnp.einsum('had,dck->hack', rel_bias.astype(F32), jnp.asarray(onehot, F32),
                   precision=lax.Precision.HIGHEST)
    f = jnp.where(valid[None, None], f, NEG_BIG)
    n_off = 2 * NA_WIN_ROWS - 1
    f = jnp.transpose(f, (0, 2, 1, 3)).reshape(NA_HEADS, GRID_W, n_off * GRID_W)
    g = f.reshape(HEAD_PAIRS, 2 * GRID_W, n_off * GRID_W)
    width = BIAS_TILES * LANES
    tabs = jnp.stack([g[:, :, :width], g[:, :, GRID_W:GRID_W + width]])
    tabs = tabs.reshape(2, HEAD_PAIRS, 2 * GRID_W, BIAS_TILES, LANES)
    return jnp.transpose(tabs, (0, 1, 3, 2, 4))


def _na_kernel(q_ref, k_ref, v_ref, t_ref, g_ref, o_ref, s_all, p_all, a_ref):
    rows = k_ref.shape[0] // GRID_W
    blk = pl.program_id(1)
    lane = lax.broadcasted_iota(jnp.int32, (GRID_W, LANES), 1)
    first_head = lane < NA_HEAD_DIM
    key_tiles = BAND_KEYS // LANES

    def one_row(rr, slot):
        s_ref = s_all.at[slot]
        p_ref = p_all.at[slot]
        r = blk * NA_ROWS_PER_STEP + rr
        rs = jnp.clip(r - NA_WIN_ROWS // 2, 0, rows - NA_WIN_ROWS)
        a0 = rs - r + (NA_WIN_ROWS - 1)
        par = a0 % 2
        j0 = a0 // 2
        q_off = pl.multiple_of(rr * GRID_W, GRID_W)
        k_off = pl.multiple_of(rs * GRID_W, GRID_W)
        for hp in range(HEAD_PAIRS):
            cs = slice(hp * LANES, (hp + 1) * LANES)
            qp = q_ref[pl.ds(q_off, GRID_W), cs]
            kp = k_ref[pl.ds(k_off, BAND_KEYS), cs]
            zero = jnp.zeros_like(qp)
            q2 = jnp.concatenate([jnp.where(first_head, qp, zero),
                                  jnp.where(first_head, zero, qp)], axis=0)
            s_ref[hp] = lax.dot_general(q2, kp, (((1,), (1,)), ((), ())),
                                        preferred_element_type=F32)
        for hp in range(HEAD_PAIRS):
            cs = slice(hp * LANES, (hp + 1) * LANES)
            inv_l = []
            for ch in range(2 * GRID_W // NA_STRIP):
                rsl = slice(ch * NA_STRIP, (ch + 1) * NA_STRIP)
                bias = jnp.concatenate([t_ref[par, hp, j0 + t, rsl, :] for t in range(key_tiles)],
                                       axis=1)
                sc = s_ref[hp, rsl, :] + bias
                m = jnp.max(sc, axis=1, keepdims=True)
                p = jnp.exp(sc - m)
                inv_l.append(1.0 / jnp.sum(p, axis=1, keepdims=True))
                p_ref[hp, rsl, :] = p.astype(BF16)
            vp = v_ref[pl.ds(k_off, BAND_KEYS), cs]
            o = jnp.dot(p_ref[hp], vp, preferred_element_type=F32)
            o = jnp.concatenate([o[ch * NA_STRIP:(ch + 1) * NA_STRIP] * inv_l[ch]
                                 for ch in range(len(inv_l))], axis=0)
            a_ref[pl.ds(q_off, GRID_W), cs] = jnp.where(first_head, o[:GRID_W], o[GRID_W:])

    def row_group(i, carry):
        for slot in range(NA_ROWS_IN_FLIGHT):
            one_row(i * NA_ROWS_IN_FLIGHT + slot, slot)
        return carry

    lax.fori_loop(0, NA_ROWS_PER_STEP // NA_ROWS_IN_FLIGHT, row_group, 0)

    a = a_ref[...]
    ms = jnp.mean(a * a, axis=-1, keepdims=True)
    o_ref[...] = (a * lax.rsqrt(ms + LN_EPS) * g_ref[...]).astype(o_ref.dtype)


def _na_attention(q, k, v, table, gain, batch, seq):
    n = q.shape[0]
    steps = seq // (GRID_W * NA_ROWS_PER_STEP)
    tq = NA_ROWS_PER_STEP * GRID_W
    return pl.pallas_call(
        _na_kernel,
        grid=(batch, steps),
        in_specs=[pl.BlockSpec((tq, D_NA), lambda b, i: (b * steps + i, 0)),
                  pl.BlockSpec((seq, D_NA), lambda b, i: (b, 0)),
                  pl.BlockSpec((seq, D_NA), lambda b, i: (b, 0)),
                  pl.BlockSpec(table.shape, lambda b, i: (0, 0, 0, 0, 0)),
                  pl.BlockSpec((1, D_NA), lambda b, i: (0, 0))],
        out_specs=pl.BlockSpec((tq, D_NA), lambda b, i: (b * steps + i, 0)),
        out_shape=jax.ShapeDtypeStruct((n, D_NA), BF16),
        scratch_shapes=[pltpu.VMEM((NA_ROWS_IN_FLIGHT, HEAD_PAIRS, 2 * GRID_W, BAND_KEYS), F32),
                        pltpu.VMEM((NA_ROWS_IN_FLIGHT, HEAD_PAIRS, 2 * GRID_W, BAND_KEYS), BF16),
                        pltpu.VMEM((tq, D_NA), F32)],
        compiler_params=pltpu.CompilerParams(dimension_semantics=("parallel", "parallel"),
                                             vmem_limit_bytes=VMEM_LIMIT),
        name="na_attention",
    )(q, k, v, table, gain)


def _sgu_kernel(u_ref, vs_ref, lng_ref, lnb_ref, ws_ref, bs_ref, g_ref, seg_ref, o_ref):
    lane = lax.broadcasted_iota(jnp.int32, (SGU_CHUNK, LANES), 1)
    first = lane < SGU_GROUP_DIM
    inv = 1.0 / SGU_GROUP_DIM

    n_tiles = D_SGU // LANES
    n_rows = TM_SGU * n_tiles

    def seg_mean(t):
        hi, lo = _split_bf16(t)
        s = jnp.dot(jnp.concatenate([hi, lo], axis=0), seg_ref[...], preferred_element_type=F32)
        return (s[:n_rows] + s[n_rows:]) * inv

    x = jnp.concatenate([vs_ref[:, j * LANES:(j + 1) * LANES] for j in range(n_tiles)],
                        axis=0).astype(F32)
    xc = x - seg_mean(x)
    xn = xc * lax.rsqrt(seg_mean(xc * xc) + LN_EPS)

    for c in range(TM_SGU // SGU_CHUNK):
        rs = slice(c * SGU_CHUNK, (c + 1) * SGU_CHUNK)
        tiles = []
        for j in range(n_tiles):
            cs = slice(j * LANES, (j + 1) * LANES)
            r0 = j * TM_SGU + c * SGU_CHUNK
            y = (xn[r0:r0 + SGU_CHUNK] * lng_ref[:, cs] + lnb_ref[:, cs]).astype(BF16)
            m = jnp.dot(jnp.concatenate([ws_ref[2 * j], ws_ref[2 * j + 1]], axis=0), y,
                        preferred_element_type=F32)
            mixed = jnp.where(first, m[:SGU_CHUNK], m[SGU_CHUNK:]) + bs_ref[:, cs]
            tiles.append(u_ref[rs, cs].astype(F32) * mixed)
        so = jnp.concatenate(tiles, axis=1)
        ms = jnp.mean(so * so, axis=-1, keepdims=True)
        o_ref[rs, :] = (so * lax.rsqrt(ms + LN_EPS) * g_ref[...]).astype(o_ref.dtype)


def _sgu(gu, gv, ln_g, ln_b, w_s_bf16, bs_full, gain):
    n = gu.shape[0]
    same_group = np.arange(LANES)[:, None] // SGU_GROUP_DIM == np.arange(LANES)[None, :] // SGU_GROUP_DIM
    seg_ones = jnp.asarray(same_group, BF16)
    tok = lambda i: (i, 0)
    row = lambda i: (0, 0)
    return pl.pallas_call(
        _sgu_kernel,
        grid=(n // TM_SGU,),
        in_specs=[pl.BlockSpec((TM_SGU, D_SGU), tok),
                  pl.BlockSpec((TM_SGU, D_SGU), tok),
                  pl.BlockSpec((1, D_SGU), row),
                  pl.BlockSpec((1, D_SGU), row),
                  pl.BlockSpec((SGU_GROUPS, SGU_CHUNK, SGU_CHUNK), lambda i: (0, 0, 0)),
                  pl.BlockSpec((SGU_CHUNK, D_SGU), row),
                  pl.BlockSpec((1, D_SGU), row),
                  pl.BlockSpec((LANES, LANES), row)],
        out_specs=pl.BlockSpec((TM_SGU, D_SGU), tok),
        out_shape=jax.ShapeDtypeStruct((n, D_SGU), BF16),
        compiler_params=pltpu.CompilerParams(dimension_semantics=("parallel",),
                                             vmem_limit_bytes=VMEM_LIMIT),
        name="spatial_gating",
    )(gu, gv, ln_g, ln_b, w_s_bf16, bs_full, gain, seg_ones)


def _split_bf16(a):
    hi = a.astype(BF16)
    lo = (a - hi.astype(F32)).astype(BF16)
    return hi, lo


def _route(lg, tri):
    gl = [lg[g:g + 1, :] for g in range(MOE_GROUPS)]
    gmax = functools.reduce(jnp.maximum, gl)
    gidx = jnp.full(gmax.shape, MOE_GROUPS - 1, jnp.int32)
    for g in range(MOE_GROUPS - 2, -1, -1):
        gidx = jnp.where(gl[g] == gmax, g, gidx)
    denom = functools.reduce(lambda a, b: a + b, [jnp.exp(t - gmax) for t in gl])
    gate = 1.0 / denom

    def expert_logit(i):
        rows_ = [lg[8 + MOE_EXPERTS_PER_GROUP * g + i:9 + MOE_EXPERTS_PER_GROUP * g + i, :]
                 for g in range(MOE_GROUPS)]
        sel = rows_[MOE_GROUPS - 1]
        for g in range(MOE_GROUPS - 2, -1, -1):
            sel = jnp.where(gidx == g, rows_[g], sel)
        return sel

    el = [expert_logit(i) for i in range(MOE_EXPERTS_PER_GROUP)]
    v1 = functools.reduce(jnp.maximum, el)
    i1 = jnp.full(v1.shape, MOE_EXPERTS_PER_GROUP - 1, jnp.int32)
    for i in range(MOE_EXPERTS_PER_GROUP - 2, -1, -1):
        i1 = jnp.where(el[i] == v1, i, i1)
    rest = [jnp.where(i1 == i, -jnp.inf, el[i]) for i in range(MOE_EXPERTS_PER_GROUP)]
    v2 = functools.reduce(jnp.maximum, rest)
    i2 = jnp.full(v2.shape, MOE_EXPERTS_PER_GROUP - 1, jnp.int32)
    for i in range(MOE_EXPERTS_PER_GROUP - 2, -1, -1):
        i2 = jnp.where((rest[i] == v2) & (i1 != i), i, i2)
    e2 = jnp.exp(v2 - v1)
    w1 = 1.0 / (1.0 + e2)
    w2 = e2 * w1
    within = [jnp.where(i1 == i, w1, 0.0) + jnp.where(i2 == i, w2, 0.0)
              for i in range(MOE_EXPERTS_PER_GROUP)]
    tm = MOE_BLOCK
    cw = [within[i] * gate for i in range(MOE_EXPERTS_PER_GROUP)]

    onehot = [jnp.where(gidx == g, 1.0, 0.0) for g in range(MOE_GROUPS)]
    oh_mat = jnp.concatenate(onehot + [jnp.zeros((16 - MOE_GROUPS, tm), F32)], axis=0)
    before = jnp.dot(oh_mat.astype(BF16), tri, preferred_element_type=F32)
    rank = functools.reduce(lambda a, b: a + b,
                            [onehot[g] * before[g:g + 1, :] for g in range(MOE_GROUPS)])
    starts, n_chunks = [], []
    start_tok = jnp.zeros_like(rank)
    end = jnp.zeros((1, 1), F32)
    for g in range(MOE_GROUPS):
        count = jnp.sum(onehot[g], axis=1, keepdims=True)
        start_tok = start_tok + onehot[g] * end
        starts.append(end)
        n_chunks.append(jnp.floor((count + (MOE_CHUNK - 1)) * (1.0 / MOE_CHUNK)))
        end = end + n_chunks[g]
    dest = rank + MOE_CHUNK * start_tok
    rout_t = jnp.concatenate([dest] + cw + [gidx.astype(F32), jnp.zeros((2, tm), F32)], axis=0)

    c = lax.broadcasted_iota(jnp.int32, (8, LANES), 1)
    tab = jnp.zeros((8, LANES), F32)
    for g in range(MOE_GROUPS):
        tab = jnp.where(c == TAB_START + g, starts[g], tab)
        tab = jnp.where(c == TAB_COUNT + g, n_chunks[g], tab)
    return rout_t, tab.astype(jnp.int32)


def _out_proj_kernel(a_ref, s_ref, x_ref, w_ref, g_ref, b_ref, rw_ref, rb_ref, tri_ref,
                     x1_ref, rt_ref, r_ref, tab_ref, wb_ref):
    @pl.when(pl.program_id(0) == 0)
    def _():
        wb_ref[...] = w_ref[0].astype(BF16)

    mixes = []
    for blk in range(OUT_BLOCKS_PER_STEP):
        rs = slice(blk * MOE_BLOCK, (blk + 1) * MOE_BLOCK)
        mix = jnp.dot(a_ref[rs, :], wb_ref[:D_NA, :], preferred_element_type=F32)
        mixes.append(mix + jnp.dot(s_ref[rs, :], wb_ref[D_NA:, :], preferred_element_type=F32))
    logits = []
    for blk in range(OUT_BLOCKS_PER_STEP):
        rs = slice(blk * MOE_BLOCK, (blk + 1) * MOE_BLOCK)
        x1 = _layer_norm_rows(DEEPNORM_ALPHA * x_ref[rs, :] + mixes[blk], g_ref[...], b_ref[...])
        x1_ref[rs, :] = x1
        xh, xl = _split_bf16(x1)
        logits.append(jnp.dot(xh, rw_ref[...], preferred_element_type=F32)
                      + jnp.dot(xl, rw_ref[...], preferred_element_type=F32))
    for blk in range(OUT_BLOCKS_PER_STEP):
        rs = slice(blk * MOE_BLOCK, (blk + 1) * MOE_BLOCK)
        lg_t = logits[blk].T
        lg = lg_t[:ROUTER_ROWS] + lg_t[ROUTER_ROWS:2 * ROUTER_ROWS] + rb_ref[:, 0:1]
        rout_t, tab = _route(lg, tri_ref[...])
        rt_ref[:, rs] = rout_t
        r_ref[rs, :] = jnp.concatenate([rout_t, jnp.zeros((LANES - 8, MOE_BLOCK), F32)], axis=0).T
        tab_ref[blk] = tab


def _out_proj(a, s, x, w_out, layer, g1, b1, rw, rb, tri):
    n = x.shape[0]
    tok = lambda i: (i, 0)
    row = lambda i: (0, 0)
    tm = OUT_BLOCKS_PER_STEP * MOE_BLOCK
    return pl.pallas_call(
        _out_proj_kernel,
        grid=(n // tm,),
        in_specs=[pl.BlockSpec((tm, D_NA), tok),
                  pl.BlockSpec((tm, D_SGU), tok),
                  pl.BlockSpec((tm, D_MODEL), tok),
                  pl.BlockSpec((1, D_MODEL, D_MODEL), lambda i: (layer, 0, 0),
                               pipeline_mode=pl.Buffered(1)),
                  pl.BlockSpec((1, D_MODEL), row),
                  pl.BlockSpec((1, D_MODEL), row),
                  pl.BlockSpec((D_MODEL, LANES), row),
                  pl.BlockSpec((ROUTER_ROWS, LANES), row),
                  pl.BlockSpec((MOE_BLOCK, MOE_BLOCK), row)],
        out_specs=[pl.BlockSpec((tm, D_MODEL), tok),
                   pl.BlockSpec((8, tm), lambda i: (0, i)),
                   pl.BlockSpec((tm, LANES), tok),
                   pl.BlockSpec((OUT_BLOCKS_PER_STEP, 8, LANES), lambda i: (i, 0, 0))],
        out_shape=[jax.ShapeDtypeStruct((n, D_MODEL), F32),
                   jax.ShapeDtypeStruct((8, n), F32),
                   jax.ShapeDtypeStruct((n, LANES), F32),
                   jax.ShapeDtypeStruct((n // MOE_BLOCK, 8, LANES), jnp.int32)],
        scratch_shapes=[pltpu.VMEM((D_MODEL, D_MODEL), BF16)],
        compiler_params=pltpu.CompilerParams(dimension_semantics=("arbitrary",),
                                             vmem_limit_bytes=VMEM_LIMIT),
        name="out_proj_ln_router",
    )(a, s, x, w_out, g1, b1, rw, rb, tri)


def _moe_kernel(tab_ref, x_ref, rt_ref, r_ref, wg_ref, wu_ref, wd_ref, g_ref, b_ref, o_ref,
                xs_ref, ys_ref, cw_ref):
    blk = pl.program_id(0)
    x1 = x_ref[...]
    dest_row = rt_ref[0:1, :].astype(jnp.int32)
    rows_i = lax.broadcasted_iota(jnp.int32, (MOE_SORTED_ROWS, MOE_BLOCK), 0)
    p_in = jnp.where(rows_i == dest_row, 1.0, 0.0).astype(BF16)
    xs_ref[...] = jnp.dot(p_in, x1.astype(BF16), preferred_element_type=F32).astype(BF16)
    r_hi, r_lo = _split_bf16(r_ref[...])
    cw2 = jnp.dot(p_in, jnp.concatenate([r_hi, r_lo], axis=1), preferred_element_type=F32)
    cw_ref[...] = cw2[:, :LANES] + cw2[:, LANES:]
    ys_ref[...] = jnp.zeros_like(ys_ref)

    def run_experts(g, first_chunk, n_rows):
        off = pl.multiple_of(first_chunk * MOE_CHUNK, MOE_CHUNK)
        xs = xs_ref[pl.ds(off, n_rows), :]
        cw = cw_ref[pl.ds(off, n_rows), :]
        y = None
        for i in range(MOE_EXPERTS_PER_GROUP):
            e = g * MOE_EXPERTS_PER_GROUP + i
            hg = jnp.dot(xs, wg_ref[e], preferred_element_type=F32)
            hu = jnp.dot(xs, wu_ref[e], preferred_element_type=F32)
            act = hg * (1.0 / (1.0 + jnp.exp(-hg))) * hu * cw[:, 1 + i:2 + i]
            d = jnp.dot(act.astype(BF16), wd_ref[e], preferred_element_type=F32)
            y = d if y is None else y + d
        ys_ref[pl.ds(off, n_rows), :] = y.astype(BF16)

    def group(g, carry):
        start = tab_ref[blk, TAB_START + g]
        n_chunks = tab_ref[blk, TAB_COUNT + g]

        def pair(j, c):
            run_experts(g, start + 2 * j, 2 * MOE_CHUNK)
            return c

        lax.fori_loop(0, n_chunks // 2, pair, 0)

        @pl.when(n_chunks % 2 == 1)
        def _():
            run_experts(g, start + n_chunks - 1, MOE_CHUNK)

        return carry

    lax.fori_loop(0, MOE_GROUPS, group, 0)

    dest_col = r_ref[:, 0:1].astype(jnp.int32)
    cols_i = lax.broadcasted_iota(jnp.int32, (MOE_BLOCK, MOE_SORTED_ROWS), 1)
    p_out = jnp.where(cols_i == dest_col, 1.0, 0.0).astype(BF16)
    y_tok = jnp.dot(p_out, ys_ref[...], preferred_element_type=F32)
    o_ref[...] = _layer_norm_rows(DEEPNORM_ALPHA * x1 + y_tok, g_ref[...], b_ref[...])


def _moe(tab, x1, rout_t, rout, wg, wu, wd, g2, b2):
    n = x1.shape[0]
    tok = lambda i, t: (i, 0)
    row = lambda i, t: (0, 0)
    whole = lambda i, t: (0, 0, 0)
    resident = pl.Buffered(1)
    return pl.pallas_call(
        _moe_kernel,
        grid_spec=pltpu.PrefetchScalarGridSpec(
            num_scalar_prefetch=1,
            grid=(n // MOE_BLOCK,),
            in_specs=[pl.BlockSpec((MOE_BLOCK, D_MODEL), tok),
                      pl.BlockSpec((8, MOE_BLOCK), lambda i, t: (0, i)),
                      pl.BlockSpec((MOE_BLOCK, LANES), tok),
                      pl.BlockSpec((MOE_EXPERTS, D_MODEL, D_EXPERT), whole, pipeline_mode=resident),
                      pl.BlockSpec((MOE_EXPERTS, D_MODEL, D_EXPERT), whole, pipeline_mode=resident),
                      pl.BlockSpec((MOE_EXPERTS, D_EXPERT, D_MODEL), whole, pipeline_mode=resident),
                      pl.BlockSpec((1, D_MODEL), row),
                      pl.BlockSpec((1, D_MODEL), row)],
            out_specs=pl.BlockSpec((MOE_BLOCK, D_MODEL), tok),
            scratch_shapes=[pltpu.VMEM((MOE_SORTED_ROWS, D_MODEL), BF16),
                            pltpu.VMEM((MOE_SORTED_ROWS, D_MODEL), BF16),
                            pltpu.VMEM((MOE_SORTED_ROWS, LANES), F32)]),
        out_shape=jax.ShapeDtypeStruct((n, D_MODEL), F32),
        compiler_params=pltpu.CompilerParams(dimension_semantics=("parallel",),
                                             vmem_limit_bytes=VMEM_LIMIT),
        name="moe_experts_ln",
    )(tab, x1, rout_t, rout, wg, wu, wd, g2, b2)


def _router_weights(w_rg, b_rg, w_re, b_re):
    wt = jnp.zeros((ROUTER_ROWS, D_MODEL), F32)
    wt = wt.at[:MOE_GROUPS].set(w_rg.T.astype(F32))
    wt = wt.at[8:8 + MOE_EXPERTS].set(w_re.T.astype(F32))
    bt = jnp.zeros((ROUTER_ROWS,), F32)
    bt = bt.at[:MOE_GROUPS].set(b_rg.astype(F32))
    bt = bt.at[8:8 + MOE_EXPERTS].set(b_re.astype(F32))
    hi, lo = _split_bf16(wt)
    rw = jnp.concatenate([hi.T, lo.T, jnp.zeros((D_MODEL, LANES - 2 * ROUTER_ROWS), BF16)], axis=1)
    return rw, jnp.broadcast_to(bt[:, None], (ROUTER_ROWS, LANES))


def kernel(x, w_in, w_out, na_rel_bias, sgu_ln_g, sgu_ln_b, sgu_w, sgu_b, mix_norm_g, ln1_g, ln1_b, router_group_w, router_group_b, router_expert_w, router_expert_b, expert_w_gate, expert_w_up, expert_w_down, ln2_g, ln2_b):
    batch, seq, d = x.shape
    n = batch * seq
    xf = x.reshape(n, d).astype(F32)
    row = lambda a: a.astype(F32).reshape(1, -1)
    tri = jnp.asarray(np.triu(np.ones((MOE_BLOCK, MOE_BLOCK), np.float32), k=1), BF16)
    for l in range(DEPTH):
        q, k, v, gu, gv = _in_proj(xf, w_in.astype(F32), l)
        a = _na_attention(q, k, v, _na_bias_table(na_rel_bias[l]), row(mix_norm_g[l, :D_NA]),
                          batch, seq)
        bs_full = jnp.repeat(sgu_b[l].astype(F32).T, SGU_GROUP_DIM, axis=1)
        s = _sgu(gu, gv, row(sgu_ln_g[l]), row(sgu_ln_b[l]), sgu_w[l].astype(BF16), bs_full,
                 row(mix_norm_g[l, D_NA:]))
        rw, rb = _router_weights(router_group_w[l], router_group_b[l],
                                           router_expert_w[l], router_expert_b[l])
        x1, rout_t, rout, tab = _out_proj(a, s, xf, w_out.astype(F32), l, row(ln1_g[l]),
                                          row(ln1_b[l]), rw, rb, tri)
        xf = _moe(tab[:, 0, :8], x1, rout_t, rout, expert_w_gate[l].astype(BF16),
                  expert_w_up[l].astype(BF16), expert_w_down[l].astype(BF16),
                  row(ln2_g[l]), row(ln2_b[l]))
    return xf.reshape(batch, seq, d).astype(x.dtype)
```

```python
import functools
import math

import numpy as np
import jax
import jax.numpy as jnp
from jax import lax
from jax.experimental import pallas as pl
from jax.experimental.pallas import tpu as pltpu

F32 = jnp.float32
BF16 = jnp.bfloat16

D_MODEL = 1024
DEPTH = 4
GRID_W = 64
NA_HEADS = 8
NA_HEAD_DIM = 64
NA_WIN_ROWS = 8
NA_WIN_COLS = 16
D_NA = NA_HEADS * NA_HEAD_DIM
SGU_GROUPS = 8
SGU_GROUP_DIM = 64
SGU_CHUNK = 128
D_SGU = SGU_GROUPS * SGU_GROUP_DIM
D_IN = 3 * D_NA + 2 * D_SGU
MOE_GROUPS = 4
MOE_EXPERTS_PER_GROUP = 4
MOE_EXPERTS = MOE_GROUPS * MOE_EXPERTS_PER_GROUP
D_EXPERT = 256
D_GROUP = MOE_EXPERTS_PER_GROUP * D_EXPERT
DEEPNORM_ALPHA = (2 * DEPTH) ** 0.25
LN_EPS = 1e-5

LANES = 128
HEAD_PAIRS = NA_HEADS // 2
NEG_BIG = -1e30
BAND_KEYS = NA_WIN_ROWS * GRID_W
BIAS_TILES = 7
ROUTER_ROWS = 32

TM_PROJ = 512
NA_ROWS_PER_STEP = 8
NA_ROWS_IN_FLIGHT = 4
NA_STRIP = 16
TM_SGU = 512
MOE_BLOCK = 512
MOE_CHUNK = 128
MOE_MAX_CHUNKS = MOE_BLOCK // MOE_CHUNK + MOE_GROUPS - 1
MOE_SORTED_ROWS = MOE_MAX_CHUNKS * MOE_CHUNK
OUT_BLOCKS_PER_STEP = 2
TAB_START = 0
TAB_COUNT = MOE_GROUPS
VMEM_LIMIT = 56 * 1024 * 1024


def _gelu_tanh(x):
    c = math.sqrt(2.0 / math.pi)
    return x * (0.5 * (1.0 + jnp.tanh(c * (x + 0.044715 * (x * x * x)))))


def _layer_norm_rows(h, g, b):
    mu = jnp.mean(h, axis=-1, keepdims=True)
    hc = h - mu
    var = jnp.mean(hc * hc, axis=-1, keepdims=True)
    return hc * lax.rsqrt(var + LN_EPS) * g + b


def _in_proj_kernel(x_ref, w_ref, wg_ref, wu_ref, wd_ref,
                    q_ref, k_ref, v_ref, u_ref, vs_ref, wg_out, wu_out, wd_out, wb_ref):
    @pl.when(pl.program_id(0) == 0)
    def _():
        wb_ref[...] = w_ref[0].astype(BF16)

    wg_out[0] = wg_ref[0].astype(BF16)
    wu_out[0] = wu_ref[0].astype(BF16)
    wd_out[...] = wd_ref[0].astype(BF16)

    xb = x_ref[...].astype(BF16)

    def mm(j):
        return jnp.dot(xb, wb_ref[:, j * D_NA:(j + 1) * D_NA], preferred_element_type=F32)

    q_ref[...] = (mm(0) * (NA_HEAD_DIM ** -0.5)).astype(BF16)
    k_ref[...] = mm(1).astype(BF16)
    v_ref[...] = mm(2).astype(BF16)
    u_ref[...] = _gelu_tanh(mm(3)).astype(BF16)
    vs_ref[...] = _gelu_tanh(mm(4)).astype(BF16)


def _in_proj(x, w_in, w_gate, w_up, w_down, layer):
    n = x.shape[0]
    steps = n // TM_PROJ
    gu_rows = MOE_EXPERTS * D_MODEL // steps
    parts = D_MODEL // gu_rows
    dn_rows = MOE_EXPERTS * D_EXPERT // steps
    assert gu_rows * steps == MOE_EXPERTS * D_MODEL and parts * gu_rows == D_MODEL
    assert dn_rows * steps == MOE_EXPERTS * D_EXPERT
    tok = lambda i: (i, 0)
    slab = lambda i: (layer, i, 0)
    grouped = lambda i: (i // (parts * MOE_EXPERTS_PER_GROUP), i % parts,
                         (i // parts) % MOE_EXPERTS_PER_GROUP)
    depth = w_gate.shape[0]
    outs = pl.pallas_call(
        _in_proj_kernel,
        grid=(steps,),
        in_specs=[pl.BlockSpec((TM_PROJ, D_MODEL), tok),
                  pl.BlockSpec((1, D_MODEL, D_IN), lambda i: (layer, 0, 0),
                               pipeline_mode=pl.Buffered(1)),
                  pl.BlockSpec((1, gu_rows, D_EXPERT), slab),
                  pl.BlockSpec((1, gu_rows, D_EXPERT), slab),
                  pl.BlockSpec((1, dn_rows, D_MODEL), slab)],
        out_specs=[pl.BlockSpec((TM_PROJ, D_NA), tok)] * 5
                  + [pl.BlockSpec((1, gu_rows, D_EXPERT), grouped)] * 2
                  + [pl.BlockSpec((dn_rows, D_MODEL), tok)],
        out_shape=[jax.ShapeDtypeStruct((n, D_NA), BF16)] * 5
                  + [jax.ShapeDtypeStruct((MOE_GROUPS, D_MODEL, D_GROUP), BF16)] * 2
                  + [jax.ShapeDtypeStruct((MOE_GROUPS * D_GROUP, D_MODEL), BF16)],
        scratch_shapes=[pltpu.VMEM((D_MODEL, D_IN), BF16)],
        compiler_params=pltpu.CompilerParams(dimension_semantics=("arbitrary",),
                                             vmem_limit_bytes=VMEM_LIMIT),
        name="in_proj",
    )(x, w_in,
      w_gate.reshape(depth, MOE_EXPERTS * D_MODEL, D_EXPERT),
      w_up.reshape(depth, MOE_EXPERTS * D_MODEL, D_EXPERT),
      w_down.reshape(depth, MOE_EXPERTS * D_EXPERT, D_MODEL))
    q, k, v, gu, gv, wg, wu, wd = outs
    return q, k, v, gu, gv, wg, wu, wd.reshape(MOE_GROUPS, D_GROUP, D_MODEL)


def _na_bias_table(rel_bias):
    cols = np.arange(GRID_W)
    col_start = np.clip(cols - NA_WIN_COLS // 2, 0, GRID_W - NA_WIN_COLS)
    kc = np.arange(GRID_W)
    valid = (kc[None, :] >= col_start[:, None]) & (kc[None, :] < col_start[:, None] + NA_WIN_COLS)
    dc = kc[None, :] - cols[:, None] + NA_WIN_COLS - 1
    onehot = (dc[None] == np.arange(2 * NA_WIN_COLS - 1)[:, None, None]) & valid[None]
    f = jnp.einsum('had,dck->hack', rel_bias.astype(F32), jnp.asarray(onehot, F32),
                   precision=lax.Precision.HIGHEST)
    f = jnp.where(valid[None, None], f, NEG_BIG)
    n_off = 2 * NA_WIN_ROWS - 1
    f = jnp.transpose(f, (0, 2, 1, 3)).reshape(NA_HEADS, GRID_W, n_off * GRID_W)
    g = f.reshape(HEAD_PAIRS, 2 * GRID_W, n_off * GRID_W)
    width = BIAS_TILES * LANES
    tabs = jnp.stack([g[:, :, :width], g[:, :, GRID_W:GRID_W + width]])
    tabs = tabs.reshape(2, HEAD_PAIRS, 2 * GRID_W, BIAS_TILES, LANES)
    return jnp.transpose(tabs, (0, 1, 3, 2, 4))


def _na_kernel(q_ref, k_ref, v_ref, t_ref, g_ref, o_ref, s_all, p_all, a_ref):
    rows = k_ref.shape[0] // GRID_W
    blk = pl.program_id(1)
    lane = lax.broadcasted_iota(jnp.int32, (GRID_W, LANES), 1)
    first_head = lane < NA_HEAD_DIM
    key_tiles = BAND_KEYS // LANES

    def one_row(rr, slot):
        s_ref = s_all.at[slot]
        p_ref = p_all.at[slot]
        r = blk * NA_ROWS_PER_STEP + rr
        rs = jnp.clip(r - NA_WIN_ROWS // 2, 0, rows - NA_WIN_ROWS)
        a0 = rs - r + (NA_WIN_ROWS - 1)
        par = a0 % 2
        j0 = a0 // 2
        q_off = pl.multiple_of(rr * GRID_W, GRID_W)
        k_off = pl.multiple_of(rs * GRID_W, GRID_W)
        for hp in range(HEAD_PAIRS):
            cs = slice(hp * LANES, (hp + 1) * LANES)
            qp = q_ref[pl.ds(q_off, GRID_W), cs]
            kp = k_ref[pl.ds(k_off, BAND_KEYS), cs]
            zero = jnp.zeros_like(qp)
            q2 = jnp.concatenate([jnp.where(first_head, qp, zero),
                                  jnp.where(first_head, zero, qp)], axis=0)
            s_ref[hp] = lax.dot_general(q2, kp, (((1,), (1,)), ((), ())),
                                        preferred_element_type=F32)
        for hp in range(HEAD_PAIRS):
            cs = slice(hp * LANES, (hp + 1) * LANES)
            inv_l = []
            for ch in range(2 * GRID_W // NA_STRIP):
                rsl = slice(ch * NA_STRIP, (ch + 1) * NA_STRIP)
                bias = jnp.concatenate([t_ref[par, hp, j0 + t, rsl, :] for t in range(key_tiles)],
                                       axis=1)
                sc = s_ref[hp, rsl, :] + bias
                m = jnp.max(sc, axis=1, keepdims=True)
                p = jnp.exp(sc - m)
                inv_l.append(1.0 / jnp.sum(p, axis=1, keepdims=True))
                p_ref[hp, rsl, :] = p.astype(BF16)
            vp = v_ref[pl.ds(k_off, BAND_KEYS), cs]
            o = jnp.dot(p_ref[hp], vp, preferred_element_type=F32)
            o = jnp.concatenate([o[ch * NA_STRIP:(ch + 1) * NA_STRIP] * inv_l[ch]
                                 for ch in range(len(inv_l))], axis=0)
            a_ref[pl.ds(q_off, GRID_W), cs] = jnp.where(first_head, o[:GRID_W], o[GRID_W:])

    def row_group(i, carry):
        for slot in range(NA_ROWS_IN_FLIGHT):
            one_row(i * NA_ROWS_IN_FLIGHT + slot, slot)
        return carry

    lax.fori_loop(0, NA_ROWS_PER_STEP // NA_ROWS_IN_FLIGHT, row_group, 0)

    a = a_ref[...]
    ms = jnp.mean(a * a, axis=-1, keepdims=True)
    o_ref[...] = (a * lax.rsqrt(ms + LN_EPS) * g_ref[...]).astype(o_ref.dtype)


def _na_attention(q, k, v, table, gain, batch, seq):
    n = q.shape[0]
    steps = seq // (GRID_W * NA_ROWS_PER_STEP)
    tq = NA_ROWS_PER_STEP * GRID_W
    return pl.pallas_call(
        _na_kernel,
        grid=(batch, steps),
        in_specs=[pl.BlockSpec((tq, D_NA), lambda b, i: (b * steps + i, 0)),
                  pl.BlockSpec((seq, D_NA), lambda b, i: (b, 0)),
                  pl.BlockSpec((seq, D_NA), lambda b, i: (b, 0)),
                  pl.BlockSpec(table.shape, lambda b, i: (0, 0, 0, 0, 0)),
                  pl.BlockSpec((1, D_NA), lambda b, i: (0, 0))],
        out_specs=pl.BlockSpec((tq, D_NA), lambda b, i: (b * steps + i, 0)),
        out_shape=jax.ShapeDtypeStruct((n, D_NA), BF16),
        scratch_shapes=[pltpu.VMEM((NA_ROWS_IN_FLIGHT, HEAD_PAIRS, 2 * GRID_W, BAND_KEYS), F32),
                        pltpu.VMEM((NA_ROWS_IN_FLIGHT, HEAD_PAIRS, 2 * GRID_W, BAND_KEYS), BF16),
                        pltpu.VMEM((tq, D_NA), F32)],
        compiler_params=pltpu.CompilerParams(dimension_semantics=("parallel", "parallel"),
                                             vmem_limit_bytes=VMEM_LIMIT),
        name="na_attention",
    )(q, k, v, table, gain)


def _sgu_kernel(u_ref, vs_ref, lng_ref, lnb_ref, ws_ref, bs_ref, g_ref, seg_ref, o_ref):
    lane = lax.broadcasted_iota(jnp.int32, (SGU_CHUNK, LANES), 1)
    first = lane < SGU_GROUP_DIM
    inv = 1.0 / SGU_GROUP_DIM

    n_tiles = D_SGU // LANES
    n_rows = TM_SGU * n_tiles

    def seg_mean(t):
        hi, lo = _split_bf16(t)
        s = jnp.dot(jnp.concatenate([hi, lo], axis=0), seg_ref[...], preferred_element_type=F32)
        return (s[:n_rows] + s[n_rows:]) * inv

    x = jnp.concatenate([vs_ref[:, j * LANES:(j + 1) * LANES] for j in range(n_tiles)],
                        axis=0).astype(F32)
    xc = x - seg_mean(x)
    xn = xc * lax.rsqrt(seg_mean(xc * xc) + LN_EPS)

    for c in range(TM_SGU // SGU_CHUNK):
        rs = slice(c * SGU_CHUNK, (c + 1) * SGU_CHUNK)
        tiles = []
        for j in range(n_tiles):
            cs = slice(j * LANES, (j + 1) * LANES)
            r0 = j * TM_SGU + c * SGU_CHUNK
            y = (xn[r0:r0 + SGU_CHUNK] * lng_ref[:, cs] + lnb_ref[:, cs]).astype(BF16)
            m = jnp.dot(jnp.concatenate([ws_ref[2 * j], ws_ref[2 * j + 1]], axis=0), y,
                        preferred_element_type=F32)
            mixed = jnp.where(first, m[:SGU_CHUNK], m[SGU_CHUNK:]) + bs_ref[:, cs]
            tiles.append(u_ref[rs, cs].astype(F32) * mixed)
        so = jnp.concatenate(tiles, axis=1)
        ms = jnp.mean(so * so, axis=-1, keepdims=True)
        o_ref[rs, :] = (so * lax.rsqrt(ms + LN_EPS) * g_ref[...]).astype(o_ref.dtype)


def _sgu(gu, gv, ln_g, ln_b, w_s_bf16, bs_full, gain):
    n = gu.shape[0]
    same_group = np.arange(LANES)[:, None] // SGU_GROUP_DIM == np.arange(LANES)[None, :] // SGU_GROUP_DIM
    seg_ones = jnp.asarray(same_group, BF16)
    tok = lambda i: (i, 0)
    row = lambda i: (0, 0)
    return pl.pallas_call(
        _sgu_kernel,
        grid=(n // TM_SGU,),
        in_specs=[pl.BlockSpec((TM_SGU, D_SGU), tok),
                  pl.BlockSpec((TM_SGU, D_SGU), tok),
                  pl.BlockSpec((1, D_SGU), row),
                  pl.BlockSpec((1, D_SGU), row),
                  pl.BlockSpec((SGU_GROUPS, SGU_CHUNK, SGU_CHUNK), lambda i: (0, 0, 0)),
                  pl.BlockSpec((SGU_CHUNK, D_SGU), row),
                  pl.BlockSpec((1, D_SGU), row),
                  pl.BlockSpec((LANES, LANES), row)],
        out_specs=pl.BlockSpec((TM_SGU, D_SGU), tok),
        out_shape=jax.ShapeDtypeStruct((n, D_SGU), BF16),
        compiler_params=pltpu.CompilerParams(dimension_semantics=("parallel",),
                                             vmem_limit_bytes=VMEM_LIMIT),
        name="spatial_gating",
    )(gu, gv, ln_g, ln_b, w_s_bf16, bs_full, gain, seg_ones)


def _split_bf16(a):
    hi = a.astype(BF16)
    lo = (a - hi.astype(F32)).astype(BF16)
    return hi, lo


def _route(lg, tri):
    gl = [lg[g:g + 1, :] for g in range(MOE_GROUPS)]
    gmax = functools.reduce(jnp.maximum, gl)
    gidx = jnp.full(gmax.shape, MOE_GROUPS - 1, jnp.int32)
    for g in range(MOE_GROUPS - 2, -1, -1):
        gidx = jnp.where(gl[g] == gmax, g, gidx)
    denom = functools.reduce(lambda a, b: a + b, [jnp.exp(t - gmax) for t in gl])
    gate = 1.0 / denom

    def expert_logit(i):
        rows_ = [lg[8 + MOE_EXPERTS_PER_GROUP * g + i:9 + MOE_EXPERTS_PER_GROUP * g + i, :]
                 for g in range(MOE_GROUPS)]
        sel = rows_[MOE_GROUPS - 1]
        for g in range(MOE_GROUPS - 2, -1, -1):
            sel = jnp.where(gidx == g, rows_[g], sel)
        return sel

    el = [expert_logit(i) for i in range(MOE_EXPERTS_PER_GROUP)]
    v1 = functools.reduce(jnp.maximum, el)
    i1 = jnp.full(v1.shape, MOE_EXPERTS_PER_GROUP - 1, jnp.int32)
    for i in range(MOE_EXPERTS_PER_GROUP - 2, -1, -1):
        i1 = jnp.where(el[i] == v1, i, i1)
    rest = [jnp.where(i1 == i, -jnp.inf, el[i]) for i in range(MOE_EXPERTS_PER_GROUP)]
    v2 = functools.reduce(jnp.maximum, rest)
    i2 = jnp.full(v2.shape, MOE_EXPERTS_PER_GROUP - 1, jnp.int32)
    for i in range(MOE_EXPERTS_PER_GROUP - 2, -1, -1):
        i2 = jnp.where((rest[i] == v2) & (i1 != i), i, i2)
    e2 = jnp.exp(v2 - v1)
    w1 = 1.0 / (1.0 + e2)
    w2 = e2 * w1
    within = [jnp.where(i1 == i, w1, 0.0) + jnp.where(i2 == i, w2, 0.0)
              for i in range(MOE_EXPERTS_PER_GROUP)]
    tm = MOE_BLOCK
    cw = [within[i] * gate for i in range(MOE_EXPERTS_PER_GROUP)]

    onehot = [jnp.where(gidx == g, 1.0, 0.0) for g in range(MOE_GROUPS)]
    oh_mat = jnp.concatenate(onehot + [jnp.zeros((16 - MOE_GROUPS, tm), F32)], axis=0)
    before = jnp.dot(oh_mat.astype(BF16), tri, preferred_element_type=F32)
    rank = functools.reduce(lambda a, b: a + b,
                            [onehot[g] * before[g:g + 1, :] for g in range(MOE_GROUPS)])
    starts, n_chunks = [], []
    start_tok = jnp.zeros_like(rank)
    end = jnp.zeros((1, 1), F32)
    for g in range(MOE_GROUPS):
        count = jnp.sum(onehot[g], axis=1, keepdims=True)
        start_tok = start_tok + onehot[g] * end
        starts.append(end)
        n_chunks.append(jnp.floor((count + (MOE_CHUNK - 1)) * (1.0 / MOE_CHUNK)))
        end = end + n_chunks[g]
    dest = rank + MOE_CHUNK * start_tok
    rout_t = jnp.concatenate([dest] + cw + [gidx.astype(F32), jnp.zeros((2, tm), F32)], axis=0)

    c = lax.broadcasted_iota(jnp.int32, (8, LANES), 1)
    tab = jnp.zeros((8, LANES), F32)
    for g in range(MOE_GROUPS):
        tab = jnp.where(c == TAB_START + g, starts[g], tab)
        tab = jnp.where(c == TAB_COUNT + g, n_chunks[g], tab)
    return rout_t, tab.astype(jnp.int32)


def _out_proj_kernel(a_ref, s_ref, x_ref, w_ref, g_ref, b_ref, rw_ref, rb_ref, tri_ref,
                     x1_ref, rt_ref, r_ref, tab_ref, wb_ref):
    @pl.when(pl.program_id(0) == 0)
    def _():
        wb_ref[...] = w_ref[0].astype(BF16)

    mixes = []
    for blk in range(OUT_BLOCKS_PER_STEP):
        rs = slice(blk * MOE_BLOCK, (blk + 1) * MOE_BLOCK)
        mix = jnp.dot(a_ref[rs, :], wb_ref[:D_NA, :], preferred_element_type=F32)
        mixes.append(mix + jnp.dot(s_ref[rs, :], wb_ref[D_NA:, :], preferred_element_type=F32))
    logits = []
    for blk in range(OUT_BLOCKS_PER_STEP):
        rs = slice(blk * MOE_BLOCK, (blk + 1) * MOE_BLOCK)
        x1 = _layer_norm_rows(DEEPNORM_ALPHA * x_ref[rs, :] + mixes[blk], g_ref[...], b_ref[...])
        x1_ref[rs, :] = x1
        xh, xl = _split_bf16(x1)
        logits.append(jnp.dot(xh, rw_ref[...], preferred_element_type=F32)
                      + jnp.dot(xl, rw_ref[...], preferred_element_type=F32))
    for blk in range(OUT_BLOCKS_PER_STEP):
        rs = slice(blk * MOE_BLOCK, (blk + 1) * MOE_BLOCK)
        lg_t = logits[blk].T
        lg = lg_t[:ROUTER_ROWS] + lg_t[ROUTER_ROWS:2 * ROUTER_ROWS] + rb_ref[:, 0:1]
        rout_t, tab = _route(lg, tri_ref[...])
        rt_ref[:, rs] = rout_t
        r_ref[rs, :] = jnp.concatenate([rout_t, jnp.zeros((LANES - 8, MOE_BLOCK), F32)], axis=0).T
        tab_ref[blk] = tab


def _out_proj(a, s, x, w_out, layer, g1, b1, rw, rb, tri):
    n = x.shape[0]
    tok = lambda i: (i, 0)
    row = lambda i: (0, 0)
    tm = OUT_BLOCKS_PER_STEP * MOE_BLOCK
    return pl.pallas_call(
        _out_proj_kernel,
        grid=(n // tm,),
        in_specs=[pl.BlockSpec((tm, D_NA), tok),
                  pl.BlockSpec((tm, D_SGU), tok),
                  pl.BlockSpec((tm, D_MODEL), tok),
                  pl.BlockSpec((1, D_MODEL, D_MODEL), lambda i: (layer, 0, 0),
                               pipeline_mode=pl.Buffered(1)),
                  pl.BlockSpec((1, D_MODEL), row),
                  pl.BlockSpec((1, D_MODEL), row),
                  pl.BlockSpec((D_MODEL, LANES), row),
                  pl.BlockSpec((ROUTER_ROWS, LANES), row),
                  pl.BlockSpec((MOE_BLOCK, MOE_BLOCK), row)],
        out_specs=[pl.BlockSpec((tm, D_MODEL), tok),
                   pl.BlockSpec((8, tm), lambda i: (0, i)),
                   pl.BlockSpec((tm, LANES), tok),
                   pl.BlockSpec((OUT_BLOCKS_PER_STEP, 8, LANES), lambda i: (i, 0, 0))],
        out_shape=[jax.ShapeDtypeStruct((n, D_MODEL), F32),
                   jax.ShapeDtypeStruct((8, n), F32),
                   jax.ShapeDtypeStruct((n, LANES), F32),
                   jax.ShapeDtypeStruct((n // MOE_BLOCK, 8, LANES), jnp.int32)],
        scratch_shapes=[pltpu.VMEM((D_MODEL, D_MODEL), BF16)],
        compiler_params=pltpu.CompilerParams(dimension_semantics=("arbitrary",),
                                             vmem_limit_bytes=VMEM_LIMIT),
        name="out_proj_ln_router",
    )(a, s, x, w_out, g1, b1, rw, rb, tri)


def _moe_kernel(tab_ref, x_ref, rt_ref, r_ref, wg_ref, wu_ref, wd_ref, g_ref, b_ref, o_ref,
                xs_ref, ys_ref, cw_ref):
    blk = pl.program_id(0)
    x1 = x_ref[...]
    dest_row = rt_ref[0:1, :].astype(jnp.int32)
    rows_i = lax.broadcasted_iota(jnp.int32, (MOE_SORTED_ROWS, MOE_BLOCK), 0)
    p_in = jnp.where(rows_i == dest_row, 1.0, 0.0).astype(BF16)
    xs_ref[...] = jnp.dot(p_in, x1.astype(BF16), preferred_element_type=F32).astype(BF16)
    r_hi, r_lo = _split_bf16(r_ref[...])
    cw2 = jnp.dot(p_in, jnp.concatenate([r_hi, r_lo], axis=1), preferred_element_type=F32)
    cw_ref[...] = cw2[:, :LANES] + cw2[:, LANES:]
    ys_ref[...] = jnp.zeros_like(ys_ref)
    dest_col = r_ref[:, 0:1].astype(jnp.int32)
    cols_i = lax.broadcasted_iota(jnp.int32, (MOE_BLOCK, MOE_SORTED_ROWS), 1)
    p_out = jnp.where(cols_i == dest_col, 1.0, 0.0).astype(BF16)

    def run_experts(g, first_chunk, n_rows):
        off = pl.multiple_of(first_chunk * MOE_CHUNK, MOE_CHUNK)
        xs = xs_ref[pl.ds(off, n_rows), :]
        cw = cw_ref[pl.ds(off, n_rows), :]
        hg = jnp.dot(xs, wg_ref[g], preferred_element_type=F32)
        hu = jnp.dot(xs, wu_ref[g], preferred_element_type=F32)
        scale = jnp.concatenate([jnp.broadcast_to(cw[:, 1 + i:2 + i], (n_rows, D_EXPERT))
                                 for i in range(MOE_EXPERTS_PER_GROUP)], axis=1)
        act = hg * (1.0 / (1.0 + jnp.exp(-hg))) * hu * scale
        y = jnp.dot(act.astype(BF16), wd_ref[g], preferred_element_type=F32)
        ys_ref[pl.ds(off, n_rows), :] = y.astype(BF16)

    def group(g, carry):
        start = tab_ref[blk, TAB_START + g]
        n_chunks = tab_ref[blk, TAB_COUNT + g]

        def pair(j, c):
            run_experts(g, start + 2 * j, 2 * MOE_CHUNK)
            return c

        lax.fori_loop(0, n_chunks // 2, pair, 0)

        @pl.when(n_chunks % 2 == 1)
        def _():
            run_experts(g, start + n_chunks - 1, MOE_CHUNK)

        return carry

    lax.fori_loop(0, MOE_GROUPS, group, 0)

    half = MOE_BLOCK // 2
    ys = ys_ref[...]
    y_lo = jnp.dot(p_out[:half], ys, preferred_element_type=F32)
    y_hi = jnp.dot(p_out[half:], ys, preferred_element_type=F32)
    o_ref[:half, :] = _layer_norm_rows(DEEPNORM_ALPHA * x1[:half] + y_lo, g_ref[...], b_ref[...])
    o_ref[half:, :] = _layer_norm_rows(DEEPNORM_ALPHA * x1[half:] + y_hi, g_ref[...], b_ref[...])


def _moe(tab, x1, rout_t, rout, wg, wu, wd, g2, b2):
    n = x1.shape[0]
    tok = lambda i, t: (i, 0)
    row = lambda i, t: (0, 0)
    whole = lambda i, t: (0, 0, 0)
    resident = pl.Buffered(1)
    return pl.pallas_call(
        _moe_kernel,
        grid_spec=pltpu.PrefetchScalarGridSpec(
            num_scalar_prefetch=1,
            grid=(n // MOE_BLOCK,),
            in_specs=[pl.BlockSpec((MOE_BLOCK, D_MODEL), tok),
                      pl.BlockSpec((8, MOE_BLOCK), lambda i, t: (0, i)),
                      pl.BlockSpec((MOE_BLOCK, LANES), tok),
                      pl.BlockSpec((MOE_GROUPS, D_MODEL, D_GROUP), whole, pipeline_mode=resident),
                      pl.BlockSpec((MOE_GROUPS, D_MODEL, D_GROUP), whole, pipeline_mode=resident),
                      pl.BlockSpec((MOE_GROUPS, D_GROUP, D_MODEL), whole, pipeline_mode=resident),
                      pl.BlockSpec((1, D_MODEL), row),
                      pl.BlockSpec((1, D_MODEL), row)],
            out_specs=pl.BlockSpec((MOE_BLOCK, D_MODEL), tok),
            scratch_shapes=[pltpu.VMEM((MOE_SORTED_ROWS, D_MODEL), BF16),
                            pltpu.VMEM((MOE_SORTED_ROWS, D_MODEL), BF16),
                            pltpu.VMEM((MOE_SORTED_ROWS, LANES), F32)]),
        out_shape=jax.ShapeDtypeStruct((n, D_MODEL), F32),
        compiler_params=pltpu.CompilerParams(dimension_semantics=("parallel",),
                                             vmem_limit_bytes=VMEM_LIMIT),
        name="moe_experts_ln",
    )(tab, x1, rout_t, rout, wg, wu, wd, g2, b2)


def _router_weights(w_rg, b_rg, w_re, b_re):
    wt = jnp.zeros((ROUTER_ROWS, D_MODEL), F32)
    wt = wt.at[:MOE_GROUPS].set(w_rg.T.astype(F32))
    wt = wt.at[8:8 + MOE_EXPERTS].set(w_re.T.astype(F32))
    bt = jnp.zeros((ROUTER_ROWS,), F32)
    bt = bt.at[:MOE_GROUPS].set(b_rg.astype(F32))
    bt = bt.at[8:8 + MOE_EXPERTS].set(b_re.astype(F32))
    hi, lo = _split_bf16(wt)
    rw = jnp.concatenate([hi.T, lo.T, jnp.zeros((D_MODEL, LANES - 2 * ROUTER_ROWS), BF16)], axis=1)
    return rw, jnp.broadcast_to(bt[:, None], (ROUTER_ROWS, LANES))


def kernel(x, w_in, w_out, na_rel_bias, sgu_ln_g, sgu_ln_b, sgu_w, sgu_b, mix_norm_g, ln1_g, ln1_b, router_group_w, router_group_b, router_expert_w, router_expert_b, expert_w_gate, expert_w_up, expert_w_down, ln2_g, ln2_b):
    batch, seq, d = x.shape
    n = batch * seq
    xf = x.reshape(n, d).astype(F32)
    row = lambda a: a.astype(F32).reshape(1, -1)
    tri = jnp.asarray(np.triu(np.ones((MOE_BLOCK, MOE_BLOCK), np.float32), k=1), BF16)
    for l in range(DEPTH):
        q, k, v, gu, gv, wg, wu, wd = _in_proj(xf, w_in.astype(F32), expert_w_gate.astype(F32),
                                               expert_w_up.astype(F32), expert_w_down.astype(F32), l)
        a = _na_attention(q, k, v, _na_bias_table(na_rel_bias[l]), row(mix_norm_g[l, :D_NA]),
                          batch, seq)
        bs_full = jnp.repeat(sgu_b[l].astype(F32).T, SGU_GROUP_DIM, axis=1)
        s = _sgu(gu, gv, row(sgu_ln_g[l]), row(sgu_ln_b[l]), sgu_w[l].astype(BF16), bs_full,
                 row(mix_norm_g[l, D_NA:]))
        rw, rb = _router_weights(router_group_w[l], router_group_b[l],
                                           router_expert_w[l], router_expert_b[l])
        x1, rout_t, rout, tab = _out_proj(a, s, xf, w_out.astype(F32), l, row(ln1_g[l]),
                                          row(ln1_b[l]), rw, rb, tri)
        xf = _moe(tab[:, 0, :8], x1, rout_t, rout, wg, wu, wd, row(ln2_g[l]), row(ln2_b[l]))
    return xf.reshape(batch, seq, d).astype(x.dtype)
```

```python
import functools
import math

import numpy as np
import jax
import jax.numpy as jnp
from jax import lax
from jax.experimental import pallas as pl
from jax.experimental.pallas import tpu as pltpu

F32 = jnp.float32
BF16 = jnp.bfloat16

D_MODEL = 1024
DEPTH = 4
GRID_W = 64
NA_HEADS = 8
NA_HEAD_DIM = 64
NA_WIN_ROWS = 8
NA_WIN_COLS = 16
D_NA = NA_HEADS * NA_HEAD_DIM
SGU_GROUPS = 8
SGU_GROUP_DIM = 64
SGU_CHUNK = 128
D_SGU = SGU_GROUPS * SGU_GROUP_DIM
D_IN = 3 * D_NA + 2 * D_SGU
MOE_GROUPS = 4
MOE_EXPERTS_PER_GROUP = 4
MOE_EXPERTS = MOE_GROUPS * MOE_EXPERTS_PER_GROUP
D_EXPERT = 256
D_GROUP = MOE_EXPERTS_PER_GROUP * D_EXPERT
DEEPNORM_ALPHA = (2 * DEPTH) ** 0.25
LN_EPS = 1e-5

LANES = 128
HEAD_PAIRS = NA_HEADS // 2
NEG_BIG = -1e30
BAND_KEYS = NA_WIN_ROWS * GRID_W
BIAS_TILES = 7
ROUTER_ROWS = 32

TM_PROJ = 512
NA_ROWS_PER_STEP = 8
NA_ROWS_IN_FLIGHT = 4
NA_STRIP = 16
TM_SGU = 512
MOE_BLOCK = 512
MOE_ALIGN = 16
MOE_PIECES = (64, 128, 144, 160, 192, 256, 384, MOE_BLOCK)
MOE_SORTED_ROWS = 768
OUT_BLOCKS_PER_STEP = 2
TAB_START = 0
TAB_PIECE = MOE_GROUPS
VMEM_LIMIT = 56 * 1024 * 1024


def _gelu_tanh(x):
    c = math.sqrt(2.0 / math.pi)
    return x * (0.5 * (1.0 + jnp.tanh(c * (x + 0.044715 * (x * x * x)))))


def _layer_norm_rows(h, g, b):
    mu = jnp.mean(h, axis=-1, keepdims=True)
    hc = h - mu
    var = jnp.mean(hc * hc, axis=-1, keepdims=True)
    return hc * lax.rsqrt(var + LN_EPS) * g + b


def _in_proj_kernel(x_ref, w_ref, wg_ref, wu_ref, wd_ref,
                    q_ref, k_ref, v_ref, u_ref, vs_ref, wg_out, wu_out, wd_out, wb_ref):
    @pl.when(pl.program_id(0) == 0)
    def _():
        wb_ref[...] = w_ref[0].astype(BF16)

    wg_out[0] = wg_ref[0].astype(BF16)
    wu_out[0] = wu_ref[0].astype(BF16)
    wd_out[...] = wd_ref[0].astype(BF16)

    xb = x_ref[...].astype(BF16)

    def mm(j):
        return jnp.dot(xb, wb_ref[:, j * D_NA:(j + 1) * D_NA], preferred_element_type=F32)

    hu, hv, hq, hk, hval = mm(3), mm(4), mm(0), mm(1), mm(2)
    u_ref[...] = _gelu_tanh(hu).astype(BF16)
    vs_ref[...] = _gelu_tanh(hv).astype(BF16)
    q_ref[...] = (hq * (NA_HEAD_DIM ** -0.5)).astype(BF16)
    k_ref[...] = hk.astype(BF16)
    v_ref[...] = hval.astype(BF16)


def _in_proj(x, w_in, w_gate, w_up, w_down, layer):
    n = x.shape[0]
    steps = n // TM_PROJ
    gu_rows = MOE_EXPERTS * D_MODEL // steps
    parts = D_MODEL // gu_rows
    dn_rows = MOE_EXPERTS * D_EXPERT // steps
    assert gu_rows * steps == MOE_EXPERTS * D_MODEL and parts * gu_rows == D_MODEL
    assert dn_rows * steps == MOE_EXPERTS * D_EXPERT
    tok = lambda i: (i, 0)
    slab = lambda i: (layer, i, 0)
    grouped = lambda i: (i // (parts * MOE_EXPERTS_PER_GROUP), i % parts,
                         (i // parts) % MOE_EXPERTS_PER_GROUP)
    depth = w_gate.shape[0]
    outs = pl.pallas_call(
        _in_proj_kernel,
        grid=(steps,),
        in_specs=[pl.BlockSpec((TM_PROJ, D_MODEL), tok),
                  pl.BlockSpec((1, D_MODEL, D_IN), lambda i: (layer, 0, 0),
                               pipeline_mode=pl.Buffered(1)),
                  pl.BlockSpec((1, gu_rows, D_EXPERT), slab),
                  pl.BlockSpec((1, gu_rows, D_EXPERT), slab),
                  pl.BlockSpec((1, dn_rows, D_MODEL), slab)],
        out_specs=[pl.BlockSpec((TM_PROJ, D_NA), tok)] * 5
                  + [pl.BlockSpec((1, gu_rows, D_EXPERT), grouped)] * 2
                  + [pl.BlockSpec((dn_rows, D_MODEL), tok)],
        out_shape=[jax.ShapeDtypeStruct((n, D_NA), BF16)] * 5
                  + [jax.ShapeDtypeStruct((MOE_GROUPS, D_MODEL, D_GROUP), BF16)] * 2
                  + [jax.ShapeDtypeStruct((MOE_GROUPS * D_GROUP, D_MODEL), BF16)],
        scratch_shapes=[pltpu.VMEM((D_MODEL, D_IN), BF16)],
        compiler_params=pltpu.CompilerParams(dimension_semantics=("arbitrary",),
                                             vmem_limit_bytes=VMEM_LIMIT),
        name="in_proj",
    )(x, w_in,
      w_gate.reshape(depth, MOE_EXPERTS * D_MODEL, D_EXPERT),
      w_up.reshape(depth, MOE_EXPERTS * D_MODEL, D_EXPERT),
      w_down.reshape(depth, MOE_EXPERTS * D_EXPERT, D_MODEL))
    q, k, v, gu, gv, wg, wu, wd = outs
    return q, k, v, gu, gv, wg, wu, wd.reshape(MOE_GROUPS, D_GROUP, D_MODEL)


def _na_bias_table(rel_bias):
    cols = np.arange(GRID_W)
    col_start = np.clip(cols - NA_WIN_COLS // 2, 0, GRID_W - NA_WIN_COLS)
    kc = np.arange(GRID_W)
    valid = (kc[None, :] >= col_start[:, None]) & (kc[None, :] < col_start[:, None] + NA_WIN_COLS)
    dc = kc[None, :] - cols[:, None] + NA_WIN_COLS - 1
    onehot = (dc[None] == np.arange(2 * NA_WIN_COLS - 1)[:, None, None]) & valid[None]
    f = jnp.einsum('had,dck->hack', rel_bias.astype(F32), jnp.asarray(onehot, F32),
                   precision=lax.Precision.HIGHEST)
    f = jnp.where(valid[None, None], f, NEG_BIG)
    n_off = 2 * NA_WIN_ROWS - 1
    f = jnp.transpose(f, (0, 2, 1, 3)).reshape(NA_HEADS, GRID_W, n_off * GRID_W)
    g = f.reshape(HEAD_PAIRS, 2 * GRID_W, n_off * GRID_W)
    width = BIAS_TILES * LANES
    tabs = jnp.stack([g[:, :, :width], g[:, :, GRID_W:GRID_W + width]])
    tabs = tabs.reshape(2, HEAD_PAIRS, 2 * GRID_W, BIAS_TILES, LANES)
    return jnp.transpose(tabs, (0, 1, 3, 2, 4))


def _na_kernel(q_ref, k_ref, v_ref, t_ref, g_ref, o_ref, s_all, p_all, a_ref):
    rows = k_ref.shape[0] // GRID_W
    blk = pl.program_id(1)
    lane = lax.broadcasted_iota(jnp.int32, (GRID_W, LANES), 1)
    first_head = lane < NA_HEAD_DIM
    key_tiles = BAND_KEYS // LANES

    def one_row(rr, slot):
        s_ref = s_all.at[slot]
        p_ref = p_all.at[slot]
        r = blk * NA_ROWS_PER_STEP + rr
        rs = jnp.clip(r - NA_WIN_ROWS // 2, 0, rows - NA_WIN_ROWS)
        a0 = rs - r + (NA_WIN_ROWS - 1)
        par = a0 % 2
        j0 = a0 // 2
        q_off = pl.multiple_of(rr * GRID_W, GRID_W)
        k_off = pl.multiple_of(rs * GRID_W, GRID_W)
        for hp in range(HEAD_PAIRS):
            cs = slice(hp * LANES, (hp + 1) * LANES)
            qp = q_ref[pl.ds(q_off, GRID_W), cs]
            kp = k_ref[pl.ds(k_off, BAND_KEYS), cs]
            zero = jnp.zeros_like(qp)
            q2 = jnp.concatenate([jnp.where(first_head, qp, zero),
                                  jnp.where(first_head, zero, qp)], axis=0)
            s_ref[hp] = lax.dot_general(q2, kp, (((1,), (1,)), ((), ())),
                                        preferred_element_type=F32)
        for hp in range(HEAD_PAIRS):
            cs = slice(hp * LANES, (hp + 1) * LANES)
            inv_l = []
            for ch in range(2 * GRID_W // NA_STRIP):
                rsl = slice(ch * NA_STRIP, (ch + 1) * NA_STRIP)
                bias = jnp.concatenate([t_ref[par, hp, j0 + t, rsl, :] for t in range(key_tiles)],
                                       axis=1)
                sc = s_ref[hp, rsl, :] + bias
                m = jnp.max(sc, axis=1, keepdims=True)
                p = jnp.exp(sc - m)
                inv_l.append(1.0 / jnp.sum(p, axis=1, keepdims=True))
                p_ref[hp, rsl, :] = p.astype(BF16)
            vp = v_ref[pl.ds(k_off, BAND_KEYS), cs]
            o = jnp.dot(p_ref[hp], vp, preferred_element_type=F32)
            o = jnp.concatenate([o[ch * NA_STRIP:(ch + 1) * NA_STRIP] * inv_l[ch]
                                 for ch in range(len(inv_l))], axis=0)
            a_ref[pl.ds(q_off, GRID_W), cs] = jnp.where(first_head, o[:GRID_W], o[GRID_W:])

    def row_group(i, carry):
        for slot in range(NA_ROWS_IN_FLIGHT):
            one_row(i * NA_ROWS_IN_FLIGHT + slot, slot)
        return carry

    lax.fori_loop(0, NA_ROWS_PER_STEP // NA_ROWS_IN_FLIGHT, row_group, 0)

    a = a_ref[...]
    ms = jnp.mean(a * a, axis=-1, keepdims=True)
    o_ref[...] = (a * lax.rsqrt(ms + LN_EPS) * g_ref[...]).astype(o_ref.dtype)


def _na_attention(q, k, v, table, gain, batch, seq):
    n = q.shape[0]
    steps = seq // (GRID_W * NA_ROWS_PER_STEP)
    tq = NA_ROWS_PER_STEP * GRID_W
    return pl.pallas_call(
        _na_kernel,
        grid=(batch, steps),
        in_specs=[pl.BlockSpec((tq, D_NA), lambda b, i: (b * steps + i, 0)),
                  pl.BlockSpec((seq, D_NA), lambda b, i: (b, 0)),
                  pl.BlockSpec((seq, D_NA), lambda b, i: (b, 0)),
                  pl.BlockSpec(table.shape, lambda b, i: (0, 0, 0, 0, 0)),
                  pl.BlockSpec((1, D_NA), lambda b, i: (0, 0))],
        out_specs=pl.BlockSpec((tq, D_NA), lambda b, i: (b * steps + i, 0)),
        out_shape=jax.ShapeDtypeStruct((n, D_NA), BF16),
        scratch_shapes=[pltpu.VMEM((NA_ROWS_IN_FLIGHT, HEAD_PAIRS, 2 * GRID_W, BAND_KEYS), F32),
                        pltpu.VMEM((NA_ROWS_IN_FLIGHT, HEAD_PAIRS, 2 * GRID_W, BAND_KEYS), BF16),
                        pltpu.VMEM((tq, D_NA), F32)],
        compiler_params=pltpu.CompilerParams(dimension_semantics=("parallel", "parallel"),
                                             vmem_limit_bytes=VMEM_LIMIT),
        name="na_attention",
    )(q, k, v, table, gain)


def _sgu_kernel(u_ref, vs_ref, lng_ref, lnb_ref, ws_ref, bs_ref, g_ref, seg_ref, o_ref):
    lane = lax.broadcasted_iota(jnp.int32, (SGU_CHUNK, LANES), 1)
    first = lane < SGU_GROUP_DIM
    inv = 1.0 / SGU_GROUP_DIM

    n_tiles = D_SGU // LANES
    n_rows = TM_SGU * n_tiles

    def seg_mean(t):
        hi, lo = _split_bf16(t)
        s = jnp.dot(jnp.concatenate([hi, lo], axis=0), seg_ref[...], preferred_element_type=F32)
        return (s[:n_rows] + s[n_rows:]) * inv

    x = jnp.concatenate([vs_ref[:, j * LANES:(j + 1) * LANES] for j in range(n_tiles)],
                        axis=0).astype(F32)
    xc = x - seg_mean(x)
    xn = xc * lax.rsqrt(seg_mean(xc * xc) + LN_EPS)

    for c in range(TM_SGU // SGU_CHUNK):
        rs = slice(c * SGU_CHUNK, (c + 1) * SGU_CHUNK)
        tiles = []
        for j in range(n_tiles):
            cs = slice(j * LANES, (j + 1) * LANES)
            r0 = j * TM_SGU + c * SGU_CHUNK
            y = (xn[r0:r0 + SGU_CHUNK] * lng_ref[:, cs] + lnb_ref[:, cs]).astype(BF16)
            m = jnp.dot(jnp.concatenate([ws_ref[2 * j], ws_ref[2 * j + 1]], axis=0), y,
                        preferred_element_type=F32)
            mixed = jnp.where(first, m[:SGU_CHUNK], m[SGU_CHUNK:]) + bs_ref[:, cs]
            tiles.append(u_ref[rs, cs].astype(F32) * mixed)
        so = jnp.concatenate(tiles, axis=1)
        ms = jnp.mean(so * so, axis=-1, keepdims=True)
        o_ref[rs, :] = (so * lax.rsqrt(ms + LN_EPS) * g_ref[...]).astype(o_ref.dtype)


def _sgu(gu, gv, ln_g, ln_b, w_s_bf16, bs_full, gain):
    n = gu.shape[0]
    same_group = np.arange(LANES)[:, None] // SGU_GROUP_DIM == np.arange(LANES)[None, :] // SGU_GROUP_DIM
    seg_ones = jnp.asarray(same_group, BF16)
    tok = lambda i: (i, 0)
    row = lambda i: (0, 0)
    return pl.pallas_call(
        _sgu_kernel,
        grid=(n // TM_SGU,),
        in_specs=[pl.BlockSpec((TM_SGU, D_SGU), tok),
                  pl.BlockSpec((TM_SGU, D_SGU), tok),
                  pl.BlockSpec((1, D_SGU), row),
                  pl.BlockSpec((1, D_SGU), row),
                  pl.BlockSpec((SGU_GROUPS, SGU_CHUNK, SGU_CHUNK), lambda i: (0, 0, 0)),
                  pl.BlockSpec((SGU_CHUNK, D_SGU), row),
                  pl.BlockSpec((1, D_SGU), row),
                  pl.BlockSpec((LANES, LANES), row)],
        out_specs=pl.BlockSpec((TM_SGU, D_SGU), tok),
        out_shape=jax.ShapeDtypeStruct((n, D_SGU), BF16),
        compiler_params=pltpu.CompilerParams(dimension_semantics=("parallel",),
                                             vmem_limit_bytes=VMEM_LIMIT),
        name="spatial_gating",
    )(gu, gv, ln_g, ln_b, w_s_bf16, bs_full, gain, seg_ones)


def _split_bf16(a):
    hi = a.astype(BF16)
    lo = (a - hi.astype(F32)).astype(BF16)
    return hi, lo


def _route(lg, tri):
    gl = [lg[g:g + 1, :] for g in range(MOE_GROUPS)]
    gmax = functools.reduce(jnp.maximum, gl)
    gidx = jnp.full(gmax.shape, MOE_GROUPS - 1, jnp.int32)
    for g in range(MOE_GROUPS - 2, -1, -1):
        gidx = jnp.where(gl[g] == gmax, g, gidx)
    denom = functools.reduce(lambda a, b: a + b, [jnp.exp(t - gmax) for t in gl])
    gate = 1.0 / denom

    def expert_logit(i):
        rows_ = [lg[8 + MOE_EXPERTS_PER_GROUP * g + i:9 + MOE_EXPERTS_PER_GROUP * g + i, :]
                 for g in range(MOE_GROUPS)]
        sel = rows_[MOE_GROUPS - 1]
        for g in range(MOE_GROUPS - 2, -1, -1):
            sel = jnp.where(gidx == g, rows_[g], sel)
        return sel

    el = [expert_logit(i) for i in range(MOE_EXPERTS_PER_GROUP)]
    v1 = functools.reduce(jnp.maximum, el)
    i1 = jnp.full(v1.shape, MOE_EXPERTS_PER_GROUP - 1, jnp.int32)
    for i in range(MOE_EXPERTS_PER_GROUP - 2, -1, -1):
        i1 = jnp.where(el[i] == v1, i, i1)
    rest = [jnp.where(i1 == i, -jnp.inf, el[i]) for i in range(MOE_EXPERTS_PER_GROUP)]
    v2 = functools.reduce(jnp.maximum, rest)
    i2 = jnp.full(v2.shape, MOE_EXPERTS_PER_GROUP - 1, jnp.int32)
    for i in range(MOE_EXPERTS_PER_GROUP - 2, -1, -1):
        i2 = jnp.where((rest[i] == v2) & (i1 != i), i, i2)
    e2 = jnp.exp(v2 - v1)
    w1 = 1.0 / (1.0 + e2)
    w2 = e2 * w1
    within = [jnp.where(i1 == i, w1, 0.0) + jnp.where(i2 == i, w2, 0.0)
              for i in range(MOE_EXPERTS_PER_GROUP)]
    tm = MOE_BLOCK
    cw = [within[i] * gate for i in range(MOE_EXPERTS_PER_GROUP)]

    onehot = [jnp.where(gidx == g, 1.0, 0.0) for g in range(MOE_GROUPS)]
    oh_mat = jnp.concatenate(onehot + [jnp.zeros((16 - MOE_GROUPS, tm), F32)], axis=0)
    before = jnp.dot(oh_mat.astype(BF16), tri, preferred_element_type=F32)
    rank = functools.reduce(lambda a, b: a + b,
                            [onehot[g] * before[g:g + 1, :] for g in range(MOE_GROUPS)])
    starts, pieces = [], []
    start_tok = jnp.zeros_like(rank)
    end = jnp.zeros((1, 1), F32)
    for g in range(MOE_GROUPS):
        count = jnp.sum(onehot[g], axis=1, keepdims=True)
        start_tok = start_tok + onehot[g] * end
        starts.append(end)
        end = end + MOE_ALIGN * jnp.floor((count + (MOE_ALIGN - 1)) * (1.0 / MOE_ALIGN))
        piece = functools.reduce(lambda a, b: a + b,
                                 [jnp.where(count > p, 1.0, 0.0) for p in MOE_PIECES[:-1]])
        pieces.append(jnp.where(count > 0.0, piece, -1.0))
    dest = rank + start_tok
    rout_t = jnp.concatenate([dest] + cw + [gidx.astype(F32), jnp.zeros((2, tm), F32)], axis=0)

    c = lax.broadcasted_iota(jnp.int32, (8, LANES), 1)
    tab = jnp.zeros((8, LANES), F32)
    for g in range(MOE_GROUPS):
        tab = jnp.where(c == TAB_START + g, starts[g], tab)
        tab = jnp.where(c == TAB_PIECE + g, pieces[g], tab)
    return rout_t, tab.astype(jnp.int32)


def _out_proj_kernel(a_ref, s_ref, x_ref, w_ref, g_ref, b_ref, rw_ref, rb_ref, tri_ref,
                     x1_ref, rt_ref, r_ref, tab_ref, wb_ref):
    @pl.when(pl.program_id(0) == 0)
    def _():
        wb_ref[...] = w_ref[0].astype(BF16)

    mixes = []
    for blk in range(OUT_BLOCKS_PER_STEP):
        rs = slice(blk * MOE_BLOCK, (blk + 1) * MOE_BLOCK)
        mix = jnp.dot(a_ref[rs, :], wb_ref[:D_NA, :], preferred_element_type=F32)
        mixes.append(mix + jnp.dot(s_ref[rs, :], wb_ref[D_NA:, :], preferred_element_type=F32))
    logits = []
    for blk in range(OUT_BLOCKS_PER_STEP):
        rs = slice(blk * MOE_BLOCK, (blk + 1) * MOE_BLOCK)
        x1 = _layer_norm_rows(DEEPNORM_ALPHA * x_ref[rs, :] + mixes[blk], g_ref[...], b_ref[...])
        x1_ref[rs, :] = x1
        xh, xl = _split_bf16(x1)
        logits.append(jnp.dot(xh, rw_ref[...], preferred_element_type=F32)
                      + jnp.dot(xl, rw_ref[...], preferred_element_type=F32))
    for blk in range(OUT_BLOCKS_PER_STEP):
        rs = slice(blk * MOE_BLOCK, (blk + 1) * MOE_BLOCK)
        lg_t = logits[blk].T
        lg = lg_t[:ROUTER_ROWS] + lg_t[ROUTER_ROWS:2 * ROUTER_ROWS] + rb_ref[:, 0:1]
        rout_t, tab = _route(lg, tri_ref[...])
        rt_ref[:, rs] = rout_t
        r_ref[rs, :] = jnp.concatenate([rout_t, jnp.zeros((LANES - 8, MOE_BLOCK), F32)], axis=0).T
        tab_ref[blk] = tab


def _out_proj(a, s, x, w_out, layer, g1, b1, rw, rb, tri):
    n = x.shape[0]
    tok = lambda i: (i, 0)
    row = lambda i: (0, 0)
    tm = OUT_BLOCKS_PER_STEP * MOE_BLOCK
    return pl.pallas_call(
        _out_proj_kernel,
        grid=(n // tm,),
        in_specs=[pl.BlockSpec((tm, D_NA), tok),
                  pl.BlockSpec((tm, D_SGU), tok),
                  pl.BlockSpec((tm, D_MODEL), tok),
                  pl.BlockSpec((1, D_MODEL, D_MODEL), lambda i: (layer, 0, 0),
                               pipeline_mode=pl.Buffered(1)),
                  pl.BlockSpec((1, D_MODEL), row),
                  pl.BlockSpec((1, D_MODEL), row),
                  pl.BlockSpec((D_MODEL, LANES), row),
                  pl.BlockSpec((ROUTER_ROWS, LANES), row),
                  pl.BlockSpec((MOE_BLOCK, MOE_BLOCK), row)],
        out_specs=[pl.BlockSpec((tm, D_MODEL), tok),
                   pl.BlockSpec((8, tm), lambda i: (0, i)),
                   pl.BlockSpec((tm, LANES), tok),
                   pl.BlockSpec((OUT_BLOCKS_PER_STEP, 8, LANES), lambda i: (i, 0, 0))],
        out_shape=[jax.ShapeDtypeStruct((n, D_MODEL), F32),
                   jax.ShapeDtypeStruct((8, n), F32),
                   jax.ShapeDtypeStruct((n, LANES), F32),
                   jax.ShapeDtypeStruct((n // MOE_BLOCK, 8, LANES), jnp.int32)],
        scratch_shapes=[pltpu.VMEM((D_MODEL, D_MODEL), BF16)],
        compiler_params=pltpu.CompilerParams(dimension_semantics=("arbitrary",),
                                             vmem_limit_bytes=VMEM_LIMIT),
        name="out_proj_ln_router",
    )(a, s, x, w_out, g1, b1, rw, rb, tri)


def _moe_kernel(tab_ref, x_ref, rt_ref, r_ref, wg_ref, wu_ref, wd_ref, g_ref, b_ref, o_ref,
                xs_ref, ys_ref, cw_ref):
    blk = pl.program_id(0)
    x1 = x_ref[...]
    dest_row = rt_ref[0:1, :].astype(jnp.int32)
    rows_i = lax.broadcasted_iota(jnp.int32, (MOE_SORTED_ROWS, MOE_BLOCK), 0)
    p_in = jnp.where(rows_i == dest_row, 1.0, 0.0).astype(BF16)
    xs_ref[...] = jnp.dot(p_in, x1.astype(BF16), preferred_element_type=F32).astype(BF16)
    r_hi, r_lo = _split_bf16(r_ref[...])
    cw2 = jnp.dot(p_in, jnp.concatenate([r_hi, r_lo], axis=1), preferred_element_type=F32)
    cw_ref[...] = cw2[:, :LANES] + cw2[:, LANES:]
    ys_ref[...] = jnp.zeros_like(ys_ref)
    dest_col = r_ref[:, 0:1].astype(jnp.int32)
    cols_i = lax.broadcasted_iota(jnp.int32, (MOE_BLOCK, MOE_SORTED_ROWS), 1)
    p_out = jnp.where(cols_i == dest_col, 1.0, 0.0).astype(BF16)

    def run_experts(g, start, n_rows):
        off = pl.multiple_of(start, MOE_ALIGN)
        xs = xs_ref[pl.ds(off, n_rows), :]
        cw = cw_ref[pl.ds(off, n_rows), :]
        hg = jnp.dot(xs, wg_ref[g], preferred_element_type=F32)
        hu = jnp.dot(xs, wu_ref[g], preferred_element_type=F32)
        scale = jnp.concatenate([jnp.broadcast_to(cw[:, 1 + i:2 + i], (n_rows, D_EXPERT))
                                 for i in range(MOE_EXPERTS_PER_GROUP)], axis=1)
        act = hg * (1.0 / (1.0 + jnp.exp(-hg))) * hu * scale
        y = jnp.dot(act.astype(BF16), wd_ref[g], preferred_element_type=F32)
        ys_ref[pl.ds(off, n_rows), :] = y.astype(BF16)

    def group(g, carry):
        start = tab_ref[blk, TAB_START + g]
        piece = tab_ref[blk, TAB_PIECE + g]
        for k, n_rows in enumerate(MOE_PIECES):
            @pl.when(piece == k)
            def _(n_rows=n_rows):
                run_experts(g, start, n_rows)
        return carry

    lax.fori_loop(0, MOE_GROUPS, group, 0)

    half = MOE_BLOCK // 2
    ys = ys_ref[...]
    y_lo = jnp.dot(p_out[:half], ys, preferred_element_type=F32)
    y_hi = jnp.dot(p_out[half:], ys, preferred_element_type=F32)
    o_ref[:half, :] = _layer_norm_rows(DEEPNORM_ALPHA * x1[:half] + y_lo, g_ref[...], b_ref[...])
    o_ref[half:, :] = _layer_norm_rows(DEEPNORM_ALPHA * x1[half:] + y_hi, g_ref[...], b_ref[...])


def _moe(tab, x1, rout_t, rout, wg, wu, wd, g2, b2):
    n = x1.shape[0]
    tok = lambda i, t: (i, 0)
    row = lambda i, t: (0, 0)
    whole = lambda i, t: (0, 0, 0)
    resident = pl.Buffered(1)
    return pl.pallas_call(
        _moe_kernel,
        grid_spec=pltpu.PrefetchScalarGridSpec(
            num_scalar_prefetch=1,
            grid=(n // MOE_BLOCK,),
            in_specs=[pl.BlockSpec((MOE_BLOCK, D_MODEL), tok),
                      pl.BlockSpec((8, MOE_BLOCK), lambda i, t: (0, i)),
                      pl.BlockSpec((MOE_BLOCK, LANES), tok),
                      pl.BlockSpec((MOE_GROUPS, D_MODEL, D_GROUP), whole, pipeline_mode=resident),
                      pl.BlockSpec((MOE_GROUPS, D_MODEL, D_GROUP), whole, pipeline_mode=resident),
                      pl.BlockSpec((MOE_GROUPS, D_GROUP, D_MODEL), whole, pipeline_mode=resident),
                      pl.BlockSpec((1, D_MODEL), row),
                      pl.BlockSpec((1, D_MODEL), row)],
            out_specs=pl.BlockSpec((MOE_BLOCK, D_MODEL), tok),
            scratch_shapes=[pltpu.VMEM((MOE_SORTED_ROWS, D_MODEL), BF16),
                            pltpu.VMEM((MOE_SORTED_ROWS, D_MODEL), BF16),
                            pltpu.VMEM((MOE_SORTED_ROWS, LANES), F32)]),
        out_shape=jax.ShapeDtypeStruct((n, D_MODEL), F32),
        compiler_params=pltpu.CompilerParams(dimension_semantics=("parallel",),
                                             vmem_limit_bytes=VMEM_LIMIT),
        name="moe_experts_ln",
    )(tab, x1, rout_t, rout, wg, wu, wd, g2, b2)


def _router_weights(w_rg, b_rg, w_re, b_re):
    wt = jnp.zeros((ROUTER_ROWS, D_MODEL), F32)
    wt = wt.at[:MOE_GROUPS].set(w_rg.T.astype(F32))
    wt = wt.at[8:8 + MOE_EXPERTS].set(w_re.T.astype(F32))
    bt = jnp.zeros((ROUTER_ROWS,), F32)
    bt = bt.at[:MOE_GROUPS].set(b_rg.astype(F32))
    bt = bt.at[8:8 + MOE_EXPERTS].set(b_re.astype(F32))
    hi, lo = _split_bf16(wt)
    rw = jnp.concatenate([hi.T, lo.T, jnp.zeros((D_MODEL, LANES - 2 * ROUTER_ROWS), BF16)], axis=1)
    return rw, jnp.broadcast_to(bt[:, None], (ROUTER_ROWS, LANES))


def kernel(x, w_in, w_out, na_rel_bias, sgu_ln_g, sgu_ln_b, sgu_w, sgu_b, mix_norm_g, ln1_g, ln1_b, router_group_w, router_group_b, router_expert_w, router_expert_b, expert_w_gate, expert_w_up, expert_w_down, ln2_g, ln2_b):
    batch, seq, d = x.shape
    n = batch * seq
    xf = x.reshape(n, d).astype(F32)
    row = lambda a: a.astype(F32).reshape(1, -1)
    tri = jnp.asarray(np.triu(np.ones((MOE_BLOCK, MOE_BLOCK), np.float32), k=1), BF16)
    for l in range(DEPTH):
        q, k, v, gu, gv, wg, wu, wd = _in_proj(xf, w_in.astype(F32), expert_w_gate.astype(F32),
                                               expert_w_up.astype(F32), expert_w_down.astype(F32), l)
        a = _na_attention(q, k, v, _na_bias_table(na_rel_bias[l]), row(mix_norm_g[l, :D_NA]),
                          batch, seq)
        bs_full = jnp.repeat(sgu_b[l].astype(F32).T, SGU_GROUP_DIM, axis=1)
        s = _sgu(gu, gv, row(sgu_ln_g[l]), row(sgu_ln_b[l]), sgu_w[l].astype(BF16), bs_full,
                 row(mix_norm_g[l, D_NA:]))
        rw, rb = _router_weights(router_group_w[l], router_group_b[l],
                                           router_expert_w[l], router_expert_b[l])
        x1, rout_t, rout, tab = _out_proj(a, s, xf, w_out.astype(F32), l, row(ln1_g[l]),
                                          row(ln1_b[l]), rw, rb, tri)
        xf = _moe(tab[:, 0, :8], x1, rout_t, rout, wg, wu, wd, row(ln2_g[l]), row(ln2_b[l]))
    return xf.reshape(batch, seq, d).astype(x.dtype)
```

```python
import functools
import math

import numpy as np
import jax
import jax.numpy as jnp
from jax import lax
from jax.experimental import pallas as pl
from jax.experimental.pallas import tpu as pltpu

F32 = jnp.float32
BF16 = jnp.bfloat16

D_MODEL = 1024
DEPTH = 4
GRID_W = 64
NA_HEADS = 8
NA_HEAD_DIM = 64
NA_WIN_ROWS = 8
NA_WIN_COLS = 16
D_NA = NA_HEADS * NA_HEAD_DIM
SGU_GROUPS = 8
SGU_GROUP_DIM = 64
SGU_CHUNK = 128
D_SGU = SGU_GROUPS * SGU_GROUP_DIM
D_IN = 3 * D_NA + 2 * D_SGU
MOE_GROUPS = 4
MOE_EXPERTS_PER_GROUP = 4
MOE_EXPERTS = MOE_GROUPS * MOE_EXPERTS_PER_GROUP
D_EXPERT = 256
D_GROUP = MOE_EXPERTS_PER_GROUP * D_EXPERT
DEEPNORM_ALPHA = (2 * DEPTH) ** 0.25
LN_EPS = 1e-5

LANES = 128
HEAD_PAIRS = NA_HEADS // 2
NEG_BIG = -1e30
BAND_KEYS = NA_WIN_ROWS * GRID_W
BIAS_TILES = 7
ROUTER_ROWS = 32

TM_PROJ = 512
NA_ROWS_PER_STEP = 8
NA_ROWS_IN_FLIGHT = 4
NA_STRIP = 16
MOE_BLOCK = 512
MOE_ALIGN = 16
MOE_PIECES = (64, 128, 144, 160, 192, 256, 384, MOE_BLOCK)
MOE_SORTED_ROWS = 768
OUT_BLOCKS_PER_STEP = 2
TAB_START = 0
TAB_PIECE = MOE_GROUPS
VMEM_LIMIT = 56 * 1024 * 1024


def _gelu_tanh(x):
    c = math.sqrt(2.0 / math.pi)
    return x * (0.5 * (1.0 + jnp.tanh(c * (x + 0.044715 * (x * x * x)))))


def _layer_norm_rows(h, g, b):
    mu = jnp.mean(h, axis=-1, keepdims=True)
    hc = h - mu
    var = jnp.mean(hc * hc, axis=-1, keepdims=True)
    return hc * lax.rsqrt(var + LN_EPS) * g + b


def _in_proj_kernel(x_ref, w_ref, wg_ref, wu_ref, wd_ref, lng_ref, lnb_ref, ws_ref, bs_ref, gs_ref,
                    seg_ref, q_ref, k_ref, v_ref, s_ref, wg_out, wu_out, wd_out, wb_ref):
    @pl.when(pl.program_id(0) == 0)
    def _():
        wb_ref[...] = w_ref[0].astype(BF16)

    wg_out[0] = wg_ref[0].astype(BF16)
    wu_out[0] = wu_ref[0].astype(BF16)
    wd_out[...] = wd_ref[0].astype(BF16)

    xb = x_ref[...].astype(BF16)

    def mm(j):
        return jnp.dot(xb, wb_ref[:, j * D_NA:(j + 1) * D_NA], preferred_element_type=F32)

    hv, hu = mm(4), mm(3)

    def do_q():
        q_ref[...] = (mm(0) * (NA_HEAD_DIM ** -0.5)).astype(BF16)

    def do_k():
        k_ref[...] = mm(1).astype(BF16)

    def do_v():
        v_ref[...] = mm(2).astype(BF16)

    s_ref[...] = _spatial_gating_block(_gelu_tanh(hu), _gelu_tanh(hv), lng_ref, lnb_ref, ws_ref,
                                       bs_ref, gs_ref, seg_ref, (do_q, do_k, do_v))


def _in_proj(x, w_in, w_gate, w_up, w_down, layer, ln_g, ln_b, w_s_bf16, bs_full, gain):
    n = x.shape[0]
    steps = n // TM_PROJ
    gu_rows = MOE_EXPERTS * D_MODEL // steps
    parts = D_MODEL // gu_rows
    dn_rows = MOE_EXPERTS * D_EXPERT // steps
    assert gu_rows * steps == MOE_EXPERTS * D_MODEL and parts * gu_rows == D_MODEL
    assert dn_rows * steps == MOE_EXPERTS * D_EXPERT and TM_PROJ % SGU_CHUNK == 0
    same_group = np.arange(LANES)[:, None] // SGU_GROUP_DIM == np.arange(LANES)[None, :] // SGU_GROUP_DIM
    seg_ones = jnp.asarray(same_group, BF16)
    tok = lambda i: (i, 0)
    row = lambda i: (0, 0)
    slab = lambda i: (layer, i, 0)
    grouped = lambda i: (i // (parts * MOE_EXPERTS_PER_GROUP), i % parts,
                         (i // parts) % MOE_EXPERTS_PER_GROUP)
    depth = w_gate.shape[0]
    outs = pl.pallas_call(
        _in_proj_kernel,
        grid=(steps,),
        in_specs=[pl.BlockSpec((TM_PROJ, D_MODEL), tok),
                  pl.BlockSpec((1, D_MODEL, D_IN), lambda i: (layer, 0, 0),
                               pipeline_mode=pl.Buffered(1)),
                  pl.BlockSpec((1, gu_rows, D_EXPERT), slab),
                  pl.BlockSpec((1, gu_rows, D_EXPERT), slab),
                  pl.BlockSpec((1, dn_rows, D_MODEL), slab),
                  pl.BlockSpec((1, D_SGU), row),
                  pl.BlockSpec((1, D_SGU), row),
                  pl.BlockSpec((SGU_GROUPS, SGU_CHUNK, SGU_CHUNK), lambda i: (0, 0, 0)),
                  pl.BlockSpec((SGU_CHUNK, D_SGU), row),
                  pl.BlockSpec((1, D_SGU), row),
                  pl.BlockSpec((LANES, LANES), row)],
        out_specs=[pl.BlockSpec((TM_PROJ, D_NA), tok)] * 4
                  + [pl.BlockSpec((1, gu_rows, D_EXPERT), grouped)] * 2
                  + [pl.BlockSpec((dn_rows, D_MODEL), tok)],
        out_shape=[jax.ShapeDtypeStruct((n, D_NA), BF16)] * 4
                  + [jax.ShapeDtypeStruct((MOE_GROUPS, D_MODEL, D_GROUP), BF16)] * 2
                  + [jax.ShapeDtypeStruct((MOE_GROUPS * D_GROUP, D_MODEL), BF16)],
        scratch_shapes=[pltpu.VMEM((D_MODEL, D_IN), BF16)],
        compiler_params=pltpu.CompilerParams(dimension_semantics=("arbitrary",),
                                             vmem_limit_bytes=VMEM_LIMIT),
        name="in_proj_sgu",
    )(x, w_in,
      w_gate.reshape(depth, MOE_EXPERTS * D_MODEL, D_EXPERT),
      w_up.reshape(depth, MOE_EXPERTS * D_MODEL, D_EXPERT),
      w_down.reshape(depth, MOE_EXPERTS * D_EXPERT, D_MODEL),
      ln_g, ln_b, w_s_bf16, bs_full, gain, seg_ones)
    q, k, v, s, wg, wu, wd = outs
    return q, k, v, s, wg, wu, wd.reshape(MOE_GROUPS, D_GROUP, D_MODEL)


def _na_bias_table(rel_bias):
    cols = np.arange(GRID_W)
    col_start = np.clip(cols - NA_WIN_COLS // 2, 0, GRID_W - NA_WIN_COLS)
    kc = np.arange(GRID_W)
    valid = (kc[None, :] >= col_start[:, None]) & (kc[None, :] < col_start[:, None] + NA_WIN_COLS)
    dc = kc[None, :] - cols[:, None] + NA_WIN_COLS - 1
    onehot = (dc[None] == np.arange(2 * NA_WIN_COLS - 1)[:, None, None]) & valid[None]
    f = jnp.einsum('had,dck->hack', rel_bias.astype(F32), jnp.asarray(onehot, F32),
                   precision=lax.Precision.HIGHEST)
    f = jnp.where(valid[None, None], f, NEG_BIG)
    n_off = 2 * NA_WIN_ROWS - 1
    f = jnp.transpose(f, (0, 2, 1, 3)).reshape(NA_HEADS, GRID_W, n_off * GRID_W)
    g = f.reshape(HEAD_PAIRS, 2 * GRID_W, n_off * GRID_W)
    width = BIAS_TILES * LANES
    tabs = jnp.stack([g[:, :, :width], g[:, :, GRID_W:GRID_W + width]])
    tabs = tabs.reshape(2, HEAD_PAIRS, 2 * GRID_W, BIAS_TILES, LANES)
    return jnp.transpose(tabs, (0, 1, 3, 2, 4))


def _na_kernel(q_ref, k_ref, v_ref, t_ref, g_ref, o_ref, s_all, p_all, a_ref):
    rows = k_ref.shape[0] // GRID_W
    blk = pl.program_id(1)
    lane = lax.broadcasted_iota(jnp.int32, (GRID_W, LANES), 1)
    first_head = lane < NA_HEAD_DIM
    key_tiles = BAND_KEYS // LANES

    def one_row(rr, slot):
        s_ref = s_all.at[slot]
        p_ref = p_all.at[slot]
        r = blk * NA_ROWS_PER_STEP + rr
        rs = jnp.clip(r - NA_WIN_ROWS // 2, 0, rows - NA_WIN_ROWS)
        a0 = rs - r + (NA_WIN_ROWS - 1)
        par = a0 % 2
        j0 = a0 // 2
        q_off = pl.multiple_of(rr * GRID_W, GRID_W)
        k_off = pl.multiple_of(rs * GRID_W, GRID_W)
        for hp in range(HEAD_PAIRS):
            cs = slice(hp * LANES, (hp + 1) * LANES)
            qp = q_ref[pl.ds(q_off, GRID_W), cs]
            kp = k_ref[pl.ds(k_off, BAND_KEYS), cs]
            zero = jnp.zeros_like(qp)
            q2 = jnp.concatenate([jnp.where(first_head, qp, zero),
                                  jnp.where(first_head, zero, qp)], axis=0)
            s_ref[hp] = lax.dot_general(q2, kp, (((1,), (1,)), ((), ())),
                                        preferred_element_type=F32)
        for hp in range(HEAD_PAIRS):
            cs = slice(hp * LANES, (hp + 1) * LANES)
            inv_l = []
            for ch in range(2 * GRID_W // NA_STRIP):
                rsl = slice(ch * NA_STRIP, (ch + 1) * NA_STRIP)
                bias = jnp.concatenate([t_ref[par, hp, j0 + t, rsl, :] for t in range(key_tiles)],
                                       axis=1)
                sc = s_ref[hp, rsl, :] + bias
                m = jnp.max(sc, axis=1, keepdims=True)
                p = jnp.exp(sc - m)
                inv_l.append(1.0 / jnp.sum(p, axis=1, keepdims=True))
                p_ref[hp, rsl, :] = p.astype(BF16)
            vp = v_ref[pl.ds(k_off, BAND_KEYS), cs]
            o = jnp.dot(p_ref[hp], vp, preferred_element_type=F32)
            o = jnp.concatenate([o[ch * NA_STRIP:(ch + 1) * NA_STRIP] * inv_l[ch]
                                 for ch in range(len(inv_l))], axis=0)
            a_ref[pl.ds(q_off, GRID_W), cs] = jnp.where(first_head, o[:GRID_W], o[GRID_W:])

    def row_group(i, carry):
        for slot in range(NA_ROWS_IN_FLIGHT):
            one_row(i * NA_ROWS_IN_FLIGHT + slot, slot)
        return carry

    lax.fori_loop(0, NA_ROWS_PER_STEP // NA_ROWS_IN_FLIGHT, row_group, 0)

    a = a_ref[...]
    ms = jnp.mean(a * a, axis=-1, keepdims=True)
    o_ref[...] = (a * lax.rsqrt(ms + LN_EPS) * g_ref[...]).astype(o_ref.dtype)


def _na_attention(q, k, v, table, gain, batch, seq):
    n = q.shape[0]
    steps = seq // (GRID_W * NA_ROWS_PER_STEP)
    tq = NA_ROWS_PER_STEP * GRID_W
    return pl.pallas_call(
        _na_kernel,
        grid=(batch, steps),
        in_specs=[pl.BlockSpec((tq, D_NA), lambda b, i: (b * steps + i, 0)),
                  pl.BlockSpec((seq, D_NA), lambda b, i: (b, 0)),
                  pl.BlockSpec((seq, D_NA), lambda b, i: (b, 0)),
                  pl.BlockSpec(table.shape, lambda b, i: (0, 0, 0, 0, 0)),
                  pl.BlockSpec((1, D_NA), lambda b, i: (0, 0))],
        out_specs=pl.BlockSpec((tq, D_NA), lambda b, i: (b * steps + i, 0)),
        out_shape=jax.ShapeDtypeStruct((n, D_NA), BF16),
        scratch_shapes=[pltpu.VMEM((NA_ROWS_IN_FLIGHT, HEAD_PAIRS, 2 * GRID_W, BAND_KEYS), F32),
                        pltpu.VMEM((NA_ROWS_IN_FLIGHT, HEAD_PAIRS, 2 * GRID_W, BAND_KEYS), BF16),
                        pltpu.VMEM((tq, D_NA), F32)],
        compiler_params=pltpu.CompilerParams(dimension_semantics=("parallel", "parallel"),
                                             vmem_limit_bytes=VMEM_LIMIT),
        name="na_attention",
    )(q, k, v, table, gain)


def _spatial_gating_block(u, vs, lng_ref, lnb_ref, ws_ref, bs_ref, g_ref, seg_ref, between):
    tm = u.shape[0]
    lane = lax.broadcasted_iota(jnp.int32, (SGU_CHUNK, LANES), 1)
    first = lane < SGU_GROUP_DIM
    inv = 1.0 / SGU_GROUP_DIM
    n_tiles = D_SGU // LANES
    n_rows = tm * n_tiles

    def seg_mean(t):
        hi, lo = _split_bf16(t)
        s = jnp.dot(jnp.concatenate([hi, lo], axis=0), seg_ref[...], preferred_element_type=F32)
        return (s[:n_rows] + s[n_rows:]) * inv

    x = jnp.concatenate([vs[:, j * LANES:(j + 1) * LANES] for j in range(n_tiles)], axis=0)
    mean = seg_mean(x)
    between[0]()
    xc = x - mean
    var = seg_mean(xc * xc)
    between[1]()
    xn = xc * lax.rsqrt(var + LN_EPS)

    mixes = {}
    for c in range(tm // SGU_CHUNK):
        for j in range(n_tiles):
            cs = slice(j * LANES, (j + 1) * LANES)
            r0 = j * tm + c * SGU_CHUNK
            y = (xn[r0:r0 + SGU_CHUNK] * lng_ref[:, cs] + lnb_ref[:, cs]).astype(BF16)
            mixes[c, j] = jnp.dot(jnp.concatenate([ws_ref[2 * j], ws_ref[2 * j + 1]], axis=0), y,
                                  preferred_element_type=F32)
    between[2]()

    out = []
    for c in range(tm // SGU_CHUNK):
        rs = slice(c * SGU_CHUNK, (c + 1) * SGU_CHUNK)
        tiles = []
        for j in range(n_tiles):
            cs = slice(j * LANES, (j + 1) * LANES)
            m = mixes[c, j]
            mixed = jnp.where(first, m[:SGU_CHUNK], m[SGU_CHUNK:]) + bs_ref[:, cs]
            tiles.append(u[rs, cs] * mixed)
        so = jnp.concatenate(tiles, axis=1)
        ms = jnp.mean(so * so, axis=-1, keepdims=True)
        out.append((so * lax.rsqrt(ms + LN_EPS) * g_ref[...]).astype(BF16))
    return jnp.concatenate(out, axis=0)


def _split_bf16(a):
    hi = a.astype(BF16)
    lo = (a - hi.astype(F32)).astype(BF16)
    return hi, lo


def _route(lg, tri):
    gl = [lg[g:g + 1, :] for g in range(MOE_GROUPS)]
    gmax = functools.reduce(jnp.maximum, gl)
    gidx = jnp.full(gmax.shape, MOE_GROUPS - 1, jnp.int32)
    for g in range(MOE_GROUPS - 2, -1, -1):
        gidx = jnp.where(gl[g] == gmax, g, gidx)
    denom = functools.reduce(lambda a, b: a + b, [jnp.exp(t - gmax) for t in gl])
    gate = 1.0 / denom

    def expert_logit(i):
        rows_ = [lg[8 + MOE_EXPERTS_PER_GROUP * g + i:9 + MOE_EXPERTS_PER_GROUP * g + i, :]
                 for g in range(MOE_GROUPS)]
        sel = rows_[MOE_GROUPS - 1]
        for g in range(MOE_GROUPS - 2, -1, -1):
            sel = jnp.where(gidx == g, rows_[g], sel)
        return sel

    el = [expert_logit(i) for i in range(MOE_EXPERTS_PER_GROUP)]
    v1 = functools.reduce(jnp.maximum, el)
    i1 = jnp.full(v1.shape, MOE_EXPERTS_PER_GROUP - 1, jnp.int32)
    for i in range(MOE_EXPERTS_PER_GROUP - 2, -1, -1):
        i1 = jnp.where(el[i] == v1, i, i1)
    rest = [jnp.where(i1 == i, -jnp.inf, el[i]) for i in range(MOE_EXPERTS_PER_GROUP)]
    v2 = functools.reduce(jnp.maximum, rest)
    i2 = jnp.full(v2.shape, MOE_EXPERTS_PER_GROUP - 1, jnp.int32)
    for i in range(MOE_EXPERTS_PER_GROUP - 2, -1, -1):
        i2 = jnp.where((rest[i] == v2) & (i1 != i), i, i2)
    e2 = jnp.exp(v2 - v1)
    w1 = 1.0 / (1.0 + e2)
    w2 = e2 * w1
    within = [jnp.where(i1 == i, w1, 0.0) + jnp.where(i2 == i, w2, 0.0)
              for i in range(MOE_EXPERTS_PER_GROUP)]
    tm = MOE_BLOCK
    cw = [within[i] * gate for i in range(MOE_EXPERTS_PER_GROUP)]

    onehot = [jnp.where(gidx == g, 1.0, 0.0) for g in range(MOE_GROUPS)]
    oh_mat = jnp.concatenate(onehot + [jnp.zeros((16 - MOE_GROUPS, tm), F32)], axis=0)
    before = jnp.dot(oh_mat.astype(BF16), tri, preferred_element_type=F32)
    rank = functools.reduce(lambda a, b: a + b,
                            [onehot[g] * before[g:g + 1, :] for g in range(MOE_GROUPS)])
    starts, pieces = [], []
    start_tok = jnp.zeros_like(rank)
    end = jnp.zeros((1, 1), F32)
    for g in range(MOE_GROUPS):
        count = jnp.sum(onehot[g], axis=1, keepdims=True)
        start_tok = start_tok + onehot[g] * end
        starts.append(end)
        end = end + MOE_ALIGN * jnp.floor((count + (MOE_ALIGN - 1)) * (1.0 / MOE_ALIGN))
        piece = functools.reduce(lambda a, b: a + b,
                                 [jnp.where(count > p, 1.0, 0.0) for p in MOE_PIECES[:-1]])
        pieces.append(jnp.where(count > 0.0, piece, -1.0))
    dest = rank + start_tok
    rout_t = jnp.concatenate([dest] + cw + [gidx.astype(F32), jnp.zeros((2, tm), F32)], axis=0)

    c = lax.broadcasted_iota(jnp.int32, (8, LANES), 1)
    tab = jnp.zeros((8, LANES), F32)
    for g in range(MOE_GROUPS):
        tab = jnp.where(c == TAB_START + g, starts[g], tab)
        tab = jnp.where(c == TAB_PIECE + g, pieces[g], tab)
    return rout_t, tab.astype(jnp.int32)


def _out_proj_kernel(a_ref, s_ref, x_ref, w_ref, g_ref, b_ref, rw_ref, rb_ref, tri_ref,
                     x1_ref, rt_ref, r_ref, tab_ref, wb_ref):
    @pl.when(pl.program_id(0) == 0)
    def _():
        wb_ref[...] = w_ref[0].astype(BF16)

    mixes = []
    for blk in range(OUT_BLOCKS_PER_STEP):
        rs = slice(blk * MOE_BLOCK, (blk + 1) * MOE_BLOCK)
        mix = jnp.dot(a_ref[rs, :], wb_ref[:D_NA, :], preferred_element_type=F32)
        mixes.append(mix + jnp.dot(s_ref[rs, :], wb_ref[D_NA:, :], preferred_element_type=F32))
    logits = []
    for blk in range(OUT_BLOCKS_PER_STEP):
        rs = slice(blk * MOE_BLOCK, (blk + 1) * MOE_BLOCK)
        x1 = _layer_norm_rows(DEEPNORM_ALPHA * x_ref[rs, :] + mixes[blk], g_ref[...], b_ref[...])
        x1_ref[rs, :] = x1
        xh, xl = _split_bf16(x1)
        logits.append(jnp.dot(xh, rw_ref[...], preferred_element_type=F32)
                      + jnp.dot(xl, rw_ref[...], preferred_element_type=F32))
    for blk in range(OUT_BLOCKS_PER_STEP):
        rs = slice(blk * MOE_BLOCK, (blk + 1) * MOE_BLOCK)
        lg_t = logits[blk].T
        lg = lg_t[:ROUTER_ROWS] + lg_t[ROUTER_ROWS:2 * ROUTER_ROWS] + rb_ref[:, 0:1]
        rout_t, tab = _route(lg, tri_ref[...])
        rt_ref[:, rs] = rout_t
        r_ref[rs, :] = jnp.concatenate([rout_t, jnp.zeros((LANES - 8, MOE_BLOCK), F32)], axis=0).T
        tab_ref[blk] = tab


def _out_proj(a, s, x, w_out, layer, g1, b1, rw, rb, tri):
    n = x.shape[0]
    tok = lambda i: (i, 0)
    row = lambda i: (0, 0)
    tm = OUT_BLOCKS_PER_STEP * MOE_BLOCK
    return pl.pallas_call(
        _out_proj_kernel,
        grid=(n // tm,),
        in_specs=[pl.BlockSpec((tm, D_NA), tok),
                  pl.BlockSpec((tm, D_SGU), tok),
                  pl.BlockSpec((tm, D_MODEL), tok),
                  pl.BlockSpec((1, D_MODEL, D_MODEL), lambda i: (layer, 0, 0),
                               pipeline_mode=pl.Buffered(1)),
                  pl.BlockSpec((1, D_MODEL), row),
                  pl.BlockSpec((1, D_MODEL), row),
                  pl.BlockSpec((D_MODEL, LANES), row),
                  pl.BlockSpec((ROUTER_ROWS, LANES), row),
                  pl.BlockSpec((MOE_BLOCK, MOE_BLOCK), row)],
        out_specs=[pl.BlockSpec((tm, D_MODEL), tok),
                   pl.BlockSpec((8, tm), lambda i: (0, i)),
                   pl.BlockSpec((tm, LANES), tok),
                   pl.BlockSpec((OUT_BLOCKS_PER_STEP, 8, LANES), lambda i: (i, 0, 0))],
        out_shape=[jax.ShapeDtypeStruct((n, D_MODEL), F32),
                   jax.ShapeDtypeStruct((8, n), F32),
                   jax.ShapeDtypeStruct((n, LANES), F32),
                   jax.ShapeDtypeStruct((n // MOE_BLOCK, 8, LANES), jnp.int32)],
        scratch_shapes=[pltpu.VMEM((D_MODEL, D_MODEL), BF16)],
        compiler_params=pltpu.CompilerParams(dimension_semantics=("arbitrary",),
                                             vmem_limit_bytes=VMEM_LIMIT),
        name="out_proj_ln_router",
    )(a, s, x, w_out, g1, b1, rw, rb, tri)


def _moe_kernel(tab_ref, x_ref, rt_ref, r_ref, wg_ref, wu_ref, wd_ref, g_ref, b_ref, o_ref,
                xs_ref, ys_ref, cw_ref):
    blk = pl.program_id(0)
    x1 = x_ref[...]
    dest_row = rt_ref[0:1, :].astype(jnp.int32)
    rows_i = lax.broadcasted_iota(jnp.int32, (MOE_SORTED_ROWS, MOE_BLOCK), 0)
    p_in = jnp.where(rows_i == dest_row, 1.0, 0.0).astype(BF16)
    xs_ref[...] = jnp.dot(p_in, x1.astype(BF16), preferred_element_type=F32).astype(BF16)
    r_hi, r_lo = _split_bf16(r_ref[...])
    cw2 = jnp.dot(p_in, jnp.concatenate([r_hi, r_lo], axis=1), preferred_element_type=F32)
    cw_ref[...] = cw2[:, :LANES] + cw2[:, LANES:]
    ys_ref[...] = jnp.zeros_like(ys_ref)
    dest_col = r_ref[:, 0:1].astype(jnp.int32)
    cols_i = lax.broadcasted_iota(jnp.int32, (MOE_BLOCK, MOE_SORTED_ROWS), 1)
    p_out = jnp.where(cols_i == dest_col, 1.0, 0.0).astype(BF16)

    def run_experts(g, start, n_rows):
        off = pl.multiple_of(start, MOE_ALIGN)
        xs = xs_ref[pl.ds(off, n_rows), :]
        cw = cw_ref[pl.ds(off, n_rows), :]
        hg = jnp.dot(xs, wg_ref[g], preferred_element_type=F32)
        hu = jnp.dot(xs, wu_ref[g], preferred_element_type=F32)
        scale = jnp.concatenate([jnp.broadcast_to(cw[:, 1 + i:2 + i], (n_rows, D_EXPERT))
                                 for i in range(MOE_EXPERTS_PER_GROUP)], axis=1)
        act = hg * (1.0 / (1.0 + jnp.exp(-hg))) * hu * scale
        y = jnp.dot(act.astype(BF16), wd_ref[g], preferred_element_type=F32)
        ys_ref[pl.ds(off, n_rows), :] = y.astype(BF16)

    def group(g, carry):
        start = tab_ref[blk, TAB_START + g]
        piece = tab_ref[blk, TAB_PIECE + g]
        for k, n_rows in enumerate(MOE_PIECES):
            @pl.when(piece == k)
            def _(n_rows=n_rows):
                run_experts(g, start, n_rows)
        return carry

    lax.fori_loop(0, MOE_GROUPS, group, 0)

    half = MOE_BLOCK // 2
    ys = ys_ref[...]
    y_lo = jnp.dot(p_out[:half], ys, preferred_element_type=F32)
    y_hi = jnp.dot(p_out[half:], ys, preferred_element_type=F32)
    o_ref[:half, :] = _layer_norm_rows(DEEPNORM_ALPHA * x1[:half] + y_lo, g_ref[...], b_ref[...])
    o_ref[half:, :] = _layer_norm_rows(DEEPNORM_ALPHA * x1[half:] + y_hi, g_ref[...], b_ref[...])


def _moe(tab, x1, rout_t, rout, wg, wu, wd, g2, b2):
    n = x1.shape[0]
    tok = lambda i, t: (i, 0)
    row = lambda i, t: (0, 0)
    whole = lambda i, t: (0, 0, 0)
    resident = pl.Buffered(1)
    return pl.pallas_call(
        _moe_kernel,
        grid_spec=pltpu.PrefetchScalarGridSpec(
            num_scalar_prefetch=1,
            grid=(n // MOE_BLOCK,),
            in_specs=[pl.BlockSpec((MOE_BLOCK, D_MODEL), tok),
                      pl.BlockSpec((8, MOE_BLOCK), lambda i, t: (0, i)),
                      pl.BlockSpec((MOE_BLOCK, LANES), tok),
                      pl.BlockSpec((MOE_GROUPS, D_MODEL, D_GROUP), whole, pipeline_mode=resident),
                      pl.BlockSpec((MOE_GROUPS, D_MODEL, D_GROUP), whole, pipeline_mode=resident),
                      pl.BlockSpec((MOE_GROUPS, D_GROUP, D_MODEL), whole, pipeline_mode=resident),
                      pl.BlockSpec((1, D_MODEL), row),
                      pl.BlockSpec((1, D_MODEL), row)],
            out_specs=pl.BlockSpec((MOE_BLOCK, D_MODEL), tok),
            scratch_shapes=[pltpu.VMEM((MOE_SORTED_ROWS, D_MODEL), BF16),
                            pltpu.VMEM((MOE_SORTED_ROWS, D_MODEL), BF16),
                            pltpu.VMEM((MOE_SORTED_ROWS, LANES), F32)]),
        out_shape=jax.ShapeDtypeStruct((n, D_MODEL), F32),
        compiler_params=pltpu.CompilerParams(dimension_semantics=("parallel",),
                                             vmem_limit_bytes=VMEM_LIMIT),
        name="moe_experts_ln",
    )(tab, x1, rout_t, rout, wg, wu, wd, g2, b2)


def _router_weights(w_rg, b_rg, w_re, b_re):
    wt = jnp.zeros((ROUTER_ROWS, D_MODEL), F32)
    wt = wt.at[:MOE_GROUPS].set(w_rg.T.astype(F32))
    wt = wt.at[8:8 + MOE_EXPERTS].set(w_re.T.astype(F32))
    bt = jnp.zeros((ROUTER_ROWS,), F32)
    bt = bt.at[:MOE_GROUPS].set(b_rg.astype(F32))
    bt = bt.at[8:8 + MOE_EXPERTS].set(b_re.astype(F32))
    hi, lo = _split_bf16(wt)
    rw = jnp.concatenate([hi.T, lo.T, jnp.zeros((D_MODEL, LANES - 2 * ROUTER_ROWS), BF16)], axis=1)
    return rw, jnp.broadcast_to(bt[:, None], (ROUTER_ROWS, LANES))


def kernel(x, w_in, w_out, na_rel_bias, sgu_ln_g, sgu_ln_b, sgu_w, sgu_b, mix_norm_g, ln1_g, ln1_b, router_group_w, router_group_b, router_expert_w, router_expert_b, expert_w_gate, expert_w_up, expert_w_down, ln2_g, ln2_b):
    batch, seq, d = x.shape
    n = batch * seq
    xf = x.reshape(n, d).astype(F32)
    row = lambda a: a.astype(F32).reshape(1, -1)
    tri = jnp.asarray(np.triu(np.ones((MOE_BLOCK, MOE_BLOCK), np.float32), k=1), BF16)
    for l in range(DEPTH):
        bs_full = jnp.repeat(sgu_b[l].astype(F32).T, SGU_GROUP_DIM, axis=1)
        q, k, v, s, wg, wu, wd = _in_proj(xf, w_in.astype(F32), expert_w_gate.astype(F32),
                                          expert_w_up.astype(F32), expert_w_down.astype(F32), l,
                                          row(sgu_ln_g[l]), row(sgu_ln_b[l]), sgu_w[l].astype(BF16),
                                          bs_full, row(mix_norm_g[l, D_NA:]))
        a = _na_attention(q, k, v, _na_bias_table(na_rel_bias[l]), row(mix_norm_g[l, :D_NA]),
                          batch, seq)
        rw, rb = _router_weights(router_group_w[l], router_group_b[l],
                                           router_expert_w[l], router_expert_b[l])
        x1, rout_t, rout, tab = _out_proj(a, s, xf, w_out.astype(F32), l, row(ln1_g[l]),
                                          row(ln1_b[l]), rw, rb, tri)
        xf = _moe(tab[:, 0, :8], x1, rout_t, rout, wg, wu, wd, row(ln2_g[l]), row(ln2_b[l]))
    return xf.reshape(batch, seq, d).astype(x.dtype)
```

```python
import functools
import math

import numpy as np
import jax
import jax.numpy as jnp
from jax import lax
from jax.experimental import pallas as pl
from jax.experimental.pallas import tpu as pltpu

F32 = jnp.float32
BF16 = jnp.bfloat16

D_MODEL = 1024
DEPTH = 4
GRID_W = 64
NA_HEADS = 8
NA_HEAD_DIM = 64
NA_WIN_ROWS = 8
NA_WIN_COLS = 16
D_NA = NA_HEADS * NA_HEAD_DIM
SGU_GROUPS = 8
SGU_GROUP_DIM = 64
SGU_CHUNK = 128
D_SGU = SGU_GROUPS * SGU_GROUP_DIM
D_IN = 3 * D_NA + 2 * D_SGU
MOE_GROUPS = 4
MOE_EXPERTS_PER_GROUP = 4
MOE_EXPERTS = MOE_GROUPS * MOE_EXPERTS_PER_GROUP
D_EXPERT = 256
D_GROUP = MOE_EXPERTS_PER_GROUP * D_EXPERT
DEEPNORM_ALPHA = (2 * DEPTH) ** 0.25
LN_EPS = 1e-5

LANES = 128
HEAD_PAIRS = NA_HEADS // 2
NEG_BIG = -1e30
BAND_KEYS = NA_WIN_ROWS * GRID_W
BIAS_TILES = 7
ROUTER_ROWS = 32

TM_PROJ = 512
NA_ROWS_PER_STEP = 8
NA_ROWS_IN_FLIGHT = 8
NA_STRIP = 16
MOE_BLOCK = 512
MOE_ALIGN = 16
MOE_PIECES = (64, 128, 144, 160, 192, 256, 384, MOE_BLOCK)
MOE_SORTED_ROWS = 768
OUT_BLOCKS_PER_STEP = 2
TAB_START = 0
TAB_PIECE = MOE_GROUPS
VMEM_LIMIT = 56 * 1024 * 1024


def _gelu_tanh(x):
    c = math.sqrt(2.0 / math.pi)
    return x * (0.5 * (1.0 + jnp.tanh(c * (x + 0.044715 * (x * x * x)))))


def _layer_norm_rows(h, g, b):
    mu = jnp.mean(h, axis=-1, keepdims=True)
    hc = h - mu
    var = jnp.mean(hc * hc, axis=-1, keepdims=True)
    return hc * lax.rsqrt(var + LN_EPS) * g + b


def _in_proj_kernel(x_ref, w_ref, wg_ref, wu_ref, wd_ref, lng_ref, lnb_ref, ws_ref, bs_ref, gs_ref,
                    seg_ref, q_ref, k_ref, v_ref, s_ref, wg_out, wu_out, wd_out, wb_ref):
    @pl.when(pl.program_id(0) == 0)
    def _():
        wb_ref[...] = w_ref[0].astype(BF16)

    wg_out[0] = wg_ref[0].astype(BF16)
    wu_out[0] = wu_ref[0].astype(BF16)
    wd_out[...] = wd_ref[0].astype(BF16)

    xb = x_ref[...].astype(BF16)

    def mm(j):
        return jnp.dot(xb, wb_ref[:, j * D_NA:(j + 1) * D_NA], preferred_element_type=F32)

    hv, hu = mm(4), mm(3)

    def do_q():
        q_ref[...] = (mm(0) * (NA_HEAD_DIM ** -0.5)).astype(BF16)

    def do_k():
        k_ref[...] = mm(1).astype(BF16)

    def do_v():
        v_ref[...] = mm(2).astype(BF16)

    s_ref[...] = _spatial_gating_block(_gelu_tanh(hu), _gelu_tanh(hv), lng_ref, lnb_ref, ws_ref,
                                       bs_ref, gs_ref, seg_ref, (do_q, do_k, do_v))


def _in_proj(x, w_in, w_gate, w_up, w_down, layer, ln_g, ln_b, w_s_bf16, bs_full, gain):
    n = x.shape[0]
    steps = n // TM_PROJ
    gu_rows = MOE_EXPERTS * D_MODEL // steps
    parts = D_MODEL // gu_rows
    dn_rows = MOE_EXPERTS * D_EXPERT // steps
    assert gu_rows * steps == MOE_EXPERTS * D_MODEL and parts * gu_rows == D_MODEL
    assert dn_rows * steps == MOE_EXPERTS * D_EXPERT and TM_PROJ % SGU_CHUNK == 0
    same_group = np.arange(LANES)[:, None] // SGU_GROUP_DIM == np.arange(LANES)[None, :] // SGU_GROUP_DIM
    seg_ones = jnp.asarray(same_group, BF16)
    tok = lambda i: (i, 0)
    row = lambda i: (0, 0)
    slab = lambda i: (layer, i, 0)
    grouped = lambda i: (i // (parts * MOE_EXPERTS_PER_GROUP), i % parts,
                         (i // parts) % MOE_EXPERTS_PER_GROUP)
    depth = w_gate.shape[0]
    outs = pl.pallas_call(
        _in_proj_kernel,
        grid=(steps,),
        in_specs=[pl.BlockSpec((TM_PROJ, D_MODEL), tok),
                  pl.BlockSpec((1, D_MODEL, D_IN), lambda i: (layer, 0, 0),
                               pipeline_mode=pl.Buffered(1)),
                  pl.BlockSpec((1, gu_rows, D_EXPERT), slab),
                  pl.BlockSpec((1, gu_rows, D_EXPERT), slab),
                  pl.BlockSpec((1, dn_rows, D_MODEL), slab),
                  pl.BlockSpec((1, D_SGU), row),
                  pl.BlockSpec((1, D_SGU), row),
                  pl.BlockSpec((SGU_GROUPS, SGU_CHUNK, SGU_CHUNK), lambda i: (0, 0, 0)),
                  pl.BlockSpec((SGU_CHUNK, D_SGU), row),
                  pl.BlockSpec((1, D_SGU), row),
                  pl.BlockSpec((LANES, LANES), row)],
        out_specs=[pl.BlockSpec((TM_PROJ, D_NA), tok)] * 4
                  + [pl.BlockSpec((1, gu_rows, D_EXPERT), grouped)] * 2
                  + [pl.BlockSpec((dn_rows, D_MODEL), tok)],
        out_shape=[jax.ShapeDtypeStruct((n, D_NA), BF16)] * 4
                  + [jax.ShapeDtypeStruct((MOE_GROUPS, D_MODEL, D_GROUP), BF16)] * 2
                  + [jax.ShapeDtypeStruct((MOE_GROUPS * D_GROUP, D_MODEL), BF16)],
        scratch_shapes=[pltpu.VMEM((D_MODEL, D_IN), BF16)],
        compiler_params=pltpu.CompilerParams(dimension_semantics=("arbitrary",),
                                             vmem_limit_bytes=VMEM_LIMIT),
        name="in_proj_sgu",
    )(x, w_in,
      w_gate.reshape(depth, MOE_EXPERTS * D_MODEL, D_EXPERT),
      w_up.reshape(depth, MOE_EXPERTS * D_MODEL, D_EXPERT),
      w_down.reshape(depth, MOE_EXPERTS * D_EXPERT, D_MODEL),
      ln_g, ln_b, w_s_bf16, bs_full, gain, seg_ones)
    q, k, v, s, wg, wu, wd = outs
    return q, k, v, s, wg, wu, wd.reshape(MOE_GROUPS, D_GROUP, D_MODEL)


def _na_bias_tables(rel_bias):
    cols = np.arange(GRID_W)
    col_start = np.clip(cols - NA_WIN_COLS // 2, 0, GRID_W - NA_WIN_COLS)
    kc = np.arange(GRID_W)
    valid = (kc[None, :] >= col_start[:, None]) & (kc[None, :] < col_start[:, None] + NA_WIN_COLS)
    dc = kc[None, :] - cols[:, None] + NA_WIN_COLS - 1
    onehot = (dc[None] == np.arange(2 * NA_WIN_COLS - 1)[:, None, None]) & valid[None]
    depth = rel_bias.shape[0]
    f = jnp.einsum('lhad,dck->lhack', rel_bias.astype(F32), jnp.asarray(onehot, F32),
                   precision=lax.Precision.HIGHEST)
    f = jnp.where(valid[None, None, None], f, NEG_BIG)
    n_off = 2 * NA_WIN_ROWS - 1
    f = jnp.transpose(f, (0, 1, 3, 2, 4)).reshape(depth, NA_HEADS, GRID_W, n_off * GRID_W)
    g = f.reshape(depth, HEAD_PAIRS, 2 * GRID_W, n_off * GRID_W)
    width = BIAS_TILES * LANES
    tabs = jnp.stack([g[..., :width], g[..., GRID_W:GRID_W + width]], axis=1)
    tabs = tabs.reshape(depth, 2, HEAD_PAIRS, 2 * GRID_W, BIAS_TILES, LANES)
    return jnp.transpose(tabs, (0, 1, 2, 4, 3, 5))


def _na_kernel(q_ref, k_ref, v_ref, t_ref, g_ref, o_ref, s_all, p_all, a_ref):
    rows = k_ref.shape[0] // GRID_W
    blk = pl.program_id(1)
    lane = lax.broadcasted_iota(jnp.int32, (GRID_W, LANES), 1)
    first_head = lane < NA_HEAD_DIM
    key_tiles = BAND_KEYS // LANES

    def row_geometry(rr):
        r = blk * NA_ROWS_PER_STEP + rr
        rs = jnp.clip(r - NA_WIN_ROWS // 2, 0, rows - NA_WIN_ROWS)
        a0 = rs - r + (NA_WIN_ROWS - 1)
        q_off = pl.multiple_of(rr * GRID_W, GRID_W)
        k_off = pl.multiple_of(rs * GRID_W, GRID_W)
        return a0 % 2, a0 // 2, q_off, k_off

    def row_scores(rr, slot):
        s_ref = s_all.at[slot]
        _, _, q_off, k_off = row_geometry(rr)
        for hp in range(HEAD_PAIRS):
            cs = slice(hp * LANES, (hp + 1) * LANES)
            qp = q_ref[pl.ds(q_off, GRID_W), cs]
            kp = k_ref[pl.ds(k_off, BAND_KEYS), cs]
            zero = jnp.zeros_like(qp)
            q2 = jnp.concatenate([jnp.where(first_head, qp, zero),
                                  jnp.where(first_head, zero, qp)], axis=0)
            s_ref[hp] = lax.dot_general(q2, kp, (((1,), (1,)), ((), ())),
                                        preferred_element_type=F32)

    def row_softmax_pv(rr, slot):
        s_ref = s_all.at[slot]
        p_ref = p_all.at[slot]
        par, j0, q_off, k_off = row_geometry(rr)
        for hp in range(HEAD_PAIRS):
            cs = slice(hp * LANES, (hp + 1) * LANES)
            inv_l = []
            for ch in range(2 * GRID_W // NA_STRIP):
                rsl = slice(ch * NA_STRIP, (ch + 1) * NA_STRIP)
                bias = jnp.concatenate([t_ref[0, par, hp, j0 + t, rsl, :] for t in range(key_tiles)],
                                       axis=1)
                sc = s_ref[hp, rsl, :] + bias
                m = jnp.max(sc, axis=1, keepdims=True)
                p = jnp.exp(sc - m)
                inv_l.append(1.0 / jnp.sum(p, axis=1, keepdims=True))
                p_ref[hp, rsl, :] = p.astype(BF16)
            vp = v_ref[pl.ds(k_off, BAND_KEYS), cs]
            o = jnp.dot(p_ref[hp], vp, preferred_element_type=F32)
            o = jnp.concatenate([o[ch * NA_STRIP:(ch + 1) * NA_STRIP] * inv_l[ch]
                                 for ch in range(len(inv_l))], axis=0)
            a_ref[pl.ds(q_off, GRID_W), cs] = jnp.where(first_head, o[:GRID_W], o[GRID_W:])

    def row_group(i, carry):
        for slot in range(NA_ROWS_IN_FLIGHT):
            row_scores(i * NA_ROWS_IN_FLIGHT + slot, slot)
            row_softmax_pv(i * NA_ROWS_IN_FLIGHT + slot, slot)
        return carry

    lax.fori_loop(0, NA_ROWS_PER_STEP // NA_ROWS_IN_FLIGHT, row_group, 0)

    a = a_ref[...]
    ms = jnp.mean(a * a, axis=-1, keepdims=True)
    o_ref[...] = (a * lax.rsqrt(ms + LN_EPS) * g_ref[...]).astype(o_ref.dtype)


def _na_attention(q, k, v, tables, layer, gain, batch, seq):
    n = q.shape[0]
    steps = seq // (GRID_W * NA_ROWS_PER_STEP)
    tq = NA_ROWS_PER_STEP * GRID_W
    return pl.pallas_call(
        _na_kernel,
        grid=(batch, steps),
        in_specs=[pl.BlockSpec((tq, D_NA), lambda b, i: (b * steps + i, 0)),
                  pl.BlockSpec((seq, D_NA), lambda b, i: (b, 0)),
                  pl.BlockSpec((seq, D_NA), lambda b, i: (b, 0)),
                  pl.BlockSpec((1,) + tables.shape[1:], lambda b, i: (layer, 0, 0, 0, 0, 0)),
                  pl.BlockSpec((1, D_NA), lambda b, i: (0, 0))],
        out_specs=pl.BlockSpec((tq, D_NA), lambda b, i: (b * steps + i, 0)),
        out_shape=jax.ShapeDtypeStruct((n, D_NA), BF16),
        scratch_shapes=[pltpu.VMEM((NA_ROWS_IN_FLIGHT, HEAD_PAIRS, 2 * GRID_W, BAND_KEYS), F32),
                        pltpu.VMEM((NA_ROWS_IN_FLIGHT, HEAD_PAIRS, 2 * GRID_W, BAND_KEYS), BF16),
                        pltpu.VMEM((tq, D_NA), F32)],
        compiler_params=pltpu.CompilerParams(dimension_semantics=("parallel", "parallel"),
                                             vmem_limit_bytes=VMEM_LIMIT),
        name="na_attention",
    )(q, k, v, tables, gain)


def _spatial_gating_block(u, vs, lng_ref, lnb_ref, ws_ref, bs_ref, g_ref, seg_ref, between):
    tm = u.shape[0]
    lane = lax.broadcasted_iota(jnp.int32, (SGU_CHUNK, LANES), 1)
    first = lane < SGU_GROUP_DIM
    inv = 1.0 / SGU_GROUP_DIM
    n_tiles = D_SGU // LANES
    n_rows = tm * n_tiles

    def seg_mean(t):
        hi, lo = _split_bf16(t)
        s = jnp.dot(jnp.concatenate([hi, lo], axis=0), seg_ref[...], preferred_element_type=F32)
        return (s[:n_rows] + s[n_rows:]) * inv

    x = jnp.concatenate([vs[:, j * LANES:(j + 1) * LANES] for j in range(n_tiles)], axis=0)
    mean = seg_mean(x)
    between[0]()
    xc = x - mean
    var = seg_mean(xc * xc)
    between[1]()
    xn = xc * lax.rsqrt(var + LN_EPS)

    mixes = {}
    for c in range(tm // SGU_CHUNK):
        for j in range(n_tiles):
            cs = slice(j * LANES, (j + 1) * LANES)
            r0 = j * tm + c * SGU_CHUNK
            y = (xn[r0:r0 + SGU_CHUNK] * lng_ref[:, cs] + lnb_ref[:, cs]).astype(BF16)
            mixes[c, j] = jnp.dot(jnp.concatenate([ws_ref[2 * j], ws_ref[2 * j + 1]], axis=0), y,
                                  preferred_element_type=F32)
    between[2]()

    out = []
    for c in range(tm // SGU_CHUNK):
        rs = slice(c * SGU_CHUNK, (c + 1) * SGU_CHUNK)
        tiles = []
        for j in range(n_tiles):
            cs = slice(j * LANES, (j + 1) * LANES)
            m = mixes[c, j]
            mixed = jnp.where(first, m[:SGU_CHUNK], m[SGU_CHUNK:]) + bs_ref[:, cs]
            tiles.append(u[rs, cs] * mixed)
        so = jnp.concatenate(tiles, axis=1)
        ms = jnp.mean(so * so, axis=-1, keepdims=True)
        out.append((so * lax.rsqrt(ms + LN_EPS) * g_ref[...]).astype(BF16))
    return jnp.concatenate(out, axis=0)


def _split_bf16(a):
    hi = a.astype(BF16)
    lo = (a - hi.astype(F32)).astype(BF16)
    return hi, lo


def _route(lg, tri):
    gl = [lg[g:g + 1, :] for g in range(MOE_GROUPS)]
    gmax = functools.reduce(jnp.maximum, gl)
    gidx = jnp.full(gmax.shape, MOE_GROUPS - 1, jnp.int32)
    for g in range(MOE_GROUPS - 2, -1, -1):
        gidx = jnp.where(gl[g] == gmax, g, gidx)
    denom = functools.reduce(lambda a, b: a + b, [jnp.exp(t - gmax) for t in gl])
    gate = 1.0 / denom

    def expert_logit(i):
        rows_ = [lg[8 + MOE_EXPERTS_PER_GROUP * g + i:9 + MOE_EXPERTS_PER_GROUP * g + i, :]
                 for g in range(MOE_GROUPS)]
        sel = rows_[MOE_GROUPS - 1]
        for g in range(MOE_GROUPS - 2, -1, -1):
            sel = jnp.where(gidx == g, rows_[g], sel)
        return sel

    el = [expert_logit(i) for i in range(MOE_EXPERTS_PER_GROUP)]
    v1 = functools.reduce(jnp.maximum, el)
    i1 = jnp.full(v1.shape, MOE_EXPERTS_PER_GROUP - 1, jnp.int32)
    for i in range(MOE_EXPERTS_PER_GROUP - 2, -1, -1):
        i1 = jnp.where(el[i] == v1, i, i1)
    rest = [jnp.where(i1 == i, -jnp.inf, el[i]) for i in range(MOE_EXPERTS_PER_GROUP)]
    v2 = functools.reduce(jnp.maximum, rest)
    i2 = jnp.full(v2.shape, MOE_EXPERTS_PER_GROUP - 1, jnp.int32)
    for i in range(MOE_EXPERTS_PER_GROUP - 2, -1, -1):
        i2 = jnp.where((rest[i] == v2) & (i1 != i), i, i2)
    e2 = jnp.exp(v2 - v1)
    w1 = 1.0 / (1.0 + e2)
    w2 = e2 * w1
    within = [jnp.where(i1 == i, w1, 0.0) + jnp.where(i2 == i, w2, 0.0)
              for i in range(MOE_EXPERTS_PER_GROUP)]
    tm = MOE_BLOCK
    cw = [within[i] * gate for i in range(MOE_EXPERTS_PER_GROUP)]

    onehot = [jnp.where(gidx == g, 1.0, 0.0) for g in range(MOE_GROUPS)]
    oh_mat = jnp.concatenate(onehot + [jnp.zeros((16 - MOE_GROUPS, tm), F32)], axis=0)
    before = jnp.dot(oh_mat.astype(BF16), tri, preferred_element_type=F32)
    rank = functools.reduce(lambda a, b: a + b,
                            [onehot[g] * before[g:g + 1, :] for g in range(MOE_GROUPS)])
    starts, pieces = [], []
    start_tok = jnp.zeros_like(rank)
    end = jnp.zeros((1, 1), F32)
    for g in range(MOE_GROUPS):
        count = jnp.sum(onehot[g], axis=1, keepdims=True)
        start_tok = start_tok + onehot[g] * end
        starts.append(end)
        end = end + MOE_ALIGN * jnp.floor((count + (MOE_ALIGN - 1)) * (1.0 / MOE_ALIGN))
        piece = functools.reduce(lambda a, b: a + b,
                                 [jnp.where(count > p, 1.0, 0.0) for p in MOE_PIECES[:-1]])
        pieces.append(jnp.where(count > 0.0, piece, -1.0))
    dest = rank + start_tok
    rout_t = jnp.concatenate([dest] + cw + [gidx.astype(F32), jnp.zeros((2, tm), F32)], axis=0)

    c = lax.broadcasted_iota(jnp.int32, (8, LANES), 1)
    tab = jnp.zeros((8, LANES), F32)
    for g in range(MOE_GROUPS):
        tab = jnp.where(c == TAB_START + g, starts[g], tab)
        tab = jnp.where(c == TAB_PIECE + g, pieces[g], tab)
    return rout_t, tab.astype(jnp.int32)


def _out_proj_kernel(a_ref, s_ref, x_ref, w_ref, g_ref, b_ref, rw_ref, rb_ref, tri_ref,
                     x1_ref, rt_ref, r_ref, tab_ref, wb_ref):
    @pl.when(pl.program_id(0) == 0)
    def _():
        wb_ref[...] = w_ref[0].astype(BF16)

    mixes = []
    for blk in range(OUT_BLOCKS_PER_STEP):
        rs = slice(blk * MOE_BLOCK, (blk + 1) * MOE_BLOCK)
        mix = jnp.dot(a_ref[rs, :], wb_ref[:D_NA, :], preferred_element_type=F32)
        mixes.append(mix + jnp.dot(s_ref[rs, :], wb_ref[D_NA:, :], preferred_element_type=F32))
    logits = []
    for blk in range(OUT_BLOCKS_PER_STEP):
        rs = slice(blk * MOE_BLOCK, (blk + 1) * MOE_BLOCK)
        x1 = _layer_norm_rows(DEEPNORM_ALPHA * x_ref[rs, :] + mixes[blk], g_ref[...], b_ref[...])
        x1_ref[rs, :] = x1
        xh, xl = _split_bf16(x1)
        logits.append(jnp.dot(xh, rw_ref[0], preferred_element_type=F32)
                      + jnp.dot(xl, rw_ref[0], preferred_element_type=F32))
    for blk in range(OUT_BLOCKS_PER_STEP):
        rs = slice(blk * MOE_BLOCK, (blk + 1) * MOE_BLOCK)
        lg_t = logits[blk].T
        lg = lg_t[:ROUTER_ROWS] + lg_t[ROUTER_ROWS:2 * ROUTER_ROWS] + rb_ref[0, :, 0:1]
        rout_t, tab = _route(lg, tri_ref[...])
        rt_ref[:, rs] = rout_t
        r_ref[rs, :] = jnp.concatenate([rout_t, jnp.zeros((LANES - 8, MOE_BLOCK), F32)], axis=0).T
        tab_ref[blk] = tab


def _out_proj(a, s, x, w_out, layer, g1, b1, rw, rb, tri):
    n = x.shape[0]
    tok = lambda i: (i, 0)
    row = lambda i: (0, 0)
    tm = OUT_BLOCKS_PER_STEP * MOE_BLOCK
    return pl.pallas_call(
        _out_proj_kernel,
        grid=(n // tm,),
        in_specs=[pl.BlockSpec((tm, D_NA), tok),
                  pl.BlockSpec((tm, D_SGU), tok),
                  pl.BlockSpec((tm, D_MODEL), tok),
                  pl.BlockSpec((1, D_MODEL, D_MODEL), lambda i: (layer, 0, 0),
                               pipeline_mode=pl.Buffered(1)),
                  pl.BlockSpec((1, D_MODEL), row),
                  pl.BlockSpec((1, D_MODEL), row),
                  pl.BlockSpec((1, D_MODEL, LANES), lambda i: (layer, 0, 0)),
                  pl.BlockSpec((1, ROUTER_ROWS, LANES), lambda i: (layer, 0, 0)),
                  pl.BlockSpec((MOE_BLOCK, MOE_BLOCK), row)],
        out_specs=[pl.BlockSpec((tm, D_MODEL), tok),
                   pl.BlockSpec((8, tm), lambda i: (0, i)),
                   pl.BlockSpec((tm, LANES), tok),
                   pl.BlockSpec((OUT_BLOCKS_PER_STEP, 8, LANES), lambda i: (i, 0, 0))],
        out_shape=[jax.ShapeDtypeStruct((n, D_MODEL), F32),
                   jax.ShapeDtypeStruct((8, n), F32),
                   jax.ShapeDtypeStruct((n, LANES), F32),
                   jax.ShapeDtypeStruct((n // MOE_BLOCK, 8, LANES), jnp.int32)],
        scratch_shapes=[pltpu.VMEM((D_MODEL, D_MODEL), BF16)],
        compiler_params=pltpu.CompilerParams(dimension_semantics=("arbitrary",),
                                             vmem_limit_bytes=VMEM_LIMIT),
        name="out_proj_ln_router",
    )(a, s, x, w_out, g1, b1, rw, rb, tri)


def _moe_kernel(tab_ref, x_ref, rt_ref, r_ref, wg_ref, wu_ref, wd_ref, g_ref, b_ref, o_ref,
                xs_ref, ys_ref, cw_ref):
    blk = pl.program_id(0)
    x1 = x_ref[...]
    dest_row = rt_ref[0:1, :].astype(jnp.int32)
    rows_i = lax.broadcasted_iota(jnp.int32, (MOE_SORTED_ROWS, MOE_BLOCK), 0)
    p_in = jnp.where(rows_i == dest_row, 1.0, 0.0).astype(BF16)
    xs_ref[...] = jnp.dot(p_in, x1.astype(BF16), preferred_element_type=F32).astype(BF16)
    r_hi, r_lo = _split_bf16(r_ref[...])
    cw2 = jnp.dot(p_in, jnp.concatenate([r_hi, r_lo], axis=1), preferred_element_type=F32)
    cw_ref[...] = cw2[:, :LANES] + cw2[:, LANES:]
    ys_ref[...] = jnp.zeros_like(ys_ref)
    dest_col = r_ref[:, 0:1].astype(jnp.int32)
    cols_i = lax.broadcasted_iota(jnp.int32, (MOE_BLOCK, MOE_SORTED_ROWS), 1)
    p_out = jnp.where(cols_i == dest_col, 1.0, 0.0).astype(BF16)

    def run_experts(g, start, n_rows):
        off = pl.multiple_of(start, MOE_ALIGN)
        xs = xs_ref[pl.ds(off, n_rows), :]
        cw = cw_ref[pl.ds(off, n_rows), :]
        hg = jnp.dot(xs, wg_ref[g], preferred_element_type=F32)
        hu = jnp.dot(xs, wu_ref[g], preferred_element_type=F32)
        scale = jnp.concatenate([jnp.broadcast_to(cw[:, 1 + i:2 + i], (n_rows, D_EXPERT))
                                 for i in range(MOE_EXPERTS_PER_GROUP)], axis=1)
        act = hg * (1.0 / (1.0 + jnp.exp(-hg))) * hu * scale
        y = jnp.dot(act.astype(BF16), wd_ref[g], preferred_element_type=F32)
        ys_ref[pl.ds(off, n_rows), :] = y.astype(BF16)

    def group(g, carry):
        start = tab_ref[blk, TAB_START + g]
        piece = tab_ref[blk, TAB_PIECE + g]
        for k, n_rows in enumerate(MOE_PIECES):
            @pl.when(piece == k)
            def _(n_rows=n_rows):
                run_experts(g, start, n_rows)
        return carry

    lax.fori_loop(0, MOE_GROUPS, group, 0)

    half = MOE_BLOCK // 2
    ys = ys_ref[...]
    y_lo = jnp.dot(p_out[:half], ys, preferred_element_type=F32)
    y_hi = jnp.dot(p_out[half:], ys, preferred_element_type=F32)
    o_ref[:half, :] = _layer_norm_rows(DEEPNORM_ALPHA * x1[:half] + y_lo, g_ref[...], b_ref[...])
    o_ref[half:, :] = _layer_norm_rows(DEEPNORM_ALPHA * x1[half:] + y_hi, g_ref[...], b_ref[...])


def _moe(tab, x1, rout_t, rout, wg, wu, wd, g2, b2):
    n = x1.shape[0]
    tok = lambda i, t: (i, 0)
    row = lambda i, t: (0, 0)
    whole = lambda i, t: (0, 0, 0)
    resident = pl.Buffered(1)
    return pl.pallas_call(
        _moe_kernel,
        grid_spec=pltpu.PrefetchScalarGridSpec(
            num_scalar_prefetch=1,
            grid=(n // MOE_BLOCK,),
            in_specs=[pl.BlockSpec((MOE_BLOCK, D_MODEL), tok),
                      pl.BlockSpec((8, MOE_BLOCK), lambda i, t: (0, i)),
                      pl.BlockSpec((MOE_BLOCK, LANES), tok),
                      pl.BlockSpec((MOE_GROUPS, D_MODEL, D_GROUP), whole, pipeline_mode=resident),
                      pl.BlockSpec((MOE_GROUPS, D_MODEL, D_GROUP), whole, pipeline_mode=resident),
                      pl.BlockSpec((MOE_GROUPS, D_GROUP, D_MODEL), whole, pipeline_mode=resident),
                      pl.BlockSpec((1, D_MODEL), row),
                      pl.BlockSpec((1, D_MODEL), row)],
            out_specs=pl.BlockSpec((MOE_BLOCK, D_MODEL), tok),
            scratch_shapes=[pltpu.VMEM((MOE_SORTED_ROWS, D_MODEL), BF16),
                            pltpu.VMEM((MOE_SORTED_ROWS, D_MODEL), BF16),
                            pltpu.VMEM((MOE_SORTED_ROWS, LANES), F32)]),
        out_shape=jax.ShapeDtypeStruct((n, D_MODEL), F32),
        compiler_params=pltpu.CompilerParams(dimension_semantics=("parallel",),
                                             vmem_limit_bytes=VMEM_LIMIT),
        name="moe_experts_ln",
    )(tab, x1, rout_t, rout, wg, wu, wd, g2, b2)


def _router_weights(w_rg, b_rg, w_re, b_re):
    depth = w_rg.shape[0]
    pad = lambda a, n: jnp.zeros(a.shape[:-1] + (n,), F32)
    wt = jnp.concatenate([w_rg.astype(F32), pad(w_rg, 8 - MOE_GROUPS), w_re.astype(F32),
                          pad(w_re, ROUTER_ROWS - 8 - MOE_EXPERTS)], axis=-1)
    bt = jnp.concatenate([b_rg.astype(F32), pad(b_rg, 8 - MOE_GROUPS), b_re.astype(F32),
                          pad(b_re, ROUTER_ROWS - 8 - MOE_EXPERTS)], axis=-1)
    hi, lo = _split_bf16(wt)
    rw = jnp.concatenate([hi, lo, jnp.zeros((depth, D_MODEL, LANES - 2 * ROUTER_ROWS), BF16)], axis=-1)
    return rw, jnp.broadcast_to(bt[:, :, None], (depth, ROUTER_ROWS, LANES))


def kernel(x, w_in, w_out, na_rel_bias, sgu_ln_g, sgu_ln_b, sgu_w, sgu_b, mix_norm_g, ln1_g, ln1_b, router_group_w, router_group_b, router_expert_w, router_expert_b, expert_w_gate, expert_w_up, expert_w_down, ln2_g, ln2_b):
    batch, seq, d = x.shape
    n = batch * seq
    xf = x.reshape(n, d).astype(F32)
    row = lambda a: a.astype(F32).reshape(1, -1)
    tri = jnp.asarray(np.triu(np.ones((MOE_BLOCK, MOE_BLOCK), np.float32), k=1), BF16)
    tables = _na_bias_tables(na_rel_bias)
    rw, rb = _router_weights(router_group_w, router_group_b, router_expert_w, router_expert_b)
    for l in range(DEPTH):
        bs_full = jnp.repeat(sgu_b[l].astype(F32).T, SGU_GROUP_DIM, axis=1)
        q, k, v, s, wg, wu, wd = _in_proj(xf, w_in.astype(F32), expert_w_gate.astype(F32),
                                          expert_w_up.astype(F32), expert_w_down.astype(F32), l,
                                          row(sgu_ln_g[l]), row(sgu_ln_b[l]), sgu_w[l].astype(BF16),
                                          bs_full, row(mix_norm_g[l, D_NA:]))
        a = _na_attention(q, k, v, tables, l, row(mix_norm_g[l, :D_NA]), batch, seq)
        x1, rout_t, rout, tab = _out_proj(a, s, xf, w_out.astype(F32), l, row(ln1_g[l]),
                                          row(ln1_b[l]), rw, rb, tri)
        xf = _moe(tab[:, 0, :8], x1, rout_t, rout, wg, wu, wd, row(ln2_g[l]), row(ln2_b[l]))
    return xf.reshape(batch, seq, d).astype(x.dtype)
```

```python
import functools
import math

import numpy as np
import jax
import jax.numpy as jnp
from jax import lax
from jax.experimental import pallas as pl
from jax.experimental.pallas import tpu as pltpu

F32 = jnp.float32
BF16 = jnp.bfloat16

D_MODEL = 1024
DEPTH = 4
GRID_W = 64
NA_HEADS = 8
NA_HEAD_DIM = 64
NA_WIN_ROWS = 8
NA_WIN_COLS = 16
D_NA = NA_HEADS * NA_HEAD_DIM
SGU_GROUPS = 8
SGU_GROUP_DIM = 64
SGU_CHUNK = 128
D_SGU = SGU_GROUPS * SGU_GROUP_DIM
D_IN = 3 * D_NA + 2 * D_SGU
MOE_GROUPS = 4
MOE_EXPERTS_PER_GROUP = 4
MOE_EXPERTS = MOE_GROUPS * MOE_EXPERTS_PER_GROUP
D_EXPERT = 256
D_GROUP = MOE_EXPERTS_PER_GROUP * D_EXPERT
DEEPNORM_ALPHA = (2 * DEPTH) ** 0.25
LN_EPS = 1e-5

LANES = 128
HEAD_PAIRS = NA_HEADS // 2
NEG_BIG = -1e30
NA_COL_BLOCK = 16
NA_COL_BLOCKS = GRID_W // NA_COL_BLOCK
NA_HALF_ROWS = 3 * 2 * NA_COL_BLOCK
ROUTER_ROWS = 32

TM_PROJ = 512
NA_ROWS_PER_STEP = 8
NA_ROWS_IN_FLIGHT = 8
MOE_BLOCK = 512
MOE_ALIGN = 16
MOE_PIECES = (64, 128, 144, 160, 192, 256, 384, MOE_BLOCK)
MOE_OUT_PARTS = 2
MOE_SORTED_ROWS = 768
OUT_BLOCKS_PER_STEP = 2
TAB_START = 0
TAB_PIECE = MOE_GROUPS
VMEM_LIMIT = 56 * 1024 * 1024


def _gelu_tanh(x):
    c = math.sqrt(2.0 / math.pi)
    return x * (0.5 * (1.0 + jnp.tanh(c * (x + 0.044715 * (x * x * x)))))


def _layer_norm_rows(h, g, b):
    mu = jnp.mean(h, axis=-1, keepdims=True)
    hc = h - mu
    var = jnp.mean(hc * hc, axis=-1, keepdims=True)
    return hc * lax.rsqrt(var + LN_EPS) * g + b


def _in_proj_kernel(x_ref, w_ref, wg_ref, wu_ref, wd_ref, lng_ref, lnb_ref, ws_ref, bs_ref, gs_ref,
                    seg_ref, q_ref, k_ref, v_ref, s_ref, wg_out, wu_out, wd_out, wb_ref):
    @pl.when(pl.program_id(0) == 0)
    def _():
        wb_ref[...] = w_ref[0].astype(BF16)

    wg_out[0] = wg_ref[0].astype(BF16)
    wu_out[0] = wu_ref[0].astype(BF16)
    wd_out[...] = wd_ref[0].astype(BF16)

    xb = x_ref[...].astype(BF16)

    def mm(j):
        return jnp.dot(xb, wb_ref[:, j * D_NA:(j + 1) * D_NA], preferred_element_type=F32)

    hv, hu = mm(4), mm(3)

    def do_q():
        q_ref[...] = (mm(0) * (NA_HEAD_DIM ** -0.5)).astype(BF16)

    def store_planes(ref, h):
        hb = h.astype(BF16)
        for kb in range(NA_COL_BLOCKS):
            ref[kb] = jnp.concatenate(
                [hb[r * GRID_W + kb * NA_COL_BLOCK:r * GRID_W + (kb + 1) * NA_COL_BLOCK]
                 for r in range(TM_PROJ // GRID_W)], axis=0)

    def do_k():
        store_planes(k_ref, mm(1))

    def do_v():
        store_planes(v_ref, mm(2))

    s_ref[...] = _spatial_gating_block(_gelu_tanh(hu), _gelu_tanh(hv), lng_ref, lnb_ref, ws_ref,
                                       bs_ref, gs_ref, seg_ref, (do_q, do_k, do_v))


def _in_proj(x, w_in, w_gate, w_up, w_down, layer, ln_g, ln_b, w_s_bf16, bs_full, gain):
    n = x.shape[0]
    steps = n // TM_PROJ
    gu_rows = MOE_EXPERTS * D_MODEL // steps
    parts = D_MODEL // gu_rows
    dn_rows = MOE_EXPERTS * D_EXPERT // steps
    assert gu_rows * steps == MOE_EXPERTS * D_MODEL and parts * gu_rows == D_MODEL
    assert dn_rows * steps == MOE_EXPERTS * D_EXPERT and TM_PROJ % SGU_CHUNK == 0
    same_group = np.arange(LANES)[:, None] // SGU_GROUP_DIM == np.arange(LANES)[None, :] // SGU_GROUP_DIM
    seg_ones = jnp.asarray(same_group, BF16)
    tok = lambda i: (i, 0)
    row = lambda i: (0, 0)
    slab = lambda i: (layer, i, 0)
    planes = pl.BlockSpec((NA_COL_BLOCKS, TM_PROJ // NA_COL_BLOCKS, D_NA), lambda i: (0, i, 0))
    grouped =lambda i: (i // (parts * MOE_EXPERTS_PER_GROUP), i % parts,
                         (i // parts) % MOE_EXPERTS_PER_GROUP)
    depth = w_gate.shape[0]
    outs = pl.pallas_call(
        _in_proj_kernel,
        grid=(steps,),
        in_specs=[pl.BlockSpec((TM_PROJ, D_MODEL), tok),
                  pl.BlockSpec((1, D_MODEL, D_IN), lambda i: (layer, 0, 0),
                               pipeline_mode=pl.Buffered(1)),
                  pl.BlockSpec((1, gu_rows, D_EXPERT), slab),
                  pl.BlockSpec((1, gu_rows, D_EXPERT), slab),
                  pl.BlockSpec((1, dn_rows, D_MODEL), slab),
                  pl.BlockSpec((1, D_SGU), row),
                  pl.BlockSpec((1, D_SGU), row),
                  pl.BlockSpec((SGU_GROUPS, SGU_CHUNK, SGU_CHUNK), lambda i: (0, 0, 0)),
                  pl.BlockSpec((SGU_CHUNK, D_SGU), row),
                  pl.BlockSpec((1, D_SGU), row),
                  pl.BlockSpec((LANES, LANES), row)],
        out_specs=[pl.BlockSpec((TM_PROJ, D_NA), tok), planes, planes,
                   pl.BlockSpec((TM_PROJ, D_NA), tok)]
                  + [pl.BlockSpec((1, gu_rows, D_EXPERT), grouped)] * 2
                  + [pl.BlockSpec((dn_rows, D_MODEL), tok)],
        out_shape=[jax.ShapeDtypeStruct((n, D_NA), BF16)]
                  + [jax.ShapeDtypeStruct((NA_COL_BLOCKS, n // NA_COL_BLOCKS, D_NA), BF16)] * 2
                  + [jax.ShapeDtypeStruct((n, D_NA), BF16)]
                  + [jax.ShapeDtypeStruct((MOE_GROUPS, D_MODEL, D_GROUP), BF16)] * 2
                  + [jax.ShapeDtypeStruct((MOE_GROUPS * D_GROUP, D_MODEL), BF16)],
        scratch_shapes=[pltpu.VMEM((D_MODEL, D_IN), BF16)],
        compiler_params=pltpu.CompilerParams(dimension_semantics=("arbitrary",),
                                             vmem_limit_bytes=VMEM_LIMIT),
        name="in_proj_sgu",
    )(x, w_in,
      w_gate.reshape(depth, MOE_EXPERTS * D_MODEL, D_EXPERT),
      w_up.reshape(depth, MOE_EXPERTS * D_MODEL, D_EXPERT),
      w_down.reshape(depth, MOE_EXPERTS * D_EXPERT, D_MODEL),
      ln_g, ln_b, w_s_bf16, bs_full, gain, seg_ones)
    q, k, v, s, wg, wu, wd = outs
    return q, k, v, s, wg, wu, wd.reshape(MOE_GROUPS, D_GROUP, D_MODEL)


def _na_bias_tables(rel_bias):
    cols = np.arange(GRID_W)
    col_start = np.clip(cols - NA_WIN_COLS // 2, 0, GRID_W - NA_WIN_COLS)
    kc = np.arange(GRID_W)
    valid = (kc[None, :] >= col_start[:, None]) & (kc[None, :] < col_start[:, None] + NA_WIN_COLS)
    dc = kc[None, :] - cols[:, None] + NA_WIN_COLS - 1
    onehot = (dc[None] == np.arange(2 * NA_WIN_COLS - 1)[:, None, None]) & valid[None]
    depth = rel_bias.shape[0]
    rb = rel_bias.astype(F32).reshape(depth, HEAD_PAIRS, 2, 2 * NA_WIN_ROWS - 1, 2 * NA_WIN_COLS - 1)
    blocked = (NA_COL_BLOCKS, NA_COL_BLOCK, NA_COL_BLOCKS, NA_COL_BLOCK)
    oh = jnp.asarray(onehot.reshape((2 * NA_WIN_COLS - 1,) + blocked), F32)
    ok = valid.reshape(blocked).transpose(2, 0, 1, 3)[:, :, None, :, None, :]
    tabs = []
    for a0 in range(NA_WIN_ROWS):
        tabs.append(jnp.einsum('lpsid,djybx->lpbjsyix', rb[:, :, :, a0:a0 + NA_WIN_ROWS], oh,
                               precision=lax.Precision.HIGHEST))
    tabs = jnp.stack(tabs, axis=1)
    tabs = jnp.where(ok, tabs, NEG_BIG)
    return tabs.reshape(depth, NA_WIN_ROWS, HEAD_PAIRS, NA_COL_BLOCKS, 2 * GRID_W, LANES)


def _na_kernel(q_ref, k_ref, v_ref, t_ref, g_ref, o_ref, s_all, p_all, a_ref):
    rows = k_ref.shape[1] // NA_COL_BLOCK
    blk = pl.program_id(1)
    lane = lax.broadcasted_iota(jnp.int32, (NA_COL_BLOCK, LANES), 1)
    first_head = lane < NA_HEAD_DIM
    nt = (((1,), (1,)), ((), ()))
    strip = 2 * NA_COL_BLOCK

    def row_geometry(rr):
        r = blk * NA_ROWS_PER_STEP + rr
        rs = jnp.clip(r - NA_WIN_ROWS // 2, 0, rows - NA_WIN_ROWS)
        a0 = rs - r + (NA_WIN_ROWS - 1)
        q_off = pl.multiple_of(rr * GRID_W, GRID_W)
        k_off = pl.multiple_of(rs * NA_COL_BLOCK, NA_COL_BLOCK)
        return a0, q_off, k_off

    def band_pair(ref, half, k_off, cs):
        return jnp.concatenate([ref[2 * half, pl.ds(k_off, LANES), cs],
                                ref[2 * half + 1, pl.ds(k_off, LANES), cs]], axis=0)

    def row_scores(rr, slot):
        _, q_off, k_off = row_geometry(rr)
        for hp in range(HEAD_PAIRS):
            cs = slice(hp * LANES, (hp + 1) * LANES)
            qp = q_ref[pl.ds(q_off, GRID_W), cs]
            pieces = []
            for j in range(NA_COL_BLOCKS):
                qb = qp[j * NA_COL_BLOCK:(j + 1) * NA_COL_BLOCK]
                zero = jnp.zeros_like(qb)
                pieces += [jnp.where(first_head, qb, zero), jnp.where(first_head, zero, qb)]
            q2 = jnp.concatenate(pieces, axis=0)
            for half in range(2):
                q_rows = q2[half * strip:half * strip + NA_HALF_ROWS]
                s_all[slot, hp, half] = lax.dot_general(q_rows, band_pair(k_ref, half, k_off, cs), nt,
                                                        preferred_element_type=F32)

    def row_softmax_pv(rr, slot):
        a0, q_off, k_off = row_geometry(rr)
        for hp in range(HEAD_PAIRS):
            cs = slice(hp * LANES, (hp + 1) * LANES)
            zeros = jnp.zeros((strip, LANES), BF16)
            p_all[slot, hp, 0, 2 * strip:3 * strip, 0:LANES] = zeros
            p_all[slot, hp, 1, 0:strip, LANES:2 * LANES] = zeros
            inv_l = []
            for j in range(NA_COL_BLOCKS):
                for sub in range(2):
                    q_row = j * strip + sub * NA_COL_BLOCK
                    tiles = []
                    for kb in range(max(0, j - 1), min(NA_COL_BLOCKS - 1, j + 1) + 1):
                        half, lo = kb // 2, (kb % 2) * LANES
                        r0 = q_row - half * strip
                        sc = (s_all[slot, hp, half, r0:r0 + NA_COL_BLOCK, lo:lo + LANES]
                              + t_ref[0, a0, hp, kb, q_row:q_row + NA_COL_BLOCK, :])
                        tiles.append((half, r0, lo, sc))
                    m = functools.reduce(jnp.maximum, [t[3] for t in tiles])
                    m = jnp.max(m, axis=1, keepdims=True)
                    total = None
                    for half, r0, lo, sc in tiles:
                        p = jnp.exp(sc - m)
                        total = p if total is None else total + p
                        p_all[slot, hp, half, r0:r0 + NA_COL_BLOCK, lo:lo + LANES] = p.astype(BF16)
                    inv_l.append(1.0 / jnp.sum(total, axis=1, keepdims=True))
            o_lo = jnp.dot(p_all[slot, hp, 0], band_pair(v_ref, 0, k_off, cs),
                           preferred_element_type=F32)
            o_hi = jnp.dot(p_all[slot, hp, 1], band_pair(v_ref, 1, k_off, cs),
                           preferred_element_type=F32)
            o = jnp.concatenate([o_lo[:strip], o_lo[strip:] + o_hi[:2 * strip], o_hi[2 * strip:]],
                                axis=0)
            blocks = []
            for j in range(NA_COL_BLOCKS):
                o0 = o[j * strip:j * strip + NA_COL_BLOCK] * inv_l[2 * j]
                o1 = o[j * strip + NA_COL_BLOCK:(j + 1) * strip] * inv_l[2 * j + 1]
                blocks.append(jnp.where(first_head, o0, o1))
            a_ref[pl.ds(q_off, GRID_W), cs] = jnp.concatenate(blocks, axis=0)

    def row_group(i, carry):
        for slot in range(NA_ROWS_IN_FLIGHT):
            row_scores(i * NA_ROWS_IN_FLIGHT + slot, slot)
            row_softmax_pv(i * NA_ROWS_IN_FLIGHT + slot, slot)
        return carry

    lax.fori_loop(0, NA_ROWS_PER_STEP // NA_ROWS_IN_FLIGHT, row_group, 0)

    a = a_ref[...]
    ms = jnp.mean(a * a, axis=-1, keepdims=True)
    o_ref[...] = (a * lax.rsqrt(ms + LN_EPS) * g_ref[...]).astype(o_ref.dtype)


def _na_attention(q, k, v, tables, layer, gain, batch, seq):
    n = q.shape[0]
    steps = seq // (GRID_W * NA_ROWS_PER_STEP)
    tq = NA_ROWS_PER_STEP * GRID_W
    return pl.pallas_call(
        _na_kernel,
        grid=(batch, steps),
        in_specs=[pl.BlockSpec((tq, D_NA), lambda b, i: (b * steps + i, 0)),
                  pl.BlockSpec((NA_COL_BLOCKS, seq // NA_COL_BLOCKS, D_NA), lambda b, i: (0, b, 0)),
                  pl.BlockSpec((NA_COL_BLOCKS, seq // NA_COL_BLOCKS, D_NA), lambda b, i: (0, b, 0)),
                  pl.BlockSpec((1,) + tables.shape[1:], lambda b, i: (layer, 0, 0, 0, 0, 0),
                               pipeline_mode=pl.Buffered(1)),
                  pl.BlockSpec((1, D_NA), lambda b, i: (0, 0))],
        out_specs=pl.BlockSpec((tq, D_NA), lambda b, i: (b * steps + i, 0)),
        out_shape=jax.ShapeDtypeStruct((n, D_NA), BF16),
        scratch_shapes=[pltpu.VMEM((NA_ROWS_IN_FLIGHT, HEAD_PAIRS, 2, NA_HALF_ROWS, 2 * LANES), F32),
                        pltpu.VMEM((NA_ROWS_IN_FLIGHT, HEAD_PAIRS, 2, NA_HALF_ROWS, 2 * LANES), BF16),
                        pltpu.VMEM((tq, D_NA), F32)],
        compiler_params=pltpu.CompilerParams(dimension_semantics=("parallel", "parallel"),
                                             vmem_limit_bytes=VMEM_LIMIT),
        name="na_attention",
    )(q, k, v, tables, gain)


def _spatial_gating_block(u, vs, lng_ref, lnb_ref, ws_ref, bs_ref, g_ref, seg_ref, between):
    tm = u.shape[0]
    lane = lax.broadcasted_iota(jnp.int32, (SGU_CHUNK, LANES), 1)
    first = lane < SGU_GROUP_DIM
    inv = 1.0 / SGU_GROUP_DIM
    n_tiles = D_SGU // LANES
    n_rows = tm * n_tiles

    def seg_mean(t):
        hi, lo = _split_bf16(t)
        s = jnp.dot(jnp.concatenate([hi, lo], axis=0), seg_ref[...], preferred_element_type=F32)
        return (s[:n_rows] + s[n_rows:]) * inv

    x = jnp.concatenate([vs[:, j * LANES:(j + 1) * LANES] for j in range(n_tiles)], axis=0)
    mean = seg_mean(x)
    between[0]()
    xc = x - mean
    var = seg_mean(xc * xc)
    between[1]()
    xn = xc * lax.rsqrt(var + LN_EPS)

    mixes = {}
    for c in range(tm // SGU_CHUNK):
        for j in range(n_tiles):
            cs = slice(j * LANES, (j + 1) * LANES)
            r0 = j * tm + c * SGU_CHUNK
            y = (xn[r0:r0 + SGU_CHUNK] * lng_ref[:, cs] + lnb_ref[:, cs]).astype(BF16)
            mixes[c, j] = jnp.dot(jnp.concatenate([ws_ref[2 * j], ws_ref[2 * j + 1]], axis=0), y,
                                  preferred_element_type=F32)
    between[2]()

    out = []
    for c in range(tm // SGU_CHUNK):
        rs = slice(c * SGU_CHUNK, (c + 1) * SGU_CHUNK)
        tiles = []
        for j in range(n_tiles):
            cs = slice(j * LANES, (j + 1) * LANES)
            m = mixes[c, j]
            mixed = jnp.where(first, m[:SGU_CHUNK], m[SGU_CHUNK:]) + bs_ref[:, cs]
            tiles.append(u[rs, cs] * mixed)
        so = jnp.concatenate(tiles, axis=1)
        ms = jnp.mean(so * so, axis=-1, keepdims=True)
        out.append((so * lax.rsqrt(ms + LN_EPS) * g_ref[...]).astype(BF16))
    return jnp.concatenate(out, axis=0)


def _split_bf16(a):
    hi = a.astype(BF16)
    lo = (a - hi.astype(F32)).astype(BF16)
    return hi, lo


def _route(lg, tri):
    gl = [lg[g:g + 1, :] for g in range(MOE_GROUPS)]
    gmax = functools.reduce(jnp.maximum, gl)
    gidx = jnp.full(gmax.shape, MOE_GROUPS - 1, jnp.int32)
    for g in range(MOE_GROUPS - 2, -1, -1):
        gidx = jnp.where(gl[g] == gmax, g, gidx)
    denom = functools.reduce(lambda a, b: a + b, [jnp.exp(t - gmax) for t in gl])
    gate = 1.0 / denom

    def expert_logit(i):
        rows_ = [lg[8 + MOE_EXPERTS_PER_GROUP * g + i:9 + MOE_EXPERTS_PER_GROUP * g + i, :]
                 for g in range(MOE_GROUPS)]
        sel = rows_[MOE_GROUPS - 1]
        for g in range(MOE_GROUPS - 2, -1, -1):
            sel = jnp.where(gidx == g, rows_[g], sel)
        return sel

    el = [expert_logit(i) for i in range(MOE_EXPERTS_PER_GROUP)]
    v1 = functools.reduce(jnp.maximum, el)
    i1 = jnp.full(v1.shape, MOE_EXPERTS_PER_GROUP - 1, jnp.int32)
    for i in range(MOE_EXPERTS_PER_GROUP - 2, -1, -1):
        i1 = jnp.where(el[i] == v1, i, i1)
    rest = [jnp.where(i1 == i, -jnp.inf, el[i]) for i in range(MOE_EXPERTS_PER_GROUP)]
    v2 = functools.reduce(jnp.maximum, rest)
    i2 = jnp.full(v2.shape, MOE_EXPERTS_PER_GROUP - 1, jnp.int32)
    for i in range(MOE_EXPERTS_PER_GROUP - 2, -1, -1):
        i2 = jnp.where((rest[i] == v2) & (i1 != i), i, i2)
    e2 = jnp.exp(v2 - v1)
    w1 = 1.0 / (1.0 + e2)
    w2 = e2 * w1
    within = [jnp.where(i1 == i, w1, 0.0) + jnp.where(i2 == i, w2, 0.0)
              for i in range(MOE_EXPERTS_PER_GROUP)]
    tm = MOE_BLOCK
    cw = [within[i] * gate for i in range(MOE_EXPERTS_PER_GROUP)]

    onehot = [jnp.where(gidx == g, 1.0, 0.0) for g in range(MOE_GROUPS)]
    oh_mat = jnp.concatenate(onehot + [jnp.zeros((16 - MOE_GROUPS, tm), F32)], axis=0)
    before = jnp.dot(oh_mat.astype(BF16), tri, preferred_element_type=F32)
    rank = functools.reduce(lambda a, b: a + b,
                            [onehot[g] * before[g:g + 1, :] for g in range(MOE_GROUPS)])
    starts, pieces = [], []
    start_tok = jnp.zeros_like(rank)
    end = jnp.zeros((1, 1), F32)
    for g in range(MOE_GROUPS):
        count = jnp.sum(onehot[g], axis=1, keepdims=True)
        start_tok = start_tok + onehot[g] * end
        starts.append(end)
        end = end + MOE_ALIGN * jnp.floor((count + (MOE_ALIGN - 1)) * (1.0 / MOE_ALIGN))
        piece = functools.reduce(lambda a, b: a + b,
                                 [jnp.where(count > p, 1.0, 0.0) for p in MOE_PIECES[:-1]])
        pieces.append(jnp.where(count > 0.0, piece, -1.0))
    dest = rank + start_tok
    rout_t = jnp.concatenate([dest] + cw + [gidx.astype(F32), jnp.zeros((2, tm), F32)], axis=0)

    c = lax.broadcasted_iota(jnp.int32, (8, LANES), 1)
    tab = jnp.zeros((8, LANES), F32)
    for g in range(MOE_GROUPS):
        tab = jnp.where(c == TAB_START + g, starts[g], tab)
        tab = jnp.where(c == TAB_PIECE + g, pieces[g], tab)
    return rout_t, tab.astype(jnp.int32)


def _out_proj_kernel(a_ref, s_ref, x_ref, w_ref, g_ref, b_ref, rw_ref, rb_ref, tri_ref,
                     x1_ref, rt_ref, r_ref, tab_ref, wb_ref):
    @pl.when(pl.program_id(0) == 0)
    def _():
        wb_ref[...] = w_ref[0].astype(BF16)

    mixes = []
    for blk in range(OUT_BLOCKS_PER_STEP):
        rs = slice(blk * MOE_BLOCK, (blk + 1) * MOE_BLOCK)
        mix = jnp.dot(a_ref[rs, :], wb_ref[:D_NA, :], preferred_element_type=F32)
        mixes.append(mix + jnp.dot(s_ref[rs, :], wb_ref[D_NA:, :], preferred_element_type=F32))
    logits = []
    for blk in range(OUT_BLOCKS_PER_STEP):
        rs = slice(blk * MOE_BLOCK, (blk + 1) * MOE_BLOCK)
        x1 = _layer_norm_rows(DEEPNORM_ALPHA * x_ref[rs, :] + mixes[blk], g_ref[...], b_ref[...])
        x1_ref[rs, :] = x1
        xh, xl = _split_bf16(x1)
        logits.append(jnp.dot(xh, rw_ref[0], preferred_element_type=F32)
                      + jnp.dot(xl, rw_ref[0], preferred_element_type=F32))
    for blk in range(OUT_BLOCKS_PER_STEP):
        rs = slice(blk * MOE_BLOCK, (blk + 1) * MOE_BLOCK)
        lg_t = logits[blk].T
        lg = lg_t[:ROUTER_ROWS] + lg_t[ROUTER_ROWS:2 * ROUTER_ROWS] + rb_ref[0, :, 0:1]
        rout_t, tab = _route(lg, tri_ref[...])
        rt_ref[:, rs] = rout_t
        r_ref[rs, :] = jnp.concatenate([rout_t, jnp.zeros((LANES - 8, MOE_BLOCK), F32)], axis=0).T
        tab_ref[blk] = tab


def _out_proj(a, s, x, w_out, layer, g1, b1, rw, rb, tri):
    n = x.shape[0]
    tok = lambda i: (i, 0)
    row = lambda i: (0, 0)
    tm = OUT_BLOCKS_PER_STEP * MOE_BLOCK
    return pl.pallas_call(
        _out_proj_kernel,
        grid=(n // tm,),
        in_specs=[pl.BlockSpec((tm, D_NA), tok),
                  pl.BlockSpec((tm, D_SGU), tok),
                  pl.BlockSpec((tm, D_MODEL), tok),
                  pl.BlockSpec((1, D_MODEL, D_MODEL), lambda i: (layer, 0, 0),
                               pipeline_mode=pl.Buffered(1)),
                  pl.BlockSpec((1, D_MODEL), row),
                  pl.BlockSpec((1, D_MODEL), row),
                  pl.BlockSpec((1, D_MODEL, LANES), lambda i: (layer, 0, 0)),
                  pl.BlockSpec((1, ROUTER_ROWS, LANES), lambda i: (layer, 0, 0)),
                  pl.BlockSpec((MOE_BLOCK, MOE_BLOCK), row)],
        out_specs=[pl.BlockSpec((tm, D_MODEL), tok),
                   pl.BlockSpec((8, tm), lambda i: (0, i)),
                   pl.BlockSpec((tm, LANES), tok),
                   pl.BlockSpec((OUT_BLOCKS_PER_STEP, 8, LANES), lambda i: (i, 0, 0))],
        out_shape=[jax.ShapeDtypeStruct((n, D_MODEL), F32),
                   jax.ShapeDtypeStruct((8, n), F32),
                   jax.ShapeDtypeStruct((n, LANES), F32),
                   jax.ShapeDtypeStruct((n // MOE_BLOCK, 8, LANES), jnp.int32)],
        scratch_shapes=[pltpu.VMEM((D_MODEL, D_MODEL), BF16)],
        compiler_params=pltpu.CompilerParams(dimension_semantics=("arbitrary",),
                                             vmem_limit_bytes=VMEM_LIMIT),
        name="out_proj_ln_router",
    )(a, s, x, w_out, g1, b1, rw, rb, tri)


def _moe_kernel(tab_ref, x_ref, rt_ref, r_ref, wg_ref, wu_ref, wd_ref, g_ref, b_ref, o_ref,
                xs_ref, ys_ref, cw_ref):
    blk = pl.program_id(0)
    x1 = x_ref[...]
    dest_row = rt_ref[0:1, :].astype(jnp.int32)
    rows_i = lax.broadcasted_iota(jnp.int32, (MOE_SORTED_ROWS, MOE_BLOCK), 0)
    p_in = jnp.where(rows_i == dest_row, 1.0, 0.0).astype(BF16)
    xs_ref[...] = jnp.dot(p_in, x1.astype(BF16), preferred_element_type=F32).astype(BF16)
    r_hi, r_lo = _split_bf16(r_ref[...])
    cw2 = jnp.dot(p_in, jnp.concatenate([r_hi, r_lo], axis=1), preferred_element_type=F32)
    cw_ref[...] = cw2[:, :LANES] + cw2[:, LANES:]
    ys_ref[...] = jnp.zeros_like(ys_ref)
    dest_col = r_ref[:, 0:1].astype(jnp.int32)
    cols_i = lax.broadcasted_iota(jnp.int32, (MOE_BLOCK, MOE_SORTED_ROWS), 1)
    p_out = jnp.where(cols_i == dest_col, 1.0, 0.0).astype(BF16)

    def run_experts(g, start, n_rows):
        off = pl.multiple_of(start, MOE_ALIGN)
        xs = xs_ref[pl.ds(off, n_rows), :]
        cw = cw_ref[pl.ds(off, n_rows), :]
        hg = jnp.dot(xs, wg_ref[g], preferred_element_type=F32)
        hu = jnp.dot(xs, wu_ref[g], preferred_element_type=F32)
        scale = jnp.concatenate([jnp.broadcast_to(cw[:, 1 + i:2 + i], (n_rows, D_EXPERT))
                                 for i in range(MOE_EXPERTS_PER_GROUP)], axis=1)
        act = hg * (1.0 / (1.0 + jnp.exp(-hg))) * hu * scale
        y = jnp.dot(act.astype(BF16), wd_ref[g], preferred_element_type=F32)
        ys_ref[pl.ds(off, n_rows), :] = y.astype(BF16)

    def group(g, carry):
        start = tab_ref[blk, TAB_START + g]
        piece = tab_ref[blk, TAB_PIECE + g]
        for k, n_rows in enumerate(MOE_PIECES):
            @pl.when(piece == k)
            def _(n_rows=n_rows):
                run_experts(g, start, n_rows)
        return carry

    lax.fori_loop(0, MOE_GROUPS, group, 0)

    part = MOE_BLOCK // MOE_OUT_PARTS
    ys = ys_ref[...]
    y_parts = [jnp.dot(p_out[i * part:(i + 1) * part], ys, preferred_element_type=F32)
               for i in range(MOE_OUT_PARTS)]
    for i in range(MOE_OUT_PARTS):
        rs = slice(i * part, (i + 1) * part)
        o_ref[rs, :] = _layer_norm_rows(DEEPNORM_ALPHA * x1[rs] + y_parts[i], g_ref[...], b_ref[...])


def _moe(tab, x1, rout_t, rout, wg, wu, wd, g2, b2):
    n = x1.shape[0]
    tok = lambda i, t: (i, 0)
    row = lambda i, t: (0, 0)
    whole = lambda i, t: (0, 0, 0)
    resident = pl.Buffered(1)
    return pl.pallas_call(
        _moe_kernel,
        grid_spec=pltpu.PrefetchScalarGridSpec(
            num_scalar_prefetch=1,
            grid=(n // MOE_BLOCK,),
            in_specs=[pl.BlockSpec((MOE_BLOCK, D_MODEL), tok),
                      pl.BlockSpec((8, MOE_BLOCK), lambda i, t: (0, i)),
                      pl.BlockSpec((MOE_BLOCK, LANES), tok),
                      pl.BlockSpec((MOE_GROUPS, D_MODEL, D_GROUP), whole, pipeline_mode=resident),
                      pl.BlockSpec((MOE_GROUPS, D_MODEL, D_GROUP), whole, pipeline_mode=resident),
                      pl.BlockSpec((MOE_GROUPS, D_GROUP, D_MODEL), whole, pipeline_mode=resident),
                      pl.BlockSpec((1, D_MODEL), row),
                      pl.BlockSpec((1, D_MODEL), row)],
            out_specs=pl.BlockSpec((MOE_BLOCK, D_MODEL), tok),
            scratch_shapes=[pltpu.VMEM((MOE_SORTED_ROWS, D_MODEL), BF16),
                            pltpu.VMEM((MOE_SORTED_ROWS, D_MODEL), BF16),
                            pltpu.VMEM((MOE_SORTED_ROWS, LANES), F32)]),
        out_shape=jax.ShapeDtypeStruct((n, D_MODEL), F32),
        compiler_params=pltpu.CompilerParams(dimension_semantics=("parallel",),
                                             vmem_limit_bytes=VMEM_LIMIT),
        name="moe_experts_ln",
    )(tab, x1, rout_t, rout, wg, wu, wd, g2, b2)


def _router_weights(w_rg, b_rg, w_re, b_re):
    depth = w_rg.shape[0]
    pad = lambda a, n: jnp.zeros(a.shape[:-1] + (n,), F32)
    wt = jnp.concatenate([w_rg.astype(F32), pad(w_rg, 8 - MOE_GROUPS), w_re.astype(F32),
                          pad(w_re, ROUTER_ROWS - 8 - MOE_EXPERTS)], axis=-1)
    bt = jnp.concatenate([b_rg.astype(F32), pad(b_rg, 8 - MOE_GROUPS), b_re.astype(F32),
                          pad(b_re, ROUTER_ROWS - 8 - MOE_EXPERTS)], axis=-1)
    hi, lo = _split_bf16(wt)
    rw = jnp.concatenate([hi, lo, jnp.zeros((depth, D_MODEL, LANES - 2 * ROUTER_ROWS), BF16)], axis=-1)
    return rw, jnp.broadcast_to(bt[:, :, None], (depth, ROUTER_ROWS, LANES))


def kernel(x, w_in, w_out, na_rel_bias, sgu_ln_g, sgu_ln_b, sgu_w, sgu_b, mix_norm_g, ln1_g, ln1_b, router_group_w, router_group_b, router_expert_w, router_expert_b, expert_w_gate, expert_w_up, expert_w_down, ln2_g, ln2_b):
    batch, seq, d = x.shape
    n = batch * seq
    xf = x.reshape(n, d).astype(F32)
    row = lambda a: a.astype(F32).reshape(1, -1)
    tri = jnp.asarray(np.triu(np.ones((MOE_BLOCK, MOE_BLOCK), np.float32), k=1), BF16)
    tables = _na_bias_tables(na_rel_bias)
    rw, rb = _router_weights(router_group_w, router_group_b, router_expert_w, router_expert_b)
    for l in range(DEPTH):
        bs_full = jnp.repeat(sgu_b[l].astype(F32).T, SGU_GROUP_DIM, axis=1)
        q, k, v, s, wg, wu, wd = _in_proj(xf, w_in.astype(F32), expert_w_gate.astype(F32),
                                          expert_w_up.astype(F32), expert_w_down.astype(F32), l,
                                          row(sgu_ln_g[l]), row(sgu_ln_b[l]), sgu_w[l].astype(BF16),
                                          bs_full, row(mix_norm_g[l, D_NA:]))
        a = _na_attention(q, k, v, tables, l, row(mix_norm_g[l, :D_NA]), batch, seq)
        x1, rout_t, rout, tab = _out_proj(a, s, xf, w_out.astype(F32), l, row(ln1_g[l]),
                                          row(ln1_b[l]), rw, rb, tri)
        xf = _moe(tab[:, 0, :8], x1, rout_t, rout, wg, wu, wd, row(ln2_g[l]), row(ln2_b[l]))
    return xf.reshape(batch, seq, d).astype(x.dtype)
```

```python
import functools
import math

import numpy as np
import jax
import jax.numpy as jnp
from jax import lax
from jax.experimental import pallas as pl
from jax.experimental.pallas import tpu as pltpu

F32 = jnp.float32
BF16 = jnp.bfloat16

D_MODEL = 1024
DEPTH = 4
GRID_W = 64
NA_HEADS = 8
NA_HEAD_DIM = 64
NA_WIN_ROWS = 8
NA_WIN_COLS = 16
D_NA = NA_HEADS * NA_HEAD_DIM
SGU_GROUPS = 8
SGU_GROUP_DIM = 64
SGU_CHUNK = 128
D_SGU = SGU_GROUPS * SGU_GROUP_DIM
D_IN = 3 * D_NA + 2 * D_SGU
MOE_GROUPS = 4
MOE_EXPERTS_PER_GROUP = 4
MOE_EXPERTS = MOE_GROUPS * MOE_EXPERTS_PER_GROUP
D_EXPERT = 256
D_GROUP = MOE_EXPERTS_PER_GROUP * D_EXPERT
DEEPNORM_ALPHA = (2 * DEPTH) ** 0.25
LN_EPS = 1e-5
INV_ALPHA = 1.0 / DEEPNORM_ALPHA
LN_EPS_RESIDUAL = LN_EPS / DEEPNORM_ALPHA ** 2

LANES = 128
HEAD_PAIRS = NA_HEADS // 2
NEG_BIG = -1e30
BAND_KEYS = NA_WIN_ROWS * GRID_W
BIAS_TILES = 7
ROUTER_ROWS = 32

TM_PROJ = 512
NA_ROWS_PER_STEP = 8
NA_ROWS_IN_FLIGHT = 8
NA_STRIP = 16
MOE_BLOCK = 512
MOE_ALIGN = 16
MOE_PIECES = (64, 128, 144, 160, 192, 256, 384, MOE_BLOCK)
MOE_OUT_PARTS = 2
MOE_SORTED_ROWS = 768
OUT_BLOCKS_PER_STEP = 2
TAB_START = 0
TAB_PIECE = MOE_GROUPS
VMEM_LIMIT = 56 * 1024 * 1024


def _gelu_tanh(x):
    c = math.sqrt(2.0 / math.pi)
    return x * (0.5 * (1.0 + jnp.tanh(c * (x + 0.044715 * (x * x * x)))))


def _layer_norm_rows(h, g, b, eps):
    mu = jnp.mean(h, axis=-1, keepdims=True)
    hc = h - mu
    var = jnp.mean(hc * hc, axis=-1, keepdims=True)
    return hc * lax.rsqrt(var + eps) * g + b


def _in_proj_kernel(x_ref, w_ref, wg_ref, wu_ref, wd_ref, lng_ref, lnb_ref, ws_ref, bs_ref, gs_ref,
                    seg_ref, q_ref, k_ref, v_ref, s_ref, wg_out, wu_out, wd_out, wb_ref):
    @pl.when(pl.program_id(0) == 0)
    def _():
        wb_ref[...] = w_ref[0].astype(BF16)

    wg_out[0] = wg_ref[0].astype(BF16)
    wu_out[0] = wu_ref[0].astype(BF16)
    wd_out[...] = wd_ref[0].astype(BF16)

    xb = x_ref[...].astype(BF16)

    def mm(j):
        return jnp.dot(xb, wb_ref[:, j * D_NA:(j + 1) * D_NA], preferred_element_type=F32)

    hv, hu = mm(4), mm(3)

    def do_q():
        q_ref[...] = (mm(0) * (NA_HEAD_DIM ** -0.5)).astype(BF16)

    def do_k():
        k_ref[...] = mm(1).astype(BF16)

    def do_v():
        v_ref[...] = mm(2).astype(BF16)

    s_ref[...] = _spatial_gating_block(_gelu_tanh(hu), _gelu_tanh(hv), lng_ref, lnb_ref, ws_ref,
                                       bs_ref, gs_ref, seg_ref, (do_q, do_k, do_v))


def _in_proj(x, w_in, w_gate, w_up, w_down, layer, ln_g, ln_b, w_s_bf16, bs_full, gain):
    n = x.shape[0]
    steps = n // TM_PROJ
    gu_rows = MOE_EXPERTS * D_MODEL // steps
    parts = D_MODEL // gu_rows
    dn_rows = MOE_EXPERTS * D_EXPERT // steps
    assert gu_rows * steps == MOE_EXPERTS * D_MODEL and parts * gu_rows == D_MODEL
    assert dn_rows * steps == MOE_EXPERTS * D_EXPERT and TM_PROJ % SGU_CHUNK == 0
    same_group = np.arange(LANES)[:, None] // SGU_GROUP_DIM == np.arange(LANES)[None, :] // SGU_GROUP_DIM
    seg_ones = jnp.asarray(same_group, BF16)
    tok = lambda i: (i, 0)
    row = lambda i: (0, 0)
    slab = lambda i: (layer, i, 0)
    grouped = lambda i: (i // (parts * MOE_EXPERTS_PER_GROUP), i % parts,
                         (i // parts) % MOE_EXPERTS_PER_GROUP)
    depth = w_gate.shape[0]
    outs = pl.pallas_call(
        _in_proj_kernel,
        grid=(steps,),
        in_specs=[pl.BlockSpec((TM_PROJ, D_MODEL), tok),
                  pl.BlockSpec((1, D_MODEL, D_IN), lambda i: (layer, 0, 0),
                               pipeline_mode=pl.Buffered(1)),
                  pl.BlockSpec((1, gu_rows, D_EXPERT), slab),
                  pl.BlockSpec((1, gu_rows, D_EXPERT), slab),
                  pl.BlockSpec((1, dn_rows, D_MODEL), slab),
                  pl.BlockSpec((1, D_SGU), row),
                  pl.BlockSpec((1, D_SGU), row),
                  pl.BlockSpec((SGU_GROUPS, SGU_CHUNK, SGU_CHUNK), lambda i: (0, 0, 0)),
                  pl.BlockSpec((SGU_CHUNK, D_SGU), row),
                  pl.BlockSpec((1, D_SGU), row),
                  pl.BlockSpec((LANES, LANES), row)],
        out_specs=[pl.BlockSpec((TM_PROJ, D_NA), tok)] * 4
                  + [pl.BlockSpec((1, gu_rows, D_EXPERT), grouped)] * 2
                  + [pl.BlockSpec((dn_rows, D_MODEL), tok)],
        out_shape=[jax.ShapeDtypeStruct((n, D_NA), BF16)] * 4
                  + [jax.ShapeDtypeStruct((MOE_GROUPS, D_MODEL, D_GROUP), BF16)] * 2
                  + [jax.ShapeDtypeStruct((MOE_GROUPS * D_GROUP, D_MODEL), BF16)],
        scratch_shapes=[pltpu.VMEM((D_MODEL, D_IN), BF16)],
        compiler_params=pltpu.CompilerParams(dimension_semantics=("arbitrary",),
                                             vmem_limit_bytes=VMEM_LIMIT),
        name="in_proj_sgu",
    )(x, w_in,
      w_gate.reshape(depth, MOE_EXPERTS * D_MODEL, D_EXPERT),
      w_up.reshape(depth, MOE_EXPERTS * D_MODEL, D_EXPERT),
      w_down.reshape(depth, MOE_EXPERTS * D_EXPERT, D_MODEL),
      ln_g, ln_b, w_s_bf16, bs_full, gain, seg_ones)
    q, k, v, s, wg, wu, wd = outs
    return q, k, v, s, wg, wu, wd.reshape(MOE_GROUPS, D_GROUP, D_MODEL)


def _na_bias_tables(rel_bias):
    cols = np.arange(GRID_W)
    col_start = np.clip(cols - NA_WIN_COLS // 2, 0, GRID_W - NA_WIN_COLS)
    kc = np.arange(GRID_W)
    valid = (kc[None, :] >= col_start[:, None]) & (kc[None, :] < col_start[:, None] + NA_WIN_COLS)
    dc = kc[None, :] - cols[:, None] + NA_WIN_COLS - 1
    onehot = (dc[None] == np.arange(2 * NA_WIN_COLS - 1)[:, None, None]) & valid[None]
    depth = rel_bias.shape[0]
    rb = rel_bias.astype(F32).reshape(depth, HEAD_PAIRS, 2, 2 * NA_WIN_ROWS - 1, 2 * NA_WIN_COLS - 1)
    halves = LANES // GRID_W
    tabs = []
    for par in range(2):
        rbp = rb[:, :, :, par:par + halves * BIAS_TILES].reshape(
            depth, HEAD_PAIRS, 2, BIAS_TILES, halves, 2 * NA_WIN_COLS - 1)
        tabs.append(jnp.einsum('lpstzd,dck->lptsczk', rbp, jnp.asarray(onehot, F32),
                               precision=lax.Precision.HIGHEST))
    tabs = jnp.stack(tabs, axis=1)
    tabs = jnp.where(valid[:, None, :], tabs, NEG_BIG)
    return tabs.reshape(depth, 2, HEAD_PAIRS, BIAS_TILES, 2 * GRID_W, LANES)


def _na_kernel(q_ref, k_ref, v_ref, t_ref, g_ref, o_ref, s_all, p_all, a_ref):
    rows = k_ref.shape[0] // GRID_W
    blk = pl.program_id(1)
    lane = lax.broadcasted_iota(jnp.int32, (GRID_W, LANES), 1)
    first_head = lane < NA_HEAD_DIM
    key_tiles = BAND_KEYS // LANES

    def row_geometry(rr):
        r = blk * NA_ROWS_PER_STEP + rr
        rs = jnp.clip(r - NA_WIN_ROWS // 2, 0, rows - NA_WIN_ROWS)
        a0 = rs - r + (NA_WIN_ROWS - 1)
        q_off = pl.multiple_of(rr * GRID_W, GRID_W)
        k_off = pl.multiple_of(rs * GRID_W, GRID_W)
        return a0 % 2, a0 // 2, q_off, k_off

    def row_scores(rr, slot):
        s_ref = s_all.at[slot]
        _, _, q_off, k_off = row_geometry(rr)
        for hp in range(HEAD_PAIRS):
            cs = slice(hp * LANES, (hp + 1) * LANES)
            qp = q_ref[pl.ds(q_off, GRID_W), cs]
            kp = k_ref[pl.ds(k_off, BAND_KEYS), cs]
            zero = jnp.zeros_like(qp)
            q2 = jnp.concatenate([jnp.where(first_head, qp, zero),
                                  jnp.where(first_head, zero, qp)], axis=0)
            s_ref[hp] = lax.dot_general(q2, kp, (((1,), (1,)), ((), ())),
                                        preferred_element_type=F32)

    def row_softmax_pv(rr, slot):
        s_ref = s_all.at[slot]
        p_ref = p_all.at[slot]
        par, j0, q_off, k_off = row_geometry(rr)
        for hp in range(HEAD_PAIRS):
            cs = slice(hp * LANES, (hp + 1) * LANES)
            inv_l = []
            for ch in range(2 * GRID_W // NA_STRIP):
                rsl = slice(ch * NA_STRIP, (ch + 1) * NA_STRIP)
                bias = jnp.concatenate([t_ref[0, par, hp, j0 + t, rsl, :] for t in range(key_tiles)],
                                       axis=1)
                sc = s_ref[hp, rsl, :] + bias
                m = jnp.max(sc, axis=1, keepdims=True)
                p = jnp.exp(sc - m)
                inv_l.append(1.0 / jnp.sum(p, axis=1, keepdims=True))
                p_ref[hp, rsl, :] = p.astype(BF16)
            vp = v_ref[pl.ds(k_off, BAND_KEYS), cs]
            o = jnp.dot(p_ref[hp], vp, preferred_element_type=F32)
            o = jnp.concatenate([o[ch * NA_STRIP:(ch + 1) * NA_STRIP] * inv_l[ch]
                                 for ch in range(len(inv_l))], axis=0)
            a_ref[pl.ds(q_off, GRID_W), cs] = jnp.where(first_head, o[:GRID_W], o[GRID_W:])

    def row_group(i, carry):
        for slot in range(NA_ROWS_IN_FLIGHT):
            row_scores(i * NA_ROWS_IN_FLIGHT + slot, slot)
            row_softmax_pv(i * NA_ROWS_IN_FLIGHT + slot, slot)
        return carry

    lax.fori_loop(0, NA_ROWS_PER_STEP // NA_ROWS_IN_FLIGHT, row_group, 0)

    a = a_ref[...]
    ms = jnp.mean(a * a, axis=-1, keepdims=True)
    o_ref[...] = (a * lax.rsqrt(ms + LN_EPS) * g_ref[...]).astype(o_ref.dtype)


def _na_attention(q, k, v, tables, layer, gain, batch, seq):
    n = q.shape[0]
    steps = seq // (GRID_W * NA_ROWS_PER_STEP)
    tq = NA_ROWS_PER_STEP * GRID_W
    return pl.pallas_call(
        _na_kernel,
        grid=(batch, steps),
        in_specs=[pl.BlockSpec((tq, D_NA), lambda b, i: (b * steps + i, 0)),
                  pl.BlockSpec((seq, D_NA), lambda b, i: (b, 0)),
                  pl.BlockSpec((seq, D_NA), lambda b, i: (b, 0)),
                  pl.BlockSpec((1,) + tables.shape[1:], lambda b, i: (layer, 0, 0, 0, 0, 0)),
                  pl.BlockSpec((1, D_NA), lambda b, i: (0, 0))],
        out_specs=pl.BlockSpec((tq, D_NA), lambda b, i: (b * steps + i, 0)),
        out_shape=jax.ShapeDtypeStruct((n, D_NA), BF16),
        scratch_shapes=[pltpu.VMEM((NA_ROWS_IN_FLIGHT, HEAD_PAIRS, 2 * GRID_W, BAND_KEYS), F32),
                        pltpu.VMEM((NA_ROWS_IN_FLIGHT, HEAD_PAIRS, 2 * GRID_W, BAND_KEYS), BF16),
                        pltpu.VMEM((tq, D_NA), F32)],
        compiler_params=pltpu.CompilerParams(dimension_semantics=("parallel", "parallel"),
                                             vmem_limit_bytes=VMEM_LIMIT),
        name="na_attention",
    )(q, k, v, tables, gain)


def _spatial_gating_block(u, vs, lng_ref, lnb_ref, ws_ref, bs_ref, g_ref, seg_ref, between):
    tm = u.shape[0]
    lane = lax.broadcasted_iota(jnp.int32, (SGU_CHUNK, LANES), 1)
    first = lane < SGU_GROUP_DIM
    inv = 1.0 / SGU_GROUP_DIM
    n_tiles = D_SGU // LANES
    n_rows = tm * n_tiles

    def seg_mean(t):
        hi, lo = _split_bf16(t)
        s = jnp.dot(jnp.concatenate([hi, lo], axis=0), seg_ref[...], preferred_element_type=F32)
        return (s[:n_rows] + s[n_rows:]) * inv

    x = jnp.concatenate([vs[:, j * LANES:(j + 1) * LANES] for j in range(n_tiles)], axis=0)
    mean = seg_mean(x)
    between[0]()
    xc = x - mean
    var = seg_mean(xc * xc)
    between[1]()
    xn = xc * lax.rsqrt(var + LN_EPS)

    mixes = {}
    for c in range(tm // SGU_CHUNK):
        for j in range(n_tiles):
            cs = slice(j * LANES, (j + 1) * LANES)
            r0 = j * tm + c * SGU_CHUNK
            y = (xn[r0:r0 + SGU_CHUNK] * lng_ref[:, cs] + lnb_ref[:, cs]).astype(BF16)
            mixes[c, j] = jnp.dot(jnp.concatenate([ws_ref[2 * j], ws_ref[2 * j + 1]], axis=0), y,
                                  preferred_element_type=F32)
    between[2]()

    out = []
    for c in range(tm // SGU_CHUNK):
        rs = slice(c * SGU_CHUNK, (c + 1) * SGU_CHUNK)
        tiles = []
        for j in range(n_tiles):
            cs = slice(j * LANES, (j + 1) * LANES)
            m = mixes[c, j]
            mixed = jnp.where(first, m[:SGU_CHUNK], m[SGU_CHUNK:]) + bs_ref[:, cs]
            tiles.append(u[rs, cs] * mixed)
        so = jnp.concatenate(tiles, axis=1)
        ms = jnp.mean(so * so, axis=-1, keepdims=True)
        out.append((so * lax.rsqrt(ms + LN_EPS) * g_ref[...]).astype(BF16))
    return jnp.concatenate(out, axis=0)


def _split_bf16(a):
    hi = a.astype(BF16)
    lo = (a - hi.astype(F32)).astype(BF16)
    return hi, lo


def _route(lg, tri):
    gl = [lg[g:g + 1, :] for g in range(MOE_GROUPS)]
    gmax = functools.reduce(jnp.maximum, gl)
    gidx = jnp.full(gmax.shape, MOE_GROUPS - 1, jnp.int32)
    for g in range(MOE_GROUPS - 2, -1, -1):
        gidx = jnp.where(gl[g] == gmax, g, gidx)
    denom = functools.reduce(lambda a, b: a + b, [jnp.exp(t - gmax) for t in gl])
    gate = 1.0 / denom

    def expert_logit(i):
        rows_ = [lg[8 + MOE_EXPERTS_PER_GROUP * g + i:9 + MOE_EXPERTS_PER_GROUP * g + i, :]
                 for g in range(MOE_GROUPS)]
        sel = rows_[MOE_GROUPS - 1]
        for g in range(MOE_GROUPS - 2, -1, -1):
            sel = jnp.where(gidx == g, rows_[g], sel)
        return sel

    el = [expert_logit(i) for i in range(MOE_EXPERTS_PER_GROUP)]
    v1 = functools.reduce(jnp.maximum, el)
    i1 = jnp.full(v1.shape, MOE_EXPERTS_PER_GROUP - 1, jnp.int32)
    for i in range(MOE_EXPERTS_PER_GROUP - 2, -1, -1):
        i1 = jnp.where(el[i] == v1, i, i1)
    rest = [jnp.where(i1 == i, -jnp.inf, el[i]) for i in range(MOE_EXPERTS_PER_GROUP)]
    v2 = functools.reduce(jnp.maximum, rest)
    i2 = jnp.full(v2.shape, MOE_EXPERTS_PER_GROUP - 1, jnp.int32)
    for i in range(MOE_EXPERTS_PER_GROUP - 2, -1, -1):
        i2 = jnp.where((rest[i] == v2) & (i1 != i), i, i2)
    e2 = jnp.exp(v2 - v1)
    w1 = 1.0 / (1.0 + e2)
    w2 = e2 * w1
    within = [jnp.where(i1 == i, w1, 0.0) + jnp.where(i2 == i, w2, 0.0)
              for i in range(MOE_EXPERTS_PER_GROUP)]
    tm = MOE_BLOCK
    cw = [within[i] * gate * INV_ALPHA for i in range(MOE_EXPERTS_PER_GROUP)]

    onehot = [jnp.where(gidx == g, 1.0, 0.0) for g in range(MOE_GROUPS)]
    oh_mat = jnp.concatenate(onehot + [jnp.zeros((16 - MOE_GROUPS, tm), F32)], axis=0)
    before = jnp.dot(oh_mat.astype(BF16), tri, preferred_element_type=F32)
    rank = functools.reduce(lambda a, b: a + b,
                            [onehot[g] * before[g:g + 1, :] for g in range(MOE_GROUPS)])
    starts, pieces = [], []
    start_tok = jnp.zeros_like(rank)
    end = jnp.zeros((1, 1), F32)
    for g in range(MOE_GROUPS):
        count = jnp.sum(onehot[g], axis=1, keepdims=True)
        start_tok = start_tok + onehot[g] * end
        starts.append(end)
        end = end + MOE_ALIGN * jnp.floor((count + (MOE_ALIGN - 1)) * (1.0 / MOE_ALIGN))
        piece = functools.reduce(lambda a, b: a + b,
                                 [jnp.where(count > p, 1.0, 0.0) for p in MOE_PIECES[:-1]])
        pieces.append(jnp.where(count > 0.0, piece, -1.0))
    dest = rank + start_tok
    rout_t = jnp.concatenate([dest] + cw + [gidx.astype(F32), jnp.zeros((2, tm), F32)], axis=0)

    c = lax.broadcasted_iota(jnp.int32, (8, LANES), 1)
    tab = jnp.zeros((8, LANES), F32)
    for g in range(MOE_GROUPS):
        tab = jnp.where(c == TAB_START + g, starts[g], tab)
        tab = jnp.where(c == TAB_PIECE + g, pieces[g], tab)
    return rout_t, tab.astype(jnp.int32)


def _out_proj_kernel(a_ref, s_ref, x_ref, w_ref, g_ref, b_ref, rw_ref, rb_ref, tri_ref,
                     x1_ref, rt_ref, r_ref, tab_ref, wb_ref):
    @pl.when(pl.program_id(0) == 0)
    def _():
        wb_ref[...] = w_ref[0].astype(BF16)

    mixes = []
    for blk in range(OUT_BLOCKS_PER_STEP):
        rs = slice(blk * MOE_BLOCK, (blk + 1) * MOE_BLOCK)
        mix = jnp.dot(a_ref[rs, :], wb_ref[:D_NA, :], preferred_element_type=F32)
        mixes.append(mix + jnp.dot(s_ref[rs, :], wb_ref[D_NA:, :], preferred_element_type=F32))
    logits = []
    for blk in range(OUT_BLOCKS_PER_STEP):
        rs = slice(blk * MOE_BLOCK, (blk + 1) * MOE_BLOCK)
        x1 = _layer_norm_rows(x_ref[rs, :] + mixes[blk], g_ref[...], b_ref[...], LN_EPS_RESIDUAL)
        x1_ref[rs, :] = x1
        xh, xl = _split_bf16(x1)
        logits.append(jnp.dot(xh, rw_ref[0], preferred_element_type=F32)
                      + jnp.dot(xl, rw_ref[0], preferred_element_type=F32))
    for blk in range(OUT_BLOCKS_PER_STEP):
        rs = slice(blk * MOE_BLOCK, (blk + 1) * MOE_BLOCK)
        lg_t = logits[blk].T
        lg = lg_t[:ROUTER_ROWS] + lg_t[ROUTER_ROWS:2 * ROUTER_ROWS] + rb_ref[0, :, 0:1]
        rout_t, tab = _route(lg, tri_ref[...])
        rt_ref[:, rs] = rout_t
        r_ref[rs, :] = jnp.concatenate([rout_t, jnp.zeros((LANES - 8, MOE_BLOCK), F32)], axis=0).T
        tab_ref[blk] = tab


def _out_proj(a, s, x, w_out, layer, g1, b1, rw, rb, tri):
    n = x.shape[0]
    tok = lambda i: (i, 0)
    row = lambda i: (0, 0)
    tm = OUT_BLOCKS_PER_STEP * MOE_BLOCK
    return pl.pallas_call(
        _out_proj_kernel,
        grid=(n // tm,),
        in_specs=[pl.BlockSpec((tm, D_NA), tok),
                  pl.BlockSpec((tm, D_SGU), tok),
                  pl.BlockSpec((tm, D_MODEL), tok),
                  pl.BlockSpec((1, D_MODEL, D_MODEL), lambda i: (layer, 0, 0),
                               pipeline_mode=pl.Buffered(1)),
                  pl.BlockSpec((1, D_MODEL), row),
                  pl.BlockSpec((1, D_MODEL), row),
                  pl.BlockSpec((1, D_MODEL, LANES), lambda i: (layer, 0, 0)),
                  pl.BlockSpec((1, ROUTER_ROWS, LANES), lambda i: (layer, 0, 0)),
                  pl.BlockSpec((MOE_BLOCK, MOE_BLOCK), row)],
        out_specs=[pl.BlockSpec((tm, D_MODEL), tok),
                   pl.BlockSpec((8, tm), lambda i: (0, i)),
                   pl.BlockSpec((tm, LANES), tok),
                   pl.BlockSpec((OUT_BLOCKS_PER_STEP, 8, LANES), lambda i: (i, 0, 0))],
        out_shape=[jax.ShapeDtypeStruct((n, D_MODEL), F32),
                   jax.ShapeDtypeStruct((8, n), F32),
                   jax.ShapeDtypeStruct((n, LANES), F32),
                   jax.ShapeDtypeStruct((n // MOE_BLOCK, 8, LANES), jnp.int32)],
        scratch_shapes=[pltpu.VMEM((D_MODEL, D_MODEL), BF16)],
        compiler_params=pltpu.CompilerParams(dimension_semantics=("arbitrary",),
                                             vmem_limit_bytes=VMEM_LIMIT),
        name="out_proj_ln_router",
    )(a, s, x, w_out, g1, b1, rw, rb, tri)


def _moe_kernel(tab_ref, x_ref, rt_ref, r_ref, wg_ref, wu_ref, wd_ref, g_ref, b_ref, o_ref,
                xs_ref, ys_ref, cw_ref):
    blk = pl.program_id(0)
    x1 = x_ref[...]
    dest_row = rt_ref[0:1, :].astype(jnp.int32)
    rows_i = lax.broadcasted_iota(jnp.int32, (MOE_SORTED_ROWS, MOE_BLOCK), 0)
    p_in = jnp.where(rows_i == dest_row, 1.0, 0.0).astype(BF16)
    xs_ref[...] = jnp.dot(p_in, x1.astype(BF16), preferred_element_type=F32).astype(BF16)
    r_hi, r_lo = _split_bf16(r_ref[...])
    cw2 = jnp.dot(p_in, jnp.concatenate([r_hi, r_lo], axis=1), preferred_element_type=F32)
    cw_ref[...] = cw2[:, :LANES] + cw2[:, LANES:]
    ys_ref[...] = jnp.zeros_like(ys_ref)
    dest_col = r_ref[:, 0:1].astype(jnp.int32)
    cols_i = lax.broadcasted_iota(jnp.int32, (MOE_BLOCK, MOE_SORTED_ROWS), 1)
    p_out = jnp.where(cols_i == dest_col, 1.0, 0.0).astype(BF16)

    def run_experts(g, start, n_rows):
        off = pl.multiple_of(start, MOE_ALIGN)
        xs = xs_ref[pl.ds(off, n_rows), :]
        cw = cw_ref[pl.ds(off, n_rows), :]
        hg = jnp.dot(xs, wg_ref[g], preferred_element_type=F32)
        hu = jnp.dot(xs, wu_ref[g], preferred_element_type=F32)
        scale = jnp.concatenate([jnp.broadcast_to(cw[:, 1 + i:2 + i], (n_rows, D_EXPERT))
                                 for i in range(MOE_EXPERTS_PER_GROUP)], axis=1)
        act = hg * (1.0 / (1.0 + jnp.exp(-hg))) * hu * scale
        y = jnp.dot(act.astype(BF16), wd_ref[g], preferred_element_type=F32)
        ys_ref[pl.ds(off, n_rows), :] = y.astype(BF16)

    def group(g, carry):
        start = tab_ref[blk, TAB_START + g]
        piece = tab_ref[blk, TAB_PIECE + g]
        for k, n_rows in enumerate(MOE_PIECES):
            @pl.when(piece == k)
            def _(n_rows=n_rows):
                run_experts(g, start, n_rows)
        return carry

    lax.fori_loop(0, MOE_GROUPS, group, 0)

    part = MOE_BLOCK // MOE_OUT_PARTS
    ys = ys_ref[...]
    y_parts = [jnp.dot(p_out[i * part:(i + 1) * part], ys, preferred_element_type=F32)
               for i in range(MOE_OUT_PARTS)]
    for i in range(MOE_OUT_PARTS):
        rs = slice(i * part, (i + 1) * part)
        o_ref[rs, :] = _layer_norm_rows(x1[rs] + y_parts[i], g_ref[...], b_ref[...], LN_EPS_RESIDUAL)


def _moe(tab, x1, rout_t, rout, wg, wu, wd, g2, b2):
    n = x1.shape[0]
    tok = lambda i, t: (i, 0)
    row = lambda i, t: (0, 0)
    whole = lambda i, t: (0, 0, 0)
    resident = pl.Buffered(1)
    return pl.pallas_call(
        _moe_kernel,
        grid_spec=pltpu.PrefetchScalarGridSpec(
            num_scalar_prefetch=1,
            grid=(n // MOE_BLOCK,),
            in_specs=[pl.BlockSpec((MOE_BLOCK, D_MODEL), tok),
                      pl.BlockSpec((8, MOE_BLOCK), lambda i, t: (0, i)),
                      pl.BlockSpec((MOE_BLOCK, LANES), tok),
                      pl.BlockSpec((MOE_GROUPS, D_MODEL, D_GROUP), whole, pipeline_mode=resident),
                      pl.BlockSpec((MOE_GROUPS, D_MODEL, D_GROUP), whole, pipeline_mode=resident),
                      pl.BlockSpec((MOE_GROUPS, D_GROUP, D_MODEL), whole, pipeline_mode=resident),
                      pl.BlockSpec((1, D_MODEL), row),
                      pl.BlockSpec((1, D_MODEL), row)],
            out_specs=pl.BlockSpec((MOE_BLOCK, D_MODEL), tok),
            scratch_shapes=[pltpu.VMEM((MOE_SORTED_ROWS, D_MODEL), BF16),
                            pltpu.VMEM((MOE_SORTED_ROWS, D_MODEL), BF16),
                            pltpu.VMEM((MOE_SORTED_ROWS, LANES), F32)]),
        out_shape=jax.ShapeDtypeStruct((n, D_MODEL), F32),
        compiler_params=pltpu.CompilerParams(dimension_semantics=("parallel",),
                                             vmem_limit_bytes=VMEM_LIMIT),
        name="moe_experts_ln",
    )(tab, x1, rout_t, rout, wg, wu, wd, g2, b2)


def _router_weights(w_rg, b_rg, w_re, b_re):
    depth = w_rg.shape[0]
    pad = lambda a, n: jnp.zeros(a.shape[:-1] + (n,), F32)
    wt = jnp.concatenate([w_rg.astype(F32), pad(w_rg, 8 - MOE_GROUPS), w_re.astype(F32),
                          pad(w_re, ROUTER_ROWS - 8 - MOE_EXPERTS)], axis=-1)
    bt = jnp.concatenate([b_rg.astype(F32), pad(b_rg, 8 - MOE_GROUPS), b_re.astype(F32),
                          pad(b_re, ROUTER_ROWS - 8 - MOE_EXPERTS)], axis=-1)
    hi, lo = _split_bf16(wt)
    rw = jnp.concatenate([hi, lo, jnp.zeros((depth, D_MODEL, LANES - 2 * ROUTER_ROWS), BF16)], axis=-1)
    return rw, jnp.broadcast_to(bt[:, :, None], (depth, ROUTER_ROWS, LANES))


def kernel(x, w_in, w_out, na_rel_bias, sgu_ln_g, sgu_ln_b, sgu_w, sgu_b, mix_norm_g, ln1_g, ln1_b, router_group_w, router_group_b, router_expert_w, router_expert_b, expert_w_gate, expert_w_up, expert_w_down, ln2_g, ln2_b):
    batch, seq, d = x.shape
    n = batch * seq
    xf = x.reshape(n, d).astype(F32)
    row = lambda a: a.astype(F32).reshape(1, -1)
    tri = jnp.asarray(np.triu(np.ones((MOE_BLOCK, MOE_BLOCK), np.float32), k=1), BF16)
    tables = _na_bias_tables(na_rel_bias)
    rw, rb = _router_weights(router_group_w, router_group_b, router_expert_w, router_expert_b)
    for l in range(DEPTH):
        bs_full = jnp.repeat(sgu_b[l].astype(F32).T, SGU_GROUP_DIM, axis=1)
        q, k, v, s, wg, wu, wd = _in_proj(xf, w_in.astype(F32), expert_w_gate.astype(F32),
                                          expert_w_up.astype(F32), expert_w_down.astype(F32), l,
                                          row(sgu_ln_g[l]), row(sgu_ln_b[l]), sgu_w[l].astype(BF16),
                                          bs_full, row(mix_norm_g[l, D_NA:]) * INV_ALPHA)
        a = _na_attention(q, k, v, tables, l, row(mix_norm_g[l, :D_NA]) * INV_ALPHA, batch, seq)
        x1, rout_t, rout, tab = _out_proj(a, s, xf, w_out.astype(F32), l, row(ln1_g[l]),
                                          row(ln1_b[l]), rw, rb, tri)
        xf = _moe(tab[:, 0, :8], x1, rout_t, rout, wg, wu, wd, row(ln2_g[l]), row(ln2_b[l]))
    return xf.reshape(batch, seq, d).astype(x.dtype)
```

```python
import functools
import math

import numpy as np
import jax
import jax.numpy as jnp
from jax import lax
from jax.experimental import pallas as pl
from jax.experimental.pallas import tpu as pltpu

F32 = jnp.float32
BF16 = jnp.bfloat16

D_MODEL = 1024
DEPTH = 4
GRID_W = 64
NA_HEADS = 8
NA_HEAD_DIM = 64
NA_WIN_ROWS = 8
NA_WIN_COLS = 16
D_NA = NA_HEADS * NA_HEAD_DIM
SGU_GROUPS = 8
SGU_GROUP_DIM = 64
SGU_CHUNK = 128
D_SGU = SGU_GROUPS * SGU_GROUP_DIM
D_IN = 3 * D_NA + 2 * D_SGU
MOE_GROUPS = 4
MOE_EXPERTS_PER_GROUP = 4
MOE_EXPERTS = MOE_GROUPS * MOE_EXPERTS_PER_GROUP
D_EXPERT = 256
D_GROUP = MOE_EXPERTS_PER_GROUP * D_EXPERT
DEEPNORM_ALPHA = (2 * DEPTH) ** 0.25
LN_EPS = 1e-5
INV_ALPHA = 1.0 / DEEPNORM_ALPHA
LN_EPS_RESIDUAL = LN_EPS / DEEPNORM_ALPHA ** 2

LANES = 128
HEAD_PAIRS = NA_HEADS // 2
NEG_BIG = -1e30
BAND_KEYS = NA_WIN_ROWS * GRID_W
BIAS_TILES = 7
ROUTER_ROWS = 32

TM_PROJ = 512
NA_ROWS_PER_STEP = 8
NA_ROWS_IN_FLIGHT = 8
NA_STRIP = 16
MOE_BLOCK = 512
MOE_ALIGN = 16
MOE_PIECES = (64, 128, 144, 160, 192, 256, 384, MOE_BLOCK)
MOE_OUT_PARTS = 2
MOE_SORTED_ROWS = 768
OUT_BLOCKS_PER_STEP = 2
TAB_START = 0
TAB_PIECE = MOE_GROUPS
VMEM_LIMIT = 56 * 1024 * 1024


def _gelu_tanh(x):
    c = math.sqrt(2.0 / math.pi)
    return x * (0.5 * (1.0 + jnp.tanh(c * (x + 0.044715 * (x * x * x)))))


def _layer_norm_rows(h, g, b, eps):
    mu = jnp.mean(h, axis=-1, keepdims=True)
    hc = h - mu
    var = jnp.mean(hc * hc, axis=-1, keepdims=True)
    return hc * lax.rsqrt(var + eps) * g + b


def _in_proj_kernel(x_ref, w_ref, wg_ref, wu_ref, wd_ref, lng_ref, lnb_ref, ws_ref, bs_ref, gs_ref,
                    seg_ref, q_ref, k_ref, v_ref, s_ref, wg_out, wu_out, wd_out, wb_ref):
    @pl.when(pl.program_id(0) == 0)
    def _():
        wb_ref[...] = w_ref[0].astype(BF16)

    wg_out[0] = wg_ref[0].astype(BF16)
    wu_out[0] = wu_ref[0].astype(BF16)
    wd_out[...] = wd_ref[0].astype(BF16)

    xb = x_ref[...].astype(BF16)

    def mm(j):
        return jnp.dot(xb, wb_ref[:, j * D_NA:(j + 1) * D_NA], preferred_element_type=F32)

    hv, hu = mm(4), mm(3)

    def do_q():
        q_ref[...] = (mm(0) * (NA_HEAD_DIM ** -0.5)).astype(BF16)

    def do_k():
        k_ref[...] = mm(1).astype(BF16)

    def do_v():
        v_ref[...] = mm(2).astype(BF16)

    s_ref[...] = _spatial_gating_block(_gelu_tanh(hu), _gelu_tanh(hv), lng_ref, lnb_ref, ws_ref,
                                       bs_ref, gs_ref, seg_ref, (do_q, do_k, do_v))


def _in_proj(x, w_in, w_gate, w_up, w_down, layer, ln_g, ln_b, w_s_bf16, bs_full, gain):
    n = x.shape[0]
    steps = n // TM_PROJ
    gu_rows = MOE_EXPERTS * D_MODEL // steps
    parts = D_MODEL // gu_rows
    dn_rows = MOE_EXPERTS * D_EXPERT // steps
    assert gu_rows * steps == MOE_EXPERTS * D_MODEL and parts * gu_rows == D_MODEL
    assert dn_rows * steps == MOE_EXPERTS * D_EXPERT and TM_PROJ % SGU_CHUNK == 0
    same_group = np.arange(LANES)[:, None] // SGU_GROUP_DIM == np.arange(LANES)[None, :] // SGU_GROUP_DIM
    seg_ones = jnp.asarray(same_group, BF16)
    tok = lambda i: (i, 0)
    row = lambda i: (0, 0)
    slab = lambda i: (layer, i, 0)
    grouped = lambda i: (i // (parts * MOE_EXPERTS_PER_GROUP), i % parts,
                         (i // parts) % MOE_EXPERTS_PER_GROUP)
    depth = w_gate.shape[0]
    outs = pl.pallas_call(
        _in_proj_kernel,
        grid=(steps,),
        in_specs=[pl.BlockSpec((TM_PROJ, D_MODEL), tok),
                  pl.BlockSpec((1, D_MODEL, D_IN), lambda i: (layer, 0, 0),
                               pipeline_mode=pl.Buffered(1)),
                  pl.BlockSpec((1, gu_rows, D_EXPERT), slab),
                  pl.BlockSpec((1, gu_rows, D_EXPERT), slab),
                  pl.BlockSpec((1, dn_rows, D_MODEL), slab),
                  pl.BlockSpec((1, D_SGU), row),
                  pl.BlockSpec((1, D_SGU), row),
                  pl.BlockSpec((SGU_GROUPS, SGU_CHUNK, SGU_CHUNK), lambda i: (0, 0, 0)),
                  pl.BlockSpec((SGU_CHUNK, D_SGU), row),
                  pl.BlockSpec((1, D_SGU), row),
                  pl.BlockSpec((LANES, LANES), row)],
        out_specs=[pl.BlockSpec((TM_PROJ, D_NA), tok)] * 4
                  + [pl.BlockSpec((1, gu_rows, D_EXPERT), grouped)] * 2
                  + [pl.BlockSpec((dn_rows, D_MODEL), tok)],
        out_shape=[jax.ShapeDtypeStruct((n, D_NA), BF16)] * 4
                  + [jax.ShapeDtypeStruct((MOE_GROUPS, D_MODEL, D_GROUP), BF16)] * 2
                  + [jax.ShapeDtypeStruct((MOE_GROUPS * D_GROUP, D_MODEL), BF16)],
        scratch_shapes=[pltpu.VMEM((D_MODEL, D_IN), BF16)],
        compiler_params=pltpu.CompilerParams(dimension_semantics=("arbitrary",),
                                             vmem_limit_bytes=VMEM_LIMIT),
        name="in_proj_sgu",
    )(x, w_in,
      w_gate.reshape(depth, MOE_EXPERTS * D_MODEL, D_EXPERT),
      w_up.reshape(depth, MOE_EXPERTS * D_MODEL, D_EXPERT),
      w_down.reshape(depth, MOE_EXPERTS * D_EXPERT, D_MODEL),
      ln_g, ln_b, w_s_bf16, bs_full, gain, seg_ones)
    q, k, v, s, wg, wu, wd = outs
    return q, k, v, s, wg, wu, wd.reshape(MOE_GROUPS, D_GROUP, D_MODEL)


def _na_bias_tables(rel_bias):
    cols = np.arange(GRID_W)
    col_start = np.clip(cols - NA_WIN_COLS // 2, 0, GRID_W - NA_WIN_COLS)
    kc = np.arange(GRID_W)
    valid = (kc[None, :] >= col_start[:, None]) & (kc[None, :] < col_start[:, None] + NA_WIN_COLS)
    dc = kc[None, :] - cols[:, None] + NA_WIN_COLS - 1
    onehot = (dc[None] == np.arange(2 * NA_WIN_COLS - 1)[:, None, None]) & valid[None]
    depth = rel_bias.shape[0]
    f = jnp.einsum('lhad,dck->lhack', rel_bias.astype(F32), jnp.asarray(onehot, F32),
                   precision=lax.Precision.HIGHEST)
    f = jnp.where(valid[None, None, None], f, NEG_BIG)
    n_off = 2 * NA_WIN_ROWS - 1
    f = jnp.transpose(f, (0, 1, 3, 2, 4)).reshape(depth, NA_HEADS, GRID_W, n_off * GRID_W)
    g = f.reshape(depth, HEAD_PAIRS, 2 * GRID_W, n_off * GRID_W)
    width = BIAS_TILES * LANES
    tabs = jnp.stack([g[..., :width], g[..., GRID_W:GRID_W + width]], axis=1)
    tabs = tabs.reshape(depth, 2, HEAD_PAIRS, 2 * GRID_W, BIAS_TILES, LANES)
    return jnp.transpose(tabs, (0, 1, 2, 4, 3, 5))


def _na_kernel(q_ref, k_ref, v_ref, t_ref, g_ref, o_ref, s_all, p_all, a_ref):
    rows = k_ref.shape[0] // GRID_W
    blk = pl.program_id(1)
    lane = lax.broadcasted_iota(jnp.int32, (GRID_W, LANES), 1)
    first_head = lane < NA_HEAD_DIM
    key_tiles = BAND_KEYS // LANES

    def row_geometry(rr):
        r = blk * NA_ROWS_PER_STEP + rr
        rs = jnp.clip(r - NA_WIN_ROWS // 2, 0, rows - NA_WIN_ROWS)
        a0 = rs - r + (NA_WIN_ROWS - 1)
        q_off = pl.multiple_of(rr * GRID_W, GRID_W)
        k_off = pl.multiple_of(rs * GRID_W, GRID_W)
        return a0 % 2, a0 // 2, q_off, k_off

    def row_scores(rr, slot):
        s_ref = s_all.at[slot]
        _, _, q_off, k_off = row_geometry(rr)
        for hp in range(HEAD_PAIRS):
            cs = slice(hp * LANES, (hp + 1) * LANES)
            qp = q_ref[pl.ds(q_off, GRID_W), cs]
            kp = k_ref[pl.ds(k_off, BAND_KEYS), cs]
            zero = jnp.zeros_like(qp)
            q2 = jnp.concatenate([jnp.where(first_head, qp, zero),
                                  jnp.where(first_head, zero, qp)], axis=0)
            s_ref[hp] = lax.dot_general(q2, kp, (((1,), (1,)), ((), ())),
                                        preferred_element_type=F32)

    def row_softmax_pv(rr, slot):
        s_ref = s_all.at[slot]
        p_ref = p_all.at[slot]
        par, j0, q_off, k_off = row_geometry(rr)
        for hp in range(HEAD_PAIRS):
            cs = slice(hp * LANES, (hp + 1) * LANES)
            inv_l = []
            for ch in range(2 * GRID_W // NA_STRIP):
                rsl = slice(ch * NA_STRIP, (ch + 1) * NA_STRIP)
                bias = jnp.concatenate([t_ref[0, par, hp, j0 + t, rsl, :] for t in range(key_tiles)],
                                       axis=1)
                sc = s_ref[hp, rsl, :] + bias
                m = jnp.max(sc, axis=1, keepdims=True)
                p = jnp.exp(sc - m)
                inv_l.append(1.0 / jnp.sum(p, axis=1, keepdims=True))
                p_ref[hp, rsl, :] = p.astype(BF16)
            vp = v_ref[pl.ds(k_off, BAND_KEYS), cs]
            o = jnp.dot(p_ref[hp], vp, preferred_element_type=F32)
            o = jnp.concatenate([o[ch * NA_STRIP:(ch + 1) * NA_STRIP] * inv_l[ch]
                                 for ch in range(len(inv_l))], axis=0)
            a_ref[pl.ds(q_off, GRID_W), cs] = jnp.where(first_head, o[:GRID_W], o[GRID_W:])

    def row_group(i, carry):
        for slot in range(NA_ROWS_IN_FLIGHT):
            row_scores(i * NA_ROWS_IN_FLIGHT + slot, slot)
            row_softmax_pv(i * NA_ROWS_IN_FLIGHT + slot, slot)
        return carry

    lax.fori_loop(0, NA_ROWS_PER_STEP // NA_ROWS_IN_FLIGHT, row_group, 0)

    a = a_ref[...]
    ms = jnp.mean(a * a, axis=-1, keepdims=True)
    o_ref[...] = (a * lax.rsqrt(ms + LN_EPS) * g_ref[...]).astype(o_ref.dtype)


def _na_attention(q, k, v, tables, layer, gain, batch, seq):
    n = q.shape[0]
    steps = seq // (GRID_W * NA_ROWS_PER_STEP)
    tq = NA_ROWS_PER_STEP * GRID_W
    return pl.pallas_call(
        _na_kernel,
        grid=(batch, steps),
        in_specs=[pl.BlockSpec((tq, D_NA), lambda b, i: (b * steps + i, 0)),
                  pl.BlockSpec((seq, D_NA), lambda b, i: (b, 0)),
                  pl.BlockSpec((seq, D_NA), lambda b, i: (b, 0)),
                  pl.BlockSpec((1,) + tables.shape[1:], lambda b, i: (layer, 0, 0, 0, 0, 0)),
                  pl.BlockSpec((1, D_NA), lambda b, i: (0, 0))],
        out_specs=pl.BlockSpec((tq, D_NA), lambda b, i: (b * steps + i, 0)),
        out_shape=jax.ShapeDtypeStruct((n, D_NA), BF16),
        scratch_shapes=[pltpu.VMEM((NA_ROWS_IN_FLIGHT, HEAD_PAIRS, 2 * GRID_W, BAND_KEYS), F32),
                        pltpu.VMEM((NA_ROWS_IN_FLIGHT, HEAD_PAIRS, 2 * GRID_W, BAND_KEYS), BF16),
                        pltpu.VMEM((tq, D_NA), F32)],
        compiler_params=pltpu.CompilerParams(dimension_semantics=("parallel", "parallel"),
                                             vmem_limit_bytes=VMEM_LIMIT),
        name="na_attention",
    )(q, k, v, tables, gain)


def _spatial_gating_block(u, vs, lng_ref, lnb_ref, ws_ref, bs_ref, g_ref, seg_ref, between):
    tm = u.shape[0]
    lane = lax.broadcasted_iota(jnp.int32, (SGU_CHUNK, LANES), 1)
    first = lane < SGU_GROUP_DIM
    inv = 1.0 / SGU_GROUP_DIM
    n_tiles = D_SGU // LANES
    n_rows = tm * n_tiles

    def seg_mean(t):
        hi, lo = _split_bf16(t)
        s = jnp.dot(jnp.concatenate([hi, lo], axis=0), seg_ref[...], preferred_element_type=F32)
        return (s[:n_rows] + s[n_rows:]) * inv

    x = jnp.concatenate([vs[:, j * LANES:(j + 1) * LANES] for j in range(n_tiles)], axis=0)
    mean = seg_mean(x)
    between[0]()
    xc = x - mean
    var = seg_mean(xc * xc)
    between[1]()
    xn = xc * lax.rsqrt(var + LN_EPS)

    mixes = {}
    for c in range(tm // SGU_CHUNK):
        for j in range(n_tiles):
            cs = slice(j * LANES, (j + 1) * LANES)
            r0 = j * tm + c * SGU_CHUNK
            y = (xn[r0:r0 + SGU_CHUNK] * lng_ref[:, cs] + lnb_ref[:, cs]).astype(BF16)
            mixes[c, j] = jnp.dot(jnp.concatenate([ws_ref[2 * j], ws_ref[2 * j + 1]], axis=0), y,
                                  preferred_element_type=F32)
    between[2]()

    out = []
    for c in range(tm // SGU_CHUNK):
        rs = slice(c * SGU_CHUNK, (c + 1) * SGU_CHUNK)
        tiles = []
        for j in range(n_tiles):
            cs = slice(j * LANES, (j + 1) * LANES)
            m = mixes[c, j]
            mixed = jnp.where(first, m[:SGU_CHUNK], m[SGU_CHUNK:]) + bs_ref[:, cs]
            tiles.append(u[rs, cs] * mixed)
        so = jnp.concatenate(tiles, axis=1)
        ms = jnp.mean(so * so, axis=-1, keepdims=True)
        out.append((so * lax.rsqrt(ms + LN_EPS) * g_ref[...]).astype(BF16))
    return jnp.concatenate(out, axis=0)


def _split_bf16(a):
    hi = a.astype(BF16)
    lo = (a - hi.astype(F32)).astype(BF16)
    return hi, lo


def _route(lg, tri):
    gl = [lg[g:g + 1, :] for g in range(MOE_GROUPS)]
    gmax = functools.reduce(jnp.maximum, gl)
    gidx = jnp.full(gmax.shape, MOE_GROUPS - 1, jnp.int32)
    for g in range(MOE_GROUPS - 2, -1, -1):
        gidx = jnp.where(gl[g] == gmax, g, gidx)
    denom = functools.reduce(lambda a, b: a + b, [jnp.exp(t - gmax) for t in gl])
    gate = 1.0 / denom

    def expert_logit(i):
        rows_ = [lg[8 + MOE_EXPERTS_PER_GROUP * g + i:9 + MOE_EXPERTS_PER_GROUP * g + i, :]
                 for g in range(MOE_GROUPS)]
        sel = rows_[MOE_GROUPS - 1]
        for g in range(MOE_GROUPS - 2, -1, -1):
            sel = jnp.where(gidx == g, rows_[g], sel)
        return sel

    el = [expert_logit(i) for i in range(MOE_EXPERTS_PER_GROUP)]
    v1 = functools.reduce(jnp.maximum, el)
    i1 = jnp.full(v1.shape, MOE_EXPERTS_PER_GROUP - 1, jnp.int32)
    for i in range(MOE_EXPERTS_PER_GROUP - 2, -1, -1):
        i1 = jnp.where(el[i] == v1, i, i1)
    rest = [jnp.where(i1 == i, -jnp.inf, el[i]) for i in range(MOE_EXPERTS_PER_GROUP)]
    v2 = functools.reduce(jnp.maximum, rest)
    i2 = jnp.full(v2.shape, MOE_EXPERTS_PER_GROUP - 1, jnp.int32)
    for i in range(MOE_EXPERTS_PER_GROUP - 2, -1, -1):
        i2 = jnp.where((rest[i] == v2) & (i1 != i), i, i2)
    e2 = jnp.exp(v2 - v1)
    w1 = 1.0 / (1.0 + e2)
    w2 = e2 * w1
    within = [jnp.where(i1 == i, w1, 0.0) + jnp.where(i2 == i, w2, 0.0)
              for i in range(MOE_EXPERTS_PER_GROUP)]
    tm = MOE_BLOCK
    cw = [within[i] * gate * INV_ALPHA for i in range(MOE_EXPERTS_PER_GROUP)]

    onehot = [jnp.where(gidx == g, 1.0, 0.0) for g in range(MOE_GROUPS)]
    oh_mat = jnp.concatenate(onehot + [jnp.zeros((16 - MOE_GROUPS, tm), F32)], axis=0)
    before = jnp.dot(oh_mat.astype(BF16), tri, preferred_element_type=F32)
    rank = functools.reduce(lambda a, b: a + b,
                            [onehot[g] * before[g:g + 1, :] for g in range(MOE_GROUPS)])
    starts, pieces = [], []
    start_tok = jnp.zeros_like(rank)
    end = jnp.zeros((1, 1), F32)
    for g in range(MOE_GROUPS):
        count = jnp.sum(onehot[g], axis=1, keepdims=True)
        start_tok = start_tok + onehot[g] * end
        starts.append(end)
        end = end + MOE_ALIGN * jnp.floor((count + (MOE_ALIGN - 1)) * (1.0 / MOE_ALIGN))
        piece = functools.reduce(lambda a, b: a + b,
                                 [jnp.where(count > p, 1.0, 0.0) for p in MOE_PIECES[:-1]])
        pieces.append(jnp.where(count > 0.0, piece, -1.0))
    dest = rank + start_tok
    rout_t = jnp.concatenate([dest] + cw + [gidx.astype(F32), jnp.zeros((2, tm), F32)], axis=0)

    c = lax.broadcasted_iota(jnp.int32, (8, LANES), 1)
    tab = jnp.zeros((8, LANES), F32)
    for g in range(MOE_GROUPS):
        tab = jnp.where(c == TAB_START + g, starts[g], tab)
        tab = jnp.where(c == TAB_PIECE + g, pieces[g], tab)
    return rout_t, tab.astype(jnp.int32)


def _out_proj_kernel(a_ref, s_ref, x_ref, w_ref, g_ref, b_ref, rw_ref, rb_ref, tri_ref,
                     x1_ref, rt_ref, r_ref, tab_ref, wb_ref):
    @pl.when(pl.program_id(0) == 0)
    def _():
        wb_ref[...] = w_ref[0].astype(BF16)

    mixes = []
    for blk in range(OUT_BLOCKS_PER_STEP):
        rs = slice(blk * MOE_BLOCK, (blk + 1) * MOE_BLOCK)
        mix = jnp.dot(a_ref[rs, :], wb_ref[:D_NA, :], preferred_element_type=F32)
        mixes.append(mix + jnp.dot(s_ref[rs, :], wb_ref[D_NA:, :], preferred_element_type=F32))
    logits = []
    for blk in range(OUT_BLOCKS_PER_STEP):
        rs = slice(blk * MOE_BLOCK, (blk + 1) * MOE_BLOCK)
        x1 = _layer_norm_rows(x_ref[rs, :] + mixes[blk], g_ref[...], b_ref[...], LN_EPS_RESIDUAL)
        x1_ref[rs, :] = x1
        xh, xl = _split_bf16(x1)
        logits.append(jnp.dot(xh, rw_ref[0], preferred_element_type=F32)
                      + jnp.dot(xl, rw_ref[0], preferred_element_type=F32))
    for blk in range(OUT_BLOCKS_PER_STEP):
        rs = slice(blk * MOE_BLOCK, (blk + 1) * MOE_BLOCK)
        lg_t = logits[blk].T
        lg = lg_t[:ROUTER_ROWS] + lg_t[ROUTER_ROWS:2 * ROUTER_ROWS] + rb_ref[0, :, 0:1]
        rout_t, tab = _route(lg, tri_ref[...])
        rt_ref[:, rs] = rout_t
        r_ref[rs, :] = jnp.concatenate([rout_t, jnp.zeros((LANES - 8, MOE_BLOCK), F32)], axis=0).T
        tab_ref[blk] = tab


def _out_proj(a, s, x, w_out, layer, g1, b1, rw, rb, tri):
    n = x.shape[0]
    tok = lambda i: (i, 0)
    row = lambda i: (0, 0)
    tm = OUT_BLOCKS_PER_STEP * MOE_BLOCK
    return pl.pallas_call(
        _out_proj_kernel,
        grid=(n // tm,),
        in_specs=[pl.BlockSpec((tm, D_NA), tok),
                  pl.BlockSpec((tm, D_SGU), tok),
                  pl.BlockSpec((tm, D_MODEL), tok),
                  pl.BlockSpec((1, D_MODEL, D_MODEL), lambda i: (layer, 0, 0),
                               pipeline_mode=pl.Buffered(1)),
                  pl.BlockSpec((1, D_MODEL), row),
                  pl.BlockSpec((1, D_MODEL), row),
                  pl.BlockSpec((1, D_MODEL, LANES), lambda i: (layer, 0, 0)),
                  pl.BlockSpec((1, ROUTER_ROWS, LANES), lambda i: (layer, 0, 0)),
                  pl.BlockSpec((MOE_BLOCK, MOE_BLOCK), row)],
        out_specs=[pl.BlockSpec((tm, D_MODEL), tok),
                   pl.BlockSpec((8, tm), lambda i: (0, i)),
                   pl.BlockSpec((tm, LANES), tok),
                   pl.BlockSpec((OUT_BLOCKS_PER_STEP, 8, LANES), lambda i: (i, 0, 0))],
        out_shape=[jax.ShapeDtypeStruct((n, D_MODEL), F32),
                   jax.ShapeDtypeStruct((8, n), F32),
                   jax.ShapeDtypeStruct((n, LANES), F32),
                   jax.ShapeDtypeStruct((n // MOE_BLOCK, 8, LANES), jnp.int32)],
        scratch_shapes=[pltpu.VMEM((D_MODEL, D_MODEL), BF16)],
        compiler_params=pltpu.CompilerParams(dimension_semantics=("arbitrary",),
                                             vmem_limit_bytes=VMEM_LIMIT),
        name="out_proj_ln_router",
    )(a, s, x, w_out, g1, b1, rw, rb, tri)


def _moe_kernel(tab_ref, x_ref, rt_ref, r_ref, wg_ref, wu_ref, wd_ref, g_ref, b_ref, o_ref,
                xs_ref, ys_ref, cw_ref):
    blk = pl.program_id(0)
    x1 = x_ref[...]
    dest_row = rt_ref[0:1, :].astype(jnp.int32)
    rows_i = lax.broadcasted_iota(jnp.int32, (MOE_SORTED_ROWS, MOE_BLOCK), 0)
    p_in = jnp.where(rows_i == dest_row, 1.0, 0.0).astype(BF16)
    xs_ref[...] = jnp.dot(p_in, x1.astype(BF16), preferred_element_type=F32).astype(BF16)
    r_hi, r_lo = _split_bf16(r_ref[...])
    cw2 = jnp.dot(p_in, jnp.concatenate([r_hi, r_lo], axis=1), preferred_element_type=F32)
    cw_ref[...] = cw2[:, :LANES] + cw2[:, LANES:]
    ys_ref[...] = jnp.zeros_like(ys_ref)
    dest_col = r_ref[:, 0:1].astype(jnp.int32)
    cols_i = lax.broadcasted_iota(jnp.int32, (MOE_BLOCK, MOE_SORTED_ROWS), 1)
    p_out = jnp.where(cols_i == dest_col, 1.0, 0.0).astype(BF16)

    def run_experts(g, start, n_rows):
        off = pl.multiple_of(start, MOE_ALIGN)
        xs = xs_ref[pl.ds(off, n_rows), :]
        cw = cw_ref[pl.ds(off, n_rows), :]
        hg = jnp.dot(xs, wg_ref[g], preferred_element_type=F32)
        hu = jnp.dot(xs, wu_ref[g], preferred_element_type=F32)
        scale = jnp.concatenate([jnp.broadcast_to(cw[:, 1 + i:2 + i], (n_rows, D_EXPERT))
                                 for i in range(MOE_EXPERTS_PER_GROUP)], axis=1)
        act = hg * (1.0 / (1.0 + jnp.exp(-hg))) * hu * scale
        y = jnp.dot(act.astype(BF16), wd_ref[g], preferred_element_type=F32)
        ys_ref[pl.ds(off, n_rows), :] = y.astype(BF16)

    def group(g, carry):
        start = tab_ref[blk, TAB_START + g]
        piece = tab_ref[blk, TAB_PIECE + g]
        for k, n_rows in enumerate(MOE_PIECES):
            @pl.when(piece == k)
            def _(n_rows=n_rows):
                run_experts(g, start, n_rows)
        return carry

    lax.fori_loop(0, MOE_GROUPS, group, 0)

    part = MOE_BLOCK // MOE_OUT_PARTS
    ys = ys_ref[...]
    y_parts = [jnp.dot(p_out[i * part:(i + 1) * part], ys, preferred_element_type=F32)
               for i in range(MOE_OUT_PARTS)]
    for i in range(MOE_OUT_PARTS):
        rs = slice(i * part, (i + 1) * part)
        o_ref[rs, :] = _layer_norm_rows(x1[rs] + y_parts[i], g_ref[...], b_ref[...], LN_EPS_RESIDUAL)


def _moe(tab, x1, rout_t, rout, wg, wu, wd, g2, b2):
    n = x1.shape[0]
    tok = lambda i, t: (i, 0)
    row = lambda i, t: (0, 0)
    whole = lambda i, t: (0, 0, 0)
    resident = pl.Buffered(1)
    return pl.pallas_call(
        _moe_kernel,
        grid_spec=pltpu.PrefetchScalarGridSpec(
            num_scalar_prefetch=1,
            grid=(n // MOE_BLOCK,),
            in_specs=[pl.BlockSpec((MOE_BLOCK, D_MODEL), tok),
                      pl.BlockSpec((8, MOE_BLOCK), lambda i, t: (0, i)),
                      pl.BlockSpec((MOE_BLOCK, LANES), tok),
                      pl.BlockSpec((MOE_GROUPS, D_MODEL, D_GROUP), whole, pipeline_mode=resident),
                      pl.BlockSpec((MOE_GROUPS, D_MODEL, D_GROUP), whole, pipeline_mode=resident),
                      pl.BlockSpec((MOE_GROUPS, D_GROUP, D_MODEL), whole, pipeline_mode=resident),
                      pl.BlockSpec((1, D_MODEL), row),
                      pl.BlockSpec((1, D_MODEL), row)],
            out_specs=pl.BlockSpec((MOE_BLOCK, D_MODEL), tok),
            scratch_shapes=[pltpu.VMEM((MOE_SORTED_ROWS, D_MODEL), BF16),
                            pltpu.VMEM((MOE_SORTED_ROWS, D_MODEL), BF16),
                            pltpu.VMEM((MOE_SORTED_ROWS, LANES), F32)]),
        out_shape=jax.ShapeDtypeStruct((n, D_MODEL), F32),
        compiler_params=pltpu.CompilerParams(dimension_semantics=("parallel",),
                                             vmem_limit_bytes=VMEM_LIMIT),
        name="moe_experts_ln",
    )(tab, x1, rout_t, rout, wg, wu, wd, g2, b2)


def _router_weights(w_rg, b_rg, w_re, b_re):
    depth = w_rg.shape[0]
    pad = lambda a, n: jnp.zeros(a.shape[:-1] + (n,), F32)
    wt = jnp.concatenate([w_rg.astype(F32), pad(w_rg, 8 - MOE_GROUPS), w_re.astype(F32),
                          pad(w_re, ROUTER_ROWS - 8 - MOE_EXPERTS)], axis=-1)
    bt = jnp.concatenate([b_rg.astype(F32), pad(b_rg, 8 - MOE_GROUPS), b_re.astype(F32),
                          pad(b_re, ROUTER_ROWS - 8 - MOE_EXPERTS)], axis=-1)
    hi, lo = _split_bf16(wt)
    rw = jnp.concatenate([hi, lo, jnp.zeros((depth, D_MODEL, LANES - 2 * ROUTER_ROWS), BF16)], axis=-1)
    return rw, jnp.broadcast_to(bt[:, :, None], (depth, ROUTER_ROWS, LANES))


def kernel(x, w_in, w_out, na_rel_bias, sgu_ln_g, sgu_ln_b, sgu_w, sgu_b, mix_norm_g, ln1_g, ln1_b, router_group_w, router_group_b, router_expert_w, router_expert_b, expert_w_gate, expert_w_up, expert_w_down, ln2_g, ln2_b):
    batch, seq, d = x.shape
    n = batch * seq
    xf = x.reshape(n, d).astype(F32)
    row = lambda a: a.astype(F32).reshape(1, -1)
    tri = jnp.asarray(np.triu(np.ones((MOE_BLOCK, MOE_BLOCK), np.float32), k=1), BF16)
    tables = _na_bias_tables(na_rel_bias)
    rw, rb = _router_weights(router_group_w, router_group_b, router_expert_w, router_expert_b)
    for l in range(DEPTH):
        bs_full = jnp.repeat(sgu_b[l].astype(F32).T, SGU_GROUP_DIM, axis=1)
        q, k, v, s, wg, wu, wd = _in_proj(xf, w_in.astype(F32), expert_w_gate.astype(F32),
                                          expert_w_up.astype(F32), expert_w_down.astype(F32), l,
                                          row(sgu_ln_g[l]), row(sgu_ln_b[l]), sgu_w[l].astype(BF16),
                                          bs_full, row(mix_norm_g[l, D_NA:]) * INV_ALPHA)
        a = _na_attention(q, k, v, tables, l, row(mix_norm_g[l, :D_NA]) * INV_ALPHA, batch, seq)
        x1, rout_t, rout, tab = _out_proj(a, s, xf, w_out.astype(F32), l, row(ln1_g[l]),
                                          row(ln1_b[l]), rw, rb, tri)
        xf = _moe(tab[:, 0, :8], x1, rout_t, rout, wg, wu, wd, row(ln2_g[l]), row(ln2_b[l]))
    return xf.reshape(batch, seq, d).astype(x.dtype)
```

```python
import functools
import math

import numpy as np
import jax
import jax.numpy as jnp
from jax import lax
from jax.experimental import pallas as pl
from jax.experimental.pallas import tpu as pltpu

F32 = jnp.float32
BF16 = jnp.bfloat16

D_MODEL = 1024
DEPTH = 4
GRID_W = 64
NA_HEADS = 8
NA_HEAD_DIM = 64
NA_WIN_ROWS = 8
NA_WIN_COLS = 16
D_NA = NA_HEADS * NA_HEAD_DIM
SGU_GROUPS = 8
SGU_GROUP_DIM = 64
SGU_CHUNK = 128
D_SGU = SGU_GROUPS * SGU_GROUP_DIM
D_IN = 3 * D_NA + 2 * D_SGU
MOE_GROUPS = 4
MOE_EXPERTS_PER_GROUP = 4
MOE_EXPERTS = MOE_GROUPS * MOE_EXPERTS_PER_GROUP
D_EXPERT = 256
D_GROUP = MOE_EXPERTS_PER_GROUP * D_EXPERT
DEEPNORM_ALPHA = (2 * DEPTH) ** 0.25
LN_EPS = 1e-5
INV_ALPHA = 1.0 / DEEPNORM_ALPHA
LN_EPS_RESIDUAL = LN_EPS / DEEPNORM_ALPHA ** 2

LANES = 128
HEAD_PAIRS = NA_HEADS // 2
NEG_BIG = -1e30
BAND_KEYS = NA_WIN_ROWS * GRID_W
BIAS_TILES = 7
ROUTER_ROWS = 32

TM_PROJ = 1024
NA_ROWS_PER_STEP = 16
NA_ROWS_IN_FLIGHT = 8
NA_STRIP = 16
MOE_BLOCK = 512
MOE_ALIGN = 16
MOE_PIECES = (64, 128, 144, 160, 192, 256, 384, MOE_BLOCK)
MOE_OUT_PARTS = 2
MOE_SORTED_ROWS = 768
OUT_BLOCKS_PER_STEP = 2
TAB_START = 0
TAB_PIECE = MOE_GROUPS
VMEM_LIMIT = 56 * 1024 * 1024


def _gelu_tanh(x):
    c = math.sqrt(2.0 / math.pi)
    return x * (0.5 * (1.0 + jnp.tanh(c * (x + 0.044715 * (x * x * x)))))


def _layer_norm_rows(h, g, b, eps):
    mu = jnp.mean(h, axis=-1, keepdims=True)
    hc = h - mu
    var = jnp.mean(hc * hc, axis=-1, keepdims=True)
    return hc * lax.rsqrt(var + eps) * g + b


def _in_proj_kernel(x_ref, w_ref, wg_ref, wu_ref, wd_ref, lng_ref, lnb_ref, ws_ref, bs_ref, gs_ref,
                    seg_ref, q_ref, k_ref, v_ref, s_ref, wg_out, wu_out, wd_out, wb_ref):
    @pl.when(pl.program_id(0) == 0)
    def _():
        wb_ref[...] = w_ref[0].astype(BF16)

    wg_out[0] = wg_ref[0].astype(BF16)
    wu_out[0] = wu_ref[0].astype(BF16)
    wd_out[...] = wd_ref[0].astype(BF16)

    xb = x_ref[...].astype(BF16)

    def mm(j):
        return jnp.dot(xb, wb_ref[:, j * D_NA:(j + 1) * D_NA], preferred_element_type=F32)

    hv, hu = mm(4), mm(3)

    def do_q():
        q_ref[...] = (mm(0) * (NA_HEAD_DIM ** -0.5)).astype(BF16)

    def do_k():
        k_ref[...] = mm(1).astype(BF16)

    def do_v():
        v_ref[...] = mm(2).astype(BF16)

    s_ref[...] = _spatial_gating_block(_gelu_tanh(hu), _gelu_tanh(hv), lng_ref, lnb_ref, ws_ref,
                                       bs_ref, gs_ref, seg_ref, (do_q, do_k, do_v))


def _in_proj(x, w_in, w_gate, w_up, w_down, layer, ln_g, ln_b, w_s_bf16, bs_full, gain):
    n = x.shape[0]
    steps = n // TM_PROJ
    gu_rows = MOE_EXPERTS * D_MODEL // steps
    parts = D_MODEL // gu_rows
    dn_rows = MOE_EXPERTS * D_EXPERT // steps
    assert gu_rows * steps == MOE_EXPERTS * D_MODEL and parts * gu_rows == D_MODEL
    assert dn_rows * steps == MOE_EXPERTS * D_EXPERT and TM_PROJ % SGU_CHUNK == 0
    same_group = np.arange(LANES)[:, None] // SGU_GROUP_DIM == np.arange(LANES)[None, :] // SGU_GROUP_DIM
    seg_ones = jnp.asarray(same_group, BF16)
    tok = lambda i: (i, 0)
    row = lambda i: (0, 0)
    slab = lambda i: (layer, i, 0)
    grouped = lambda i: (i // (parts * MOE_EXPERTS_PER_GROUP), i % parts,
                         (i // parts) % MOE_EXPERTS_PER_GROUP)
    depth = w_gate.shape[0]
    outs = pl.pallas_call(
        _in_proj_kernel,
        grid=(steps,),
        in_specs=[pl.BlockSpec((TM_PROJ, D_MODEL), tok),
                  pl.BlockSpec((1, D_MODEL, D_IN), lambda i: (layer, 0, 0),
                               pipeline_mode=pl.Buffered(1)),
                  pl.BlockSpec((1, gu_rows, D_EXPERT), slab),
                  pl.BlockSpec((1, gu_rows, D_EXPERT), slab),
                  pl.BlockSpec((1, dn_rows, D_MODEL), slab),
                  pl.BlockSpec((1, D_SGU), row),
                  pl.BlockSpec((1, D_SGU), row),
                  pl.BlockSpec((SGU_GROUPS, SGU_CHUNK, SGU_CHUNK), lambda i: (0, 0, 0)),
                  pl.BlockSpec((SGU_CHUNK, D_SGU), row),
                  pl.BlockSpec((1, D_SGU), row),
                  pl.BlockSpec((LANES, LANES), row)],
        out_specs=[pl.BlockSpec((TM_PROJ, D_NA), tok)] * 4
                  + [pl.BlockSpec((1, gu_rows, D_EXPERT), grouped)] * 2
                  + [pl.BlockSpec((dn_rows, D_MODEL), tok)],
        out_shape=[jax.ShapeDtypeStruct((n, D_NA), BF16)] * 4
                  + [jax.ShapeDtypeStruct((MOE_GROUPS, D_MODEL, D_GROUP), BF16)] * 2
                  + [jax.ShapeDtypeStruct((MOE_GROUPS * D_GROUP, D_MODEL), BF16)],
        scratch_shapes=[pltpu.VMEM((D_MODEL, D_IN), BF16)],
        compiler_params=pltpu.CompilerParams(dimension_semantics=("arbitrary",),
                                             vmem_limit_bytes=VMEM_LIMIT),
        name="in_proj_sgu",
    )(x, w_in,
      w_gate.reshape(depth, MOE_EXPERTS * D_MODEL, D_EXPERT),
      w_up.reshape(depth, MOE_EXPERTS * D_MODEL, D_EXPERT),
      w_down.reshape(depth, MOE_EXPERTS * D_EXPERT, D_MODEL),
      ln_g, ln_b, w_s_bf16, bs_full, gain, seg_ones)
    q, k, v, s, wg, wu, wd = outs
    return q, k, v, s, wg, wu, wd.reshape(MOE_GROUPS, D_GROUP, D_MODEL)


def _na_bias_compact(rel_bias):
    cols = np.arange(GRID_W)
    col_start = np.clip(cols - NA_WIN_COLS // 2, 0, GRID_W - NA_WIN_COLS)
    kc = np.arange(GRID_W)
    valid = (kc[None, :] >= col_start[:, None]) & (kc[None, :] < col_start[:, None] + NA_WIN_COLS)
    dc = kc[None, :] - cols[:, None] + NA_WIN_COLS - 1
    onehot = (dc[None] == np.arange(2 * NA_WIN_COLS - 1)[:, None, None]) & valid[None]
    f = jnp.einsum('lhad,dck->lhack', rel_bias.astype(F32), jnp.asarray(onehot, F32),
                   precision=lax.Precision.HIGHEST)
    return jnp.where(valid[None, None, None], f, NEG_BIG)


def _na_kernel(q_ref, k_ref, v_ref, f_ref, g_ref, o_ref, s_all, p_all, a_ref, t_ref):
    rows = k_ref.shape[0] // GRID_W
    blk = pl.program_id(1)

    @pl.when((pl.program_id(0) == 0) & (blk == 0))
    def _():
        for par in range(2):
            for hp in range(HEAD_PAIRS):
                for tile in range(BIAS_TILES):
                    a = 2 * tile + par
                    t_ref[par, hp, tile] = jnp.concatenate(
                        [jnp.concatenate([f_ref[0, 2 * hp + sub, a], f_ref[0, 2 * hp + sub, a + 1]], axis=1)
                         for sub in range(2)], axis=0)

    lane = lax.broadcasted_iota(jnp.int32, (GRID_W, LANES), 1)
    first_head = lane < NA_HEAD_DIM
    key_tiles = BAND_KEYS // LANES

    def row_geometry(rr):
        r = blk * NA_ROWS_PER_STEP + rr
        rs = jnp.clip(r - NA_WIN_ROWS // 2, 0, rows - NA_WIN_ROWS)
        a0 = rs - r + (NA_WIN_ROWS - 1)
        q_off = pl.multiple_of(rr * GRID_W, GRID_W)
        k_off = pl.multiple_of(rs * GRID_W, GRID_W)
        return a0 % 2, a0 // 2, q_off, k_off

    def row_scores(rr, slot):
        s_ref = s_all.at[slot]
        _, _, q_off, k_off = row_geometry(rr)
        for hp in range(HEAD_PAIRS):
            cs = slice(hp * LANES, (hp + 1) * LANES)
            qp = q_ref[pl.ds(q_off, GRID_W), cs]
            kp = k_ref[pl.ds(k_off, BAND_KEYS), cs]
            zero = jnp.zeros_like(qp)
            q2 = jnp.concatenate([jnp.where(first_head, qp, zero),
                                  jnp.where(first_head, zero, qp)], axis=0)
            s_ref[hp] = lax.dot_general(q2, kp, (((1,), (1,)), ((), ())),
                                        preferred_element_type=F32)

    def row_softmax_pv(rr, slot):
        s_ref = s_all.at[slot]
        p_ref = p_all.at[slot]
        par, j0, q_off, k_off = row_geometry(rr)
        for hp in range(HEAD_PAIRS):
            cs = slice(hp * LANES, (hp + 1) * LANES)
            inv_l = []
            for ch in range(2 * GRID_W // NA_STRIP):
                rsl = slice(ch * NA_STRIP, (ch + 1) * NA_STRIP)
                bias = jnp.concatenate([t_ref[par, hp, j0 + t, rsl, :] for t in range(key_tiles)],
                                       axis=1)
                sc = s_ref[hp, rsl, :] + bias
                m = jnp.max(sc, axis=1, keepdims=True)
                p = jnp.exp(sc - m)
                inv_l.append(1.0 / jnp.sum(p, axis=1, keepdims=True))
                p_ref[hp, rsl, :] = p.astype(BF16)
            vp = v_ref[pl.ds(k_off, BAND_KEYS), cs]
            o = jnp.dot(p_ref[hp], vp, preferred_element_type=F32)
            o = jnp.concatenate([o[ch * NA_STRIP:(ch + 1) * NA_STRIP] * inv_l[ch]
                                 for ch in range(len(inv_l))], axis=0)
            a_ref[pl.ds(q_off, GRID_W), cs] = jnp.where(first_head, o[:GRID_W], o[GRID_W:])

    def row_group(i, carry):
        for slot in range(NA_ROWS_IN_FLIGHT):
            row_scores(i * NA_ROWS_IN_FLIGHT + slot, slot)
            row_softmax_pv(i * NA_ROWS_IN_FLIGHT + slot, slot)
        return carry

    lax.fori_loop(0, NA_ROWS_PER_STEP // NA_ROWS_IN_FLIGHT, row_group, 0)

    a = a_ref[...]
    ms = jnp.mean(a * a, axis=-1, keepdims=True)
    o_ref[...] = (a * lax.rsqrt(ms + LN_EPS) * g_ref[...]).astype(o_ref.dtype)


def _na_attention(q, k, v, bias, layer, gain, batch, seq):
    n = q.shape[0]
    steps = seq // (GRID_W * NA_ROWS_PER_STEP)
    tq = NA_ROWS_PER_STEP * GRID_W
    return pl.pallas_call(
        _na_kernel,
        grid=(batch, steps),
        in_specs=[pl.BlockSpec((tq, D_NA), lambda b, i: (b * steps + i, 0)),
                  pl.BlockSpec((seq, D_NA), lambda b, i: (b, 0)),
                  pl.BlockSpec((seq, D_NA), lambda b, i: (b, 0)),
                  pl.BlockSpec((1,) + bias.shape[1:], lambda b, i: (layer, 0, 0, 0, 0)),
                  pl.BlockSpec((1, D_NA), lambda b, i: (0, 0))],
        out_specs=pl.BlockSpec((tq, D_NA), lambda b, i: (b * steps + i, 0)),
        out_shape=jax.ShapeDtypeStruct((n, D_NA), BF16),
        scratch_shapes=[pltpu.VMEM((NA_ROWS_IN_FLIGHT, HEAD_PAIRS, 2 * GRID_W, BAND_KEYS), F32),
                        pltpu.VMEM((NA_ROWS_IN_FLIGHT, HEAD_PAIRS, 2 * GRID_W, BAND_KEYS), BF16),
                        pltpu.VMEM((tq, D_NA), F32),
                        pltpu.VMEM((2, HEAD_PAIRS, BIAS_TILES, 2 * GRID_W, LANES), F32)],
        compiler_params=pltpu.CompilerParams(dimension_semantics=("arbitrary", "arbitrary"),
                                             vmem_limit_bytes=VMEM_LIMIT),
        name="na_attention",
    )(q, k, v, bias, gain)


def _spatial_gating_block(u, vs, lng_ref, lnb_ref, ws_ref, bs_ref, g_ref, seg_ref, between):
    tm = u.shape[0]
    lane = lax.broadcasted_iota(jnp.int32, (SGU_CHUNK, LANES), 1)
    first = lane < SGU_GROUP_DIM
    inv = 1.0 / SGU_GROUP_DIM
    n_tiles = D_SGU // LANES
    n_rows = tm * n_tiles

    def seg_mean(t):
        hi, lo = _split_bf16(t)
        s = jnp.dot(jnp.concatenate([hi, lo], axis=0), seg_ref[...], preferred_element_type=F32)
        return (s[:n_rows] + s[n_rows:]) * inv

    x = jnp.concatenate([vs[:, j * LANES:(j + 1) * LANES] for j in range(n_tiles)], axis=0)
    mean = seg_mean(x)
    between[0]()
    xc = x - mean
    var = seg_mean(xc * xc)
    between[1]()
    xn = xc * lax.rsqrt(var + LN_EPS)

    mixes = {}
    for c in range(tm // SGU_CHUNK):
        for j in range(n_tiles):
            cs = slice(j * LANES, (j + 1) * LANES)
            r0 = j * tm + c * SGU_CHUNK
            y = (xn[r0:r0 + SGU_CHUNK] * lng_ref[:, cs] + lnb_ref[:, cs]).astype(BF16)
            mixes[c, j] = jnp.dot(jnp.concatenate([ws_ref[2 * j], ws_ref[2 * j + 1]], axis=0), y,
                                  preferred_element_type=F32)
    between[2]()

    out = []
    for c in range(tm // SGU_CHUNK):
        rs = slice(c * SGU_CHUNK, (c + 1) * SGU_CHUNK)
        tiles = []
        for j in range(n_tiles):
            cs = slice(j * LANES, (j + 1) * LANES)
            m = mixes[c, j]
            mixed = jnp.where(first, m[:SGU_CHUNK], m[SGU_CHUNK:]) + bs_ref[:, cs]
            tiles.append(u[rs, cs] * mixed)
        so = jnp.concatenate(tiles, axis=1)
        ms = jnp.mean(so * so, axis=-1, keepdims=True)
        out.append((so * lax.rsqrt(ms + LN_EPS) * g_ref[...]).astype(BF16))
    return jnp.concatenate(out, axis=0)


def _split_bf16(a):
    hi = a.astype(BF16)
    lo = (a - hi.astype(F32)).astype(BF16)
    return hi, lo


def _route(lg, tri):
    gl = [lg[g:g + 1, :] for g in range(MOE_GROUPS)]
    gmax = functools.reduce(jnp.maximum, gl)
    gidx = jnp.full(gmax.shape, MOE_GROUPS - 1, jnp.int32)
    for g in range(MOE_GROUPS - 2, -1, -1):
        gidx = jnp.where(gl[g] == gmax, g, gidx)
    denom = functools.reduce(lambda a, b: a + b, [jnp.exp(t - gmax) for t in gl])
    gate = 1.0 / denom

    def expert_logit(i):
        rows_ = [lg[8 + MOE_EXPERTS_PER_GROUP * g + i:9 + MOE_EXPERTS_PER_GROUP * g + i, :]
                 for g in range(MOE_GROUPS)]
        sel = rows_[MOE_GROUPS - 1]
        for g in range(MOE_GROUPS - 2, -1, -1):
            sel = jnp.where(gidx == g, rows_[g], sel)
        return sel

    el = [expert_logit(i) for i in range(MOE_EXPERTS_PER_GROUP)]
    v1 = functools.reduce(jnp.maximum, el)
    i1 = jnp.full(v1.shape, MOE_EXPERTS_PER_GROUP - 1, jnp.int32)
    for i in range(MOE_EXPERTS_PER_GROUP - 2, -1, -1):
        i1 = jnp.where(el[i] == v1, i, i1)
    rest = [jnp.where(i1 == i, -jnp.inf, el[i]) for i in range(MOE_EXPERTS_PER_GROUP)]
    v2 = functools.reduce(jnp.maximum, rest)
    i2 = jnp.full(v2.shape, MOE_EXPERTS_PER_GROUP - 1, jnp.int32)
    for i in range(MOE_EXPERTS_PER_GROUP - 2, -1, -1):
        i2 = jnp.where((rest[i] == v2) & (i1 != i), i, i2)
    e2 = jnp.exp(v2 - v1)
    w1 = 1.0 / (1.0 + e2)
    w2 = e2 * w1
    within = [jnp.where(i1 == i, w1, 0.0) + jnp.where(i2 == i, w2, 0.0)
              for i in range(MOE_EXPERTS_PER_GROUP)]
    tm = MOE_BLOCK
    cw = [within[i] * gate * INV_ALPHA for i in range(MOE_EXPERTS_PER_GROUP)]

    onehot = [jnp.where(gidx == g, 1.0, 0.0) for g in range(MOE_GROUPS)]
    oh_mat = jnp.concatenate(onehot + [jnp.zeros((16 - MOE_GROUPS, tm), F32)], axis=0)
    before = jnp.dot(oh_mat.astype(BF16), tri, preferred_element_type=F32)
    rank = functools.reduce(lambda a, b: a + b,
                            [onehot[g] * before[g:g + 1, :] for g in range(MOE_GROUPS)])
    starts, pieces = [], []
    start_tok = jnp.zeros_like(rank)
    end = jnp.zeros((1, 1), F32)
    for g in range(MOE_GROUPS):
        count = jnp.sum(onehot[g], axis=1, keepdims=True)
        start_tok = start_tok + onehot[g] * end
        starts.append(end)
        end = end + MOE_ALIGN * jnp.floor((count + (MOE_ALIGN - 1)) * (1.0 / MOE_ALIGN))
        piece = functools.reduce(lambda a, b: a + b,
                                 [jnp.where(count > p, 1.0, 0.0) for p in MOE_PIECES[:-1]])
        pieces.append(jnp.where(count > 0.0, piece, -1.0))
    dest = rank + start_tok
    rout_t = jnp.concatenate([dest] + cw + [gidx.astype(F32), jnp.zeros((2, tm), F32)], axis=0)

    c = lax.broadcasted_iota(jnp.int32, (8, LANES), 1)
    tab = jnp.zeros((8, LANES), F32)
    for g in range(MOE_GROUPS):
        tab = jnp.where(c == TAB_START + g, starts[g], tab)
        tab = jnp.where(c == TAB_PIECE + g, pieces[g], tab)
    return rout_t, tab.astype(jnp.int32)


def _out_proj_kernel(a_ref, s_ref, x_ref, w_ref, g_ref, b_ref, rw_ref, rb_ref, tri_ref,
                     x1_ref, rt_ref, r_ref, tab_ref, wb_ref):
    @pl.when(pl.program_id(0) == 0)
    def _():
        wb_ref[...] = w_ref[0].astype(BF16)

    mixes = []
    for blk in range(OUT_BLOCKS_PER_STEP):
        rs = slice(blk * MOE_BLOCK, (blk + 1) * MOE_BLOCK)
        mix = jnp.dot(a_ref[rs, :], wb_ref[:D_NA, :], preferred_element_type=F32)
        mixes.append(mix + jnp.dot(s_ref[rs, :], wb_ref[D_NA:, :], preferred_element_type=F32))
    logits = []
    for blk in range(OUT_BLOCKS_PER_STEP):
        rs = slice(blk * MOE_BLOCK, (blk + 1) * MOE_BLOCK)
        x1 = _layer_norm_rows(x_ref[rs, :] + mixes[blk], g_ref[...], b_ref[...], LN_EPS_RESIDUAL)
        x1_ref[rs, :] = x1
        xh, xl = _split_bf16(x1)
        logits.append(jnp.dot(xh, rw_ref[0], preferred_element_type=F32)
                      + jnp.dot(xl, rw_ref[0], preferred_element_type=F32))
    for blk in range(OUT_BLOCKS_PER_STEP):
        rs = slice(blk * MOE_BLOCK, (blk + 1) * MOE_BLOCK)
        lg_t = logits[blk].T
        lg = lg_t[:ROUTER_ROWS] + lg_t[ROUTER_ROWS:2 * ROUTER_ROWS] + rb_ref[0, :, 0:1]
        rout_t, tab = _route(lg, tri_ref[...])
        rt_ref[:, rs] = rout_t
        r_ref[rs, :] = jnp.concatenate([rout_t, jnp.zeros((LANES - 8, MOE_BLOCK), F32)], axis=0).T
        tab_ref[blk] = tab


def _out_proj(a, s, x, w_out, layer, g1, b1, rw, rb, tri):
    n = x.shape[0]
    tok = lambda i: (i, 0)
    row = lambda i: (0, 0)
    tm = OUT_BLOCKS_PER_STEP * MOE_BLOCK
    return pl.pallas_call(
        _out_proj_kernel,
        grid=(n // tm,),
        in_specs=[pl.BlockSpec((tm, D_NA), tok),
                  pl.BlockSpec((tm, D_SGU), tok),
                  pl.BlockSpec((tm, D_MODEL), tok),
                  pl.BlockSpec((1, D_MODEL, D_MODEL), lambda i: (layer, 0, 0),
                               pipeline_mode=pl.Buffered(1)),
                  pl.BlockSpec((1, D_MODEL), row),
                  pl.BlockSpec((1, D_MODEL), row),
                  pl.BlockSpec((1, D_MODEL, LANES), lambda i: (layer, 0, 0)),
                  pl.BlockSpec((1, ROUTER_ROWS, LANES), lambda i: (layer, 0, 0)),
                  pl.BlockSpec((MOE_BLOCK, MOE_BLOCK), row)],
        out_specs=[pl.BlockSpec((tm, D_MODEL), tok),
                   pl.BlockSpec((8, tm), lambda i: (0, i)),
                   pl.BlockSpec((tm, LANES), tok),
                   pl.BlockSpec((OUT_BLOCKS_PER_STEP, 8, LANES), lambda i: (i, 0, 0))],
        out_shape=[jax.ShapeDtypeStruct((n, D_MODEL), F32),
                   jax.ShapeDtypeStruct((8, n), F32),
                   jax.ShapeDtypeStruct((n, LANES), F32),
                   jax.ShapeDtypeStruct((n // MOE_BLOCK, 8, LANES), jnp.int32)],
        scratch_shapes=[pltpu.VMEM((D_MODEL, D_MODEL), BF16)],
        compiler_params=pltpu.CompilerParams(dimension_semantics=("arbitrary",),
                                             vmem_limit_bytes=VMEM_LIMIT),
        name="out_proj_ln_router",
    )(a, s, x, w_out, g1, b1, rw, rb, tri)


def _moe_kernel(tab_ref, x_ref, rt_ref, r_ref, wg_ref, wu_ref, wd_ref, g_ref, b_ref, o_ref,
                xs_ref, ys_ref, cw_ref):
    blk = pl.program_id(0)
    x1 = x_ref[...]
    dest_row = rt_ref[0:1, :].astype(jnp.int32)
    rows_i = lax.broadcasted_iota(jnp.int32, (MOE_SORTED_ROWS, MOE_BLOCK), 0)
    p_in = jnp.where(rows_i == dest_row, 1.0, 0.0).astype(BF16)
    xs_ref[...] = jnp.dot(p_in, x1.astype(BF16), preferred_element_type=F32).astype(BF16)
    r_hi, r_lo = _split_bf16(r_ref[...])
    cw2 = jnp.dot(p_in, jnp.concatenate([r_hi, r_lo], axis=1), preferred_element_type=F32)
    cw_ref[...] = cw2[:, :LANES] + cw2[:, LANES:]
    ys_ref[...] = jnp.zeros_like(ys_ref)
    dest_col = r_ref[:, 0:1].astype(jnp.int32)
    cols_i = lax.broadcasted_iota(jnp.int32, (MOE_BLOCK, MOE_SORTED_ROWS), 1)
    p_out = jnp.where(cols_i == dest_col, 1.0, 0.0).astype(BF16)

    def run_experts(g, start, n_rows):
        off = pl.multiple_of(start, MOE_ALIGN)
        xs = xs_ref[pl.ds(off, n_rows), :]
        cw = cw_ref[pl.ds(off, n_rows), :]
        hg = jnp.dot(xs, wg_ref[g], preferred_element_type=F32)
        hu = jnp.dot(xs, wu_ref[g], preferred_element_type=F32)
        scale = jnp.concatenate([jnp.broadcast_to(cw[:, 1 + i:2 + i], (n_rows, D_EXPERT))
                                 for i in range(MOE_EXPERTS_PER_GROUP)], axis=1)
        act = hg * (1.0 / (1.0 + jnp.exp(-hg))) * hu * scale
        y = jnp.dot(act.astype(BF16), wd_ref[g], preferred_element_type=F32)
        ys_ref[pl.ds(off, n_rows), :] = y.astype(BF16)

    def group(g, carry):
        start = tab_ref[blk, TAB_START + g]
        piece = tab_ref[blk, TAB_PIECE + g]
        for k, n_rows in enumerate(MOE_PIECES):
            @pl.when(piece == k)
            def _(n_rows=n_rows):
                run_experts(g, start, n_rows)
        return carry

    lax.fori_loop(0, MOE_GROUPS, group, 0)

    part = MOE_BLOCK // MOE_OUT_PARTS
    ys = ys_ref[...]
    y_parts = [jnp.dot(p_out[i * part:(i + 1) * part], ys, preferred_element_type=F32)
               for i in range(MOE_OUT_PARTS)]
    for i in range(MOE_OUT_PARTS):
        rs = slice(i * part, (i + 1) * part)
        o_ref[rs, :] = _layer_norm_rows(x1[rs] + y_parts[i], g_ref[...], b_ref[...], LN_EPS_RESIDUAL)


def _moe(tab, x1, rout_t, rout, wg, wu, wd, g2, b2):
    n = x1.shape[0]
    tok = lambda i, t: (i, 0)
    row = lambda i, t: (0, 0)
    whole = lambda i, t: (0, 0, 0)
    resident = pl.Buffered(1)
    return pl.pallas_call(
        _moe_kernel,
        grid_spec=pltpu.PrefetchScalarGridSpec(
            num_scalar_prefetch=1,
            grid=(n // MOE_BLOCK,),
            in_specs=[pl.BlockSpec((MOE_BLOCK, D_MODEL), tok),
                      pl.BlockSpec((8, MOE_BLOCK), lambda i, t: (0, i)),
                      pl.BlockSpec((MOE_BLOCK, LANES), tok),
                      pl.BlockSpec((MOE_GROUPS, D_MODEL, D_GROUP), whole, pipeline_mode=resident),
                      pl.BlockSpec((MOE_GROUPS, D_MODEL, D_GROUP), whole, pipeline_mode=resident),
                      pl.BlockSpec((MOE_GROUPS, D_GROUP, D_MODEL), whole, pipeline_mode=resident),
                      pl.BlockSpec((1, D_MODEL), row),
                      pl.BlockSpec((1, D_MODEL), row)],
            out_specs=pl.BlockSpec((MOE_BLOCK, D_MODEL), tok),
            scratch_shapes=[pltpu.VMEM((MOE_SORTED_ROWS, D_MODEL), BF16),
                            pltpu.VMEM((MOE_SORTED_ROWS, D_MODEL), BF16),
                            pltpu.VMEM((MOE_SORTED_ROWS, LANES), F32)]),
        out_shape=jax.ShapeDtypeStruct((n, D_MODEL), F32),
        compiler_params=pltpu.CompilerParams(dimension_semantics=("parallel",),
                                             vmem_limit_bytes=VMEM_LIMIT),
        name="moe_experts_ln",
    )(tab, x1, rout_t, rout, wg, wu, wd, g2, b2)


def _router_weights(w_rg, b_rg, w_re, b_re):
    depth = w_rg.shape[0]
    pad = lambda a, n: jnp.zeros(a.shape[:-1] + (n,), F32)
    wt = jnp.concatenate([w_rg.astype(F32), pad(w_rg, 8 - MOE_GROUPS), w_re.astype(F32),
                          pad(w_re, ROUTER_ROWS - 8 - MOE_EXPERTS)], axis=-1)
    bt = jnp.concatenate([b_rg.astype(F32), pad(b_rg, 8 - MOE_GROUPS), b_re.astype(F32),
                          pad(b_re, ROUTER_ROWS - 8 - MOE_EXPERTS)], axis=-1)
    hi, lo = _split_bf16(wt)
    rw = jnp.concatenate([hi, lo, jnp.zeros((depth, D_MODEL, LANES - 2 * ROUTER_ROWS), BF16)], axis=-1)
    return rw, jnp.broadcast_to(bt[:, :, None], (depth, ROUTER_ROWS, LANES))


def kernel(x, w_in, w_out, na_rel_bias, sgu_ln_g, sgu_ln_b, sgu_w, sgu_b, mix_norm_g, ln1_g, ln1_b, router_group_w, router_group_b, router_expert_w, router_expert_b, expert_w_gate, expert_w_up, expert_w_down, ln2_g, ln2_b):
    batch, seq, d = x.shape
    n = batch * seq
    xf = x.reshape(n, d).astype(F32)
    row = lambda a: a.astype(F32).reshape(1, -1)
    tri = jnp.asarray(np.triu(np.ones((MOE_BLOCK, MOE_BLOCK), np.float32), k=1), BF16)
    bias = _na_bias_compact(na_rel_bias)
    rw, rb = _router_weights(router_group_w, router_group_b, router_expert_w, router_expert_b)
    for l in range(DEPTH):
        bs_full = jnp.repeat(sgu_b[l].astype(F32).T, SGU_GROUP_DIM, axis=1)
        q, k, v, s, wg, wu, wd = _in_proj(xf, w_in.astype(F32), expert_w_gate.astype(F32),
                                          expert_w_up.astype(F32), expert_w_down.astype(F32), l,
                                          row(sgu_ln_g[l]), row(sgu_ln_b[l]), sgu_w[l].astype(BF16),
                                          bs_full, row(mix_norm_g[l, D_NA:]) * INV_ALPHA)
        a = _na_attention(q, k, v, bias, l, row(mix_norm_g[l, :D_NA]) * INV_ALPHA, batch, seq)
        x1, rout_t, rout, tab = _out_proj(a, s, xf, w_out.astype(F32), l, row(ln1_g[l]),
                                          row(ln1_b[l]), rw, rb, tri)
        xf = _moe(tab[:, 0, :8], x1, rout_t, rout, wg, wu, wd, row(ln2_g[l]), row(ln2_b[l]))
    return xf.reshape(batch, seq, d).astype(x.dtype)
```

```python
import functools
import math

import numpy as np
import jax
import jax.numpy as jnp
from jax import lax
from jax.experimental import pallas as pl
from jax.experimental.pallas import tpu as pltpu

F32 = jnp.float32
BF16 = jnp.bfloat16

D_MODEL = 1024
DEPTH = 4
GRID_W = 64
NA_HEADS = 8
NA_HEAD_DIM = 64
NA_WIN_ROWS = 8
NA_WIN_COLS = 16
D_NA = NA_HEADS * NA_HEAD_DIM
SGU_GROUPS = 8
SGU_GROUP_DIM = 64
SGU_CHUNK = 128
D_SGU = SGU_GROUPS * SGU_GROUP_DIM
D_IN = 3 * D_NA + 2 * D_SGU
MOE_GROUPS = 4
MOE_EXPERTS_PER_GROUP = 4
MOE_EXPERTS = MOE_GROUPS * MOE_EXPERTS_PER_GROUP
D_EXPERT = 256
D_GROUP = MOE_EXPERTS_PER_GROUP * D_EXPERT
DEEPNORM_ALPHA = (2 * DEPTH) ** 0.25
LN_EPS = 1e-5
INV_ALPHA = 1.0 / DEEPNORM_ALPHA
LN_EPS_RESIDUAL = LN_EPS / DEEPNORM_ALPHA ** 2

LANES = 128
HEAD_PAIRS = NA_HEADS // 2
NEG_BIG = -1e30
BAND_KEYS = NA_WIN_ROWS * GRID_W
BIAS_TILES = 7
ROUTER_ROWS = 32

TM_PROJ = 512
NA_ROWS_PER_STEP = 16
NA_ROWS_IN_FLIGHT = 8
NA_STRIP = 16
MOE_BLOCK = 512
MOE_ALIGN = 16
MOE_PIECES = (64, 128, 144, 160, 192, 256, 384, MOE_BLOCK)
MOE_OUT_PARTS = 2
MOE_SORTED_ROWS = 704
OUT_BLOCKS_PER_STEP = 2
TAB_START = 0
TAB_PIECE = MOE_GROUPS
VMEM_LIMIT = 56 * 1024 * 1024


def _gelu_tanh(x):
    c = math.sqrt(2.0 / math.pi)
    return x * (0.5 * (1.0 + jnp.tanh(c * (x + 0.044715 * (x * x * x)))))


def _layer_norm_rows(h, g, b, eps):
    mu = jnp.mean(h, axis=-1, keepdims=True)
    hc = h - mu
    var = jnp.mean(hc * hc, axis=-1, keepdims=True)
    return hc * lax.rsqrt(var + eps) * g + b


def _in_proj_kernel(x_ref, w_ref, wg_ref, wu_ref, wd_ref, lng_ref, lnb_ref, ws_ref, bs_ref, gs_ref,
                    seg_ref, q_ref, k_ref, v_ref, s_ref, wg_out, wu_out, wd_out, wb_ref):
    @pl.when(pl.program_id(0) == 0)
    def _():
        wb_ref[...] = w_ref[0].astype(BF16)

    wg_out[0] = wg_ref[0].astype(BF16)
    wu_out[0] = wu_ref[0].astype(BF16)
    wd_out[...] = wd_ref[0].astype(BF16)

    xb = x_ref[...].astype(BF16)

    def mm(j):
        return jnp.dot(xb, wb_ref[:, j * D_NA:(j + 1) * D_NA], preferred_element_type=F32)

    hv, hu = mm(4), mm(3)

    def do_q():
        q_ref[...] = (mm(0) * (NA_HEAD_DIM ** -0.5)).astype(BF16)

    def do_k():
        k_ref[...] = mm(1).astype(BF16)

    def do_v():
        v_ref[...] = mm(2).astype(BF16)

    s_ref[...] = _spatial_gating_block(_gelu_tanh(hu), _gelu_tanh(hv), lng_ref, lnb_ref, ws_ref,
                                       bs_ref, gs_ref, seg_ref, (do_q, do_k, do_v))


def _in_proj(x, w_in, w_gate, w_up, w_down, layer, ln_g, ln_b, w_s_bf16, bs_full, gain):
    n = x.shape[0]
    steps = n // TM_PROJ
    gu_rows = MOE_EXPERTS * D_MODEL // steps
    parts = D_MODEL // gu_rows
    dn_rows = MOE_EXPERTS * D_EXPERT // steps
    assert gu_rows * steps == MOE_EXPERTS * D_MODEL and parts * gu_rows == D_MODEL
    assert dn_rows * steps == MOE_EXPERTS * D_EXPERT and TM_PROJ % SGU_CHUNK == 0
    same_group = np.arange(LANES)[:, None] // SGU_GROUP_DIM == np.arange(LANES)[None, :] // SGU_GROUP_DIM
    seg_ones = jnp.asarray(same_group, BF16)
    tok = lambda i: (i, 0)
    row = lambda i: (0, 0)
    slab = lambda i: (layer, i, 0)
    grouped = lambda i: (i // (parts * MOE_EXPERTS_PER_GROUP), i % parts,
                         (i // parts) % MOE_EXPERTS_PER_GROUP)
    depth = w_gate.shape[0]
    outs = pl.pallas_call(
        _in_proj_kernel,
        grid=(steps,),
        in_specs=[pl.BlockSpec((TM_PROJ, D_MODEL), tok),
                  pl.BlockSpec((1, D_MODEL, D_IN), lambda i: (layer, 0, 0),
                               pipeline_mode=pl.Buffered(1)),
                  pl.BlockSpec((1, gu_rows, D_EXPERT), slab),
                  pl.BlockSpec((1, gu_rows, D_EXPERT), slab),
                  pl.BlockSpec((1, dn_rows, D_MODEL), slab),
                  pl.BlockSpec((1, D_SGU), row),
                  pl.BlockSpec((1, D_SGU), row),
                  pl.BlockSpec((SGU_GROUPS, SGU_CHUNK, SGU_CHUNK), lambda i: (0, 0, 0)),
                  pl.BlockSpec((SGU_CHUNK, D_SGU), row),
                  pl.BlockSpec((1, D_SGU), row),
                  pl.BlockSpec((LANES, LANES), row)],
        out_specs=[pl.BlockSpec((TM_PROJ, D_NA), tok)] * 4
                  + [pl.BlockSpec((1, gu_rows, D_EXPERT), grouped)] * 2
                  + [pl.BlockSpec((dn_rows, D_MODEL), tok)],
        out_shape=[jax.ShapeDtypeStruct((n, D_NA), BF16)] * 4
                  + [jax.ShapeDtypeStruct((MOE_GROUPS, D_MODEL, D_GROUP), BF16)] * 2
                  + [jax.ShapeDtypeStruct((MOE_GROUPS * D_GROUP, D_MODEL), BF16)],
        scratch_shapes=[pltpu.VMEM((D_MODEL, D_IN), BF16)],
        compiler_params=pltpu.CompilerParams(dimension_semantics=("arbitrary",),
                                             vmem_limit_bytes=VMEM_LIMIT),
        name="in_proj_sgu",
    )(x, w_in,
      w_gate.reshape(depth, MOE_EXPERTS * D_MODEL, D_EXPERT),
      w_up.reshape(depth, MOE_EXPERTS * D_MODEL, D_EXPERT),
      w_down.reshape(depth, MOE_EXPERTS * D_EXPERT, D_MODEL),
      ln_g, ln_b, w_s_bf16, bs_full, gain, seg_ones)
    q, k, v, s, wg, wu, wd = outs
    return q, k, v, s, wg, wu, wd.reshape(MOE_GROUPS, D_GROUP, D_MODEL)


def _na_bias_compact(rel_bias):
    cols = np.arange(GRID_W)
    col_start = np.clip(cols - NA_WIN_COLS // 2, 0, GRID_W - NA_WIN_COLS)
    kc = np.arange(GRID_W)
    valid = (kc[None, :] >= col_start[:, None]) & (kc[None, :] < col_start[:, None] + NA_WIN_COLS)
    dc = kc[None, :] - cols[:, None] + NA_WIN_COLS - 1
    onehot = (dc[None] == np.arange(2 * NA_WIN_COLS - 1)[:, None, None]) & valid[None]
    f = jnp.einsum('lhad,dck->lhack', rel_bias.astype(F32), jnp.asarray(onehot, F32),
                   precision=lax.Precision.HIGHEST)
    return jnp.where(valid[None, None, None], f, NEG_BIG)


def _na_kernel(q_ref, k_ref, v_ref, f_ref, g_ref, o_ref, s_all, p_all, a_ref, t_ref):
    rows = k_ref.shape[0] // GRID_W
    blk = pl.program_id(1)

    @pl.when((pl.program_id(0) == 0) & (blk == 0))
    def _():
        for par in range(2):
            for hp in range(HEAD_PAIRS):
                for tile in range(BIAS_TILES):
                    a = 2 * tile + par
                    t_ref[par, hp, tile] = jnp.concatenate(
                        [jnp.concatenate([f_ref[0, 2 * hp + sub, a], f_ref[0, 2 * hp + sub, a + 1]], axis=1)
                         for sub in range(2)], axis=0)

    lane = lax.broadcasted_iota(jnp.int32, (GRID_W, LANES), 1)
    first_head = lane < NA_HEAD_DIM
    key_tiles = BAND_KEYS // LANES

    def row_geometry(rr):
        r = blk * NA_ROWS_PER_STEP + rr
        rs = jnp.clip(r - NA_WIN_ROWS // 2, 0, rows - NA_WIN_ROWS)
        a0 = rs - r + (NA_WIN_ROWS - 1)
        q_off = pl.multiple_of(rr * GRID_W, GRID_W)
        k_off = pl.multiple_of(rs * GRID_W, GRID_W)
        return a0 % 2, a0 // 2, q_off, k_off

    def row_scores(rr, slot):
        s_ref = s_all.at[slot]
        _, _, q_off, k_off = row_geometry(rr)
        for hp in range(HEAD_PAIRS):
            cs = slice(hp * LANES, (hp + 1) * LANES)
            qp = q_ref[pl.ds(q_off, GRID_W), cs]
            kp = k_ref[pl.ds(k_off, BAND_KEYS), cs]
            zero = jnp.zeros_like(qp)
            q2 = jnp.concatenate([jnp.where(first_head, qp, zero),
                                  jnp.where(first_head, zero, qp)], axis=0)
            s_ref[hp] = lax.dot_general(q2, kp, (((1,), (1,)), ((), ())),
                                        preferred_element_type=F32)

    def row_softmax_pv(rr, slot):
        s_ref = s_all.at[slot]
        p_ref = p_all.at[slot]
        par, j0, q_off, k_off = row_geometry(rr)
        for hp in range(HEAD_PAIRS):
            cs = slice(hp * LANES, (hp + 1) * LANES)
            inv_l = []
            for ch in range(2 * GRID_W // NA_STRIP):
                rsl = slice(ch * NA_STRIP, (ch + 1) * NA_STRIP)
                bias = jnp.concatenate([t_ref[par, hp, j0 + t, rsl, :] for t in range(key_tiles)],
                                       axis=1)
                sc = s_ref[hp, rsl, :] + bias
                m = jnp.max(sc, axis=1, keepdims=True)
                p = jnp.exp(sc - m)
                inv_l.append(1.0 / jnp.sum(p, axis=1, keepdims=True))
                p_ref[hp, rsl, :] = p.astype(BF16)
            vp = v_ref[pl.ds(k_off, BAND_KEYS), cs]
            o = jnp.dot(p_ref[hp], vp, preferred_element_type=F32)
            o = jnp.concatenate([o[ch * NA_STRIP:(ch + 1) * NA_STRIP] * inv_l[ch]
                                 for ch in range(len(inv_l))], axis=0)
            a_ref[pl.ds(q_off, GRID_W), cs] = jnp.where(first_head, o[:GRID_W], o[GRID_W:])

    def row_group(i, carry):
        for slot in range(NA_ROWS_IN_FLIGHT):
            row_scores(i * NA_ROWS_IN_FLIGHT + slot, slot)
            row_softmax_pv(i * NA_ROWS_IN_FLIGHT + slot, slot)
        return carry

    lax.fori_loop(0, NA_ROWS_PER_STEP // NA_ROWS_IN_FLIGHT, row_group, 0)

    a = a_ref[...]
    ms = jnp.mean(a * a, axis=-1, keepdims=True)
    o_ref[...] = (a * lax.rsqrt(ms + LN_EPS) * g_ref[...]).astype(o_ref.dtype)


def _na_attention(q, k, v, bias, layer, gain, batch, seq):
    n = q.shape[0]
    steps = seq // (GRID_W * NA_ROWS_PER_STEP)
    tq = NA_ROWS_PER_STEP * GRID_W
    return pl.pallas_call(
        _na_kernel,
        grid=(batch, steps),
        in_specs=[pl.BlockSpec((tq, D_NA), lambda b, i: (b * steps + i, 0)),
                  pl.BlockSpec((seq, D_NA), lambda b, i: (b, 0)),
                  pl.BlockSpec((seq, D_NA), lambda b, i: (b, 0)),
                  pl.BlockSpec((1,) + bias.shape[1:], lambda b, i: (layer, 0, 0, 0, 0)),
                  pl.BlockSpec((1, D_NA), lambda b, i: (0, 0))],
        out_specs=pl.BlockSpec((tq, D_NA), lambda b, i: (b * steps + i, 0)),
        out_shape=jax.ShapeDtypeStruct((n, D_NA), BF16),
        scratch_shapes=[pltpu.VMEM((NA_ROWS_IN_FLIGHT, HEAD_PAIRS, 2 * GRID_W, BAND_KEYS), F32),
                        pltpu.VMEM((NA_ROWS_IN_FLIGHT, HEAD_PAIRS, 2 * GRID_W, BAND_KEYS), BF16),
                        pltpu.VMEM((tq, D_NA), F32),
                        pltpu.VMEM((2, HEAD_PAIRS, BIAS_TILES, 2 * GRID_W, LANES), F32)],
        compiler_params=pltpu.CompilerParams(dimension_semantics=("arbitrary", "arbitrary"),
                                             vmem_limit_bytes=VMEM_LIMIT),
        name="na_attention",
    )(q, k, v, bias, gain)


def _spatial_gating_block(u, vs, lng_ref, lnb_ref, ws_ref, bs_ref, g_ref, seg_ref, between):
    tm = u.shape[0]
    lane = lax.broadcasted_iota(jnp.int32, (SGU_CHUNK, LANES), 1)
    first = lane < SGU_GROUP_DIM
    inv = 1.0 / SGU_GROUP_DIM
    n_tiles = D_SGU // LANES
    n_rows = tm * n_tiles

    def seg_mean(t):
        hi, lo = _split_bf16(t)
        s = jnp.dot(jnp.concatenate([hi, lo], axis=0), seg_ref[...], preferred_element_type=F32)
        return (s[:n_rows] + s[n_rows:]) * inv

    x = jnp.concatenate([vs[:, j * LANES:(j + 1) * LANES] for j in range(n_tiles)], axis=0)
    mean = seg_mean(x)
    between[0]()
    xc = x - mean
    var = seg_mean(xc * xc)
    between[1]()
    xn = xc * lax.rsqrt(var + LN_EPS)

    mixes = {}
    for c in range(tm // SGU_CHUNK):
        for j in range(n_tiles):
            cs = slice(j * LANES, (j + 1) * LANES)
            r0 = j * tm + c * SGU_CHUNK
            y = (xn[r0:r0 + SGU_CHUNK] * lng_ref[:, cs] + lnb_ref[:, cs]).astype(BF16)
            mixes[c, j] = jnp.dot(jnp.concatenate([ws_ref[2 * j], ws_ref[2 * j + 1]], axis=0), y,
                                  preferred_element_type=F32)
    between[2]()

    out = []
    for c in range(tm // SGU_CHUNK):
        rs = slice(c * SGU_CHUNK, (c + 1) * SGU_CHUNK)
        tiles = []
        for j in range(n_tiles):
            cs = slice(j * LANES, (j + 1) * LANES)
            m = mixes[c, j]
            mixed = jnp.where(first, m[:SGU_CHUNK], m[SGU_CHUNK:]) + bs_ref[:, cs]
            tiles.append(u[rs, cs] * mixed)
        so = jnp.concatenate(tiles, axis=1)
        ms = jnp.mean(so * so, axis=-1, keepdims=True)
        out.append((so * lax.rsqrt(ms + LN_EPS) * g_ref[...]).astype(BF16))
    return jnp.concatenate(out, axis=0)


def _split_bf16(a):
    hi = a.astype(BF16)
    lo = (a - hi.astype(F32)).astype(BF16)
    return hi, lo


def _route(lg, tri):
    gl = [lg[g:g + 1, :] for g in range(MOE_GROUPS)]
    gmax = functools.reduce(jnp.maximum, gl)
    gidx = jnp.full(gmax.shape, MOE_GROUPS - 1, jnp.int32)
    for g in range(MOE_GROUPS - 2, -1, -1):
        gidx = jnp.where(gl[g] == gmax, g, gidx)
    denom = functools.reduce(lambda a, b: a + b, [jnp.exp(t - gmax) for t in gl])
    gate = 1.0 / denom

    def expert_logit(i):
        rows_ = [lg[8 + MOE_EXPERTS_PER_GROUP * g + i:9 + MOE_EXPERTS_PER_GROUP * g + i, :]
                 for g in range(MOE_GROUPS)]
        sel = rows_[MOE_GROUPS - 1]
        for g in range(MOE_GROUPS - 2, -1, -1):
            sel = jnp.where(gidx == g, rows_[g], sel)
        return sel

    el = [expert_logit(i) for i in range(MOE_EXPERTS_PER_GROUP)]
    v1 = functools.reduce(jnp.maximum, el)
    i1 = jnp.full(v1.shape, MOE_EXPERTS_PER_GROUP - 1, jnp.int32)
    for i in range(MOE_EXPERTS_PER_GROUP - 2, -1, -1):
        i1 = jnp.where(el[i] == v1, i, i1)
    rest = [jnp.where(i1 == i, -jnp.inf, el[i]) for i in range(MOE_EXPERTS_PER_GROUP)]
    v2 = functools.reduce(jnp.maximum, rest)
    i2 = jnp.full(v2.shape, MOE_EXPERTS_PER_GROUP - 1, jnp.int32)
    for i in range(MOE_EXPERTS_PER_GROUP - 2, -1, -1):
        i2 = jnp.where((rest[i] == v2) & (i1 != i), i, i2)
    e2 = jnp.exp(v2 - v1)
    w1 = 1.0 / (1.0 + e2)
    w2 = e2 * w1
    within = [jnp.where(i1 == i, w1, 0.0) + jnp.where(i2 == i, w2, 0.0)
              for i in range(MOE_EXPERTS_PER_GROUP)]
    tm = MOE_BLOCK
    cw = [within[i] * gate * INV_ALPHA for i in range(MOE_EXPERTS_PER_GROUP)]

    onehot = [jnp.where(gidx == g, 1.0, 0.0) for g in range(MOE_GROUPS)]
    oh_mat = jnp.concatenate(onehot + [jnp.zeros((16 - MOE_GROUPS, tm), F32)], axis=0)
    before = jnp.dot(oh_mat.astype(BF16), tri, preferred_element_type=F32)
    rank = functools.reduce(lambda a, b: a + b,
                            [onehot[g] * before[g:g + 1, :] for g in range(MOE_GROUPS)])
    starts, pieces = [], []
    start_tok = jnp.zeros_like(rank)
    end = jnp.zeros((1, 1), F32)
    for g in range(MOE_GROUPS):
        count = jnp.sum(onehot[g], axis=1, keepdims=True)
        start_tok = start_tok + onehot[g] * end
        starts.append(end)
        end = end + MOE_ALIGN * jnp.floor((count + (MOE_ALIGN - 1)) * (1.0 / MOE_ALIGN))
        piece = functools.reduce(lambda a, b: a + b,
                                 [jnp.where(count > p, 1.0, 0.0) for p in MOE_PIECES[:-1]])
        pieces.append(jnp.where(count > 0.0, piece, -1.0))
    dest = rank + start_tok
    rout_t = jnp.concatenate([dest] + cw + [gidx.astype(F32), jnp.zeros((2, tm), F32)], axis=0)

    c = lax.broadcasted_iota(jnp.int32, (8, LANES), 1)
    tab = jnp.zeros((8, LANES), F32)
    for g in range(MOE_GROUPS):
        tab = jnp.where(c == TAB_START + g, starts[g], tab)
        tab = jnp.where(c == TAB_PIECE + g, pieces[g], tab)
    return rout_t, tab.astype(jnp.int32)


def _out_proj_kernel(a_ref, s_ref, x_ref, w_ref, g_ref, b_ref, rw_ref, rb_ref, tri_ref,
                     x1_ref, rt_ref, r_ref, tab_ref, wb_ref):
    @pl.when(pl.program_id(0) == 0)
    def _():
        wb_ref[...] = w_ref[0].astype(BF16)

    mixes = []
    for blk in range(OUT_BLOCKS_PER_STEP):
        rs = slice(blk * MOE_BLOCK, (blk + 1) * MOE_BLOCK)
        mix = jnp.dot(a_ref[rs, :], wb_ref[:D_NA, :], preferred_element_type=F32)
        mixes.append(mix + jnp.dot(s_ref[rs, :], wb_ref[D_NA:, :], preferred_element_type=F32))
    logits = []
    for blk in range(OUT_BLOCKS_PER_STEP):
        rs = slice(blk * MOE_BLOCK, (blk + 1) * MOE_BLOCK)
        x1 = _layer_norm_rows(x_ref[rs, :] + mixes[blk], g_ref[...], b_ref[...], LN_EPS_RESIDUAL)
        x1_ref[rs, :] = x1
        xh, xl = _split_bf16(x1)
        logits.append(jnp.dot(xh, rw_ref[0], preferred_element_type=F32)
                      + jnp.dot(xl, rw_ref[0], preferred_element_type=F32))
    for blk in range(OUT_BLOCKS_PER_STEP):
        rs = slice(blk * MOE_BLOCK, (blk + 1) * MOE_BLOCK)
        lg_t = logits[blk].T
        lg = lg_t[:ROUTER_ROWS] + lg_t[ROUTER_ROWS:2 * ROUTER_ROWS] + rb_ref[0, :, 0:1]
        rout_t, tab = _route(lg, tri_ref[...])
        rt_ref[:, rs] = rout_t
        r_ref[rs, :] = jnp.concatenate([rout_t, jnp.zeros((LANES - 8, MOE_BLOCK), F32)], axis=0).T
        tab_ref[blk] = tab


def _out_proj(a, s, x, w_out, layer, g1, b1, rw, rb, tri):
    n = x.shape[0]
    tok = lambda i: (i, 0)
    row = lambda i: (0, 0)
    tm = OUT_BLOCKS_PER_STEP * MOE_BLOCK
    return pl.pallas_call(
        _out_proj_kernel,
        grid=(n // tm,),
        in_specs=[pl.BlockSpec((tm, D_NA), tok),
                  pl.BlockSpec((tm, D_SGU), tok),
                  pl.BlockSpec((tm, D_MODEL), tok),
                  pl.BlockSpec((1, D_MODEL, D_MODEL), lambda i: (layer, 0, 0),
                               pipeline_mode=pl.Buffered(1)),
                  pl.BlockSpec((1, D_MODEL), row),
                  pl.BlockSpec((1, D_MODEL), row),
                  pl.BlockSpec((1, D_MODEL, LANES), lambda i: (layer, 0, 0)),
                  pl.BlockSpec((1, ROUTER_ROWS, LANES), lambda i: (layer, 0, 0)),
                  pl.BlockSpec((MOE_BLOCK, MOE_BLOCK), row)],
        out_specs=[pl.BlockSpec((tm, D_MODEL), tok),
                   pl.BlockSpec((8, tm), lambda i: (0, i)),
                   pl.BlockSpec((tm, LANES), tok),
                   pl.BlockSpec((OUT_BLOCKS_PER_STEP, 8, LANES), lambda i: (i, 0, 0))],
        out_shape=[jax.ShapeDtypeStruct((n, D_MODEL), F32),
                   jax.ShapeDtypeStruct((8, n), F32),
                   jax.ShapeDtypeStruct((n, LANES), F32),
                   jax.ShapeDtypeStruct((n // MOE_BLOCK, 8, LANES), jnp.int32)],
        scratch_shapes=[pltpu.VMEM((D_MODEL, D_MODEL), BF16)],
        compiler_params=pltpu.CompilerParams(dimension_semantics=("arbitrary",),
                                             vmem_limit_bytes=VMEM_LIMIT),
        name="out_proj_ln_router",
    )(a, s, x, w_out, g1, b1, rw, rb, tri)


def _moe_kernel(tab_ref, x_ref, rt_ref, r_ref, wg_ref, wu_ref, wd_ref, g_ref, b_ref, o_ref,
                xs_ref, ys_ref, cw_ref):
    blk = pl.program_id(0)
    x1 = x_ref[...]
    dest_row = rt_ref[0:1, :].astype(jnp.int32)
    rows_i = lax.broadcasted_iota(jnp.int32, (MOE_SORTED_ROWS, MOE_BLOCK), 0)
    p_in = jnp.where(rows_i == dest_row, 1.0, 0.0).astype(BF16)
    xs_ref[...] = jnp.dot(p_in, x1.astype(BF16), preferred_element_type=F32).astype(BF16)
    r_hi, r_lo = _split_bf16(r_ref[...])
    cw2 = jnp.dot(p_in, jnp.concatenate([r_hi, r_lo], axis=1), preferred_element_type=F32)
    cw_ref[...] = cw2[:, :LANES] + cw2[:, LANES:]
    ys_ref[...] = jnp.zeros_like(ys_ref)
    dest_col = r_ref[:, 0:1].astype(jnp.int32)
    cols_i = lax.broadcasted_iota(jnp.int32, (MOE_BLOCK, MOE_SORTED_ROWS), 1)
    p_out = jnp.where(cols_i == dest_col, 1.0, 0.0).astype(BF16)

    def run_experts(g, start, n_rows):
        off = pl.multiple_of(start, MOE_ALIGN)
        xs = xs_ref[pl.ds(off, n_rows), :]
        cw = cw_ref[pl.ds(off, n_rows), :]
        hg = jnp.dot(xs, wg_ref[g], preferred_element_type=F32)
        hu = jnp.dot(xs, wu_ref[g], preferred_element_type=F32)
        scale = jnp.concatenate([jnp.broadcast_to(cw[:, 1 + i:2 + i], (n_rows, D_EXPERT))
                                 for i in range(MOE_EXPERTS_PER_GROUP)], axis=1)
        act = hg * (1.0 / (1.0 + jnp.exp(-hg))) * hu * scale
        y = jnp.dot(act.astype(BF16), wd_ref[g], preferred_element_type=F32)
        ys_ref[pl.ds(off, n_rows), :] = y.astype(BF16)

    def group(g, carry):
        start = tab_ref[blk, TAB_START + g]
        piece = tab_ref[blk, TAB_PIECE + g]
        for k, n_rows in enumerate(MOE_PIECES):
            @pl.when(piece == k)
            def _(n_rows=n_rows):
                run_experts(g, start, n_rows)
        return carry

    lax.fori_loop(0, MOE_GROUPS, group, 0)

    part = MOE_BLOCK // MOE_OUT_PARTS
    ys = ys_ref[...]
    y_parts = [jnp.dot(p_out[i * part:(i + 1) * part], ys, preferred_element_type=F32)
               for i in range(MOE_OUT_PARTS)]
    for i in range(MOE_OUT_PARTS):
        rs = slice(i * part, (i + 1) * part)
        o_ref[rs, :] = _layer_norm_rows(x1[rs] + y_parts[i], g_ref[...], b_ref[...], LN_EPS_RESIDUAL)


def _moe(tab, x1, rout_t, rout, wg, wu, wd, g2, b2):
    n = x1.shape[0]
    tok = lambda i, t: (i, 0)
    row = lambda i, t: (0, 0)
    whole = lambda i, t: (0, 0, 0)
    resident = pl.Buffered(1)
    return pl.pallas_call(
        _moe_kernel,
        grid_spec=pltpu.PrefetchScalarGridSpec(
            num_scalar_prefetch=1,
            grid=(n // MOE_BLOCK,),
            in_specs=[pl.BlockSpec((MOE_BLOCK, D_MODEL), tok),
                      pl.BlockSpec((8, MOE_BLOCK), lambda i, t: (0, i)),
                      pl.BlockSpec((MOE_BLOCK, LANES), tok),
                      pl.BlockSpec((MOE_GROUPS, D_MODEL, D_GROUP), whole, pipeline_mode=resident),
                      pl.BlockSpec((MOE_GROUPS, D_MODEL, D_GROUP), whole, pipeline_mode=resident),
                      pl.BlockSpec((MOE_GROUPS, D_GROUP, D_MODEL), whole, pipeline_mode=resident),
                      pl.BlockSpec((1, D_MODEL), row),
                      pl.BlockSpec((1, D_MODEL), row)],
            out_specs=pl.BlockSpec((MOE_BLOCK, D_MODEL), tok),
            scratch_shapes=[pltpu.VMEM((MOE_SORTED_ROWS, D_MODEL), BF16),
                            pltpu.VMEM((MOE_SORTED_ROWS, D_MODEL), BF16),
                            pltpu.VMEM((MOE_SORTED_ROWS, LANES), F32)]),
        out_shape=jax.ShapeDtypeStruct((n, D_MODEL), F32),
        compiler_params=pltpu.CompilerParams(dimension_semantics=("parallel",),
                                             vmem_limit_bytes=VMEM_LIMIT),
        name="moe_experts_ln",
    )(tab, x1, rout_t, rout, wg, wu, wd, g2, b2)


def _router_weights(w_rg, b_rg, w_re, b_re):
    depth = w_rg.shape[0]
    pad = lambda a, n: jnp.zeros(a.shape[:-1] + (n,), F32)
    wt = jnp.concatenate([w_rg.astype(F32), pad(w_rg, 8 - MOE_GROUPS), w_re.astype(F32),
                          pad(w_re, ROUTER_ROWS - 8 - MOE_EXPERTS)], axis=-1)
    bt = jnp.concatenate([b_rg.astype(F32), pad(b_rg, 8 - MOE_GROUPS), b_re.astype(F32),
                          pad(b_re, ROUTER_ROWS - 8 - MOE_EXPERTS)], axis=-1)
    hi, lo = _split_bf16(wt)
    rw = jnp.concatenate([hi, lo, jnp.zeros((depth, D_MODEL, LANES - 2 * ROUTER_ROWS), BF16)], axis=-1)
    return rw, jnp.broadcast_to(bt[:, :, None], (depth, ROUTER_ROWS, LANES))


def kernel(x, w_in, w_out, na_rel_bias, sgu_ln_g, sgu_ln_b, sgu_w, sgu_b, mix_norm_g, ln1_g, ln1_b, router_group_w, router_group_b, router_expert_w, router_expert_b, expert_w_gate, expert_w_up, expert_w_down, ln2_g, ln2_b):
    batch, seq, d = x.shape
    n = batch * seq
    xf = x.reshape(n, d).astype(F32)
    row = lambda a: a.astype(F32).reshape(1, -1)
    tri = jnp.asarray(np.triu(np.ones((MOE_BLOCK, MOE_BLOCK), np.float32), k=1), BF16)
    bias = _na_bias_compact(na_rel_bias)
    rw, rb = _router_weights(router_group_w, router_group_b, router_expert_w, router_expert_b)
    for l in range(DEPTH):
        bs_full = jnp.repeat(sgu_b[l].astype(F32).T, SGU_GROUP_DIM, axis=1)
        q, k, v, s, wg, wu, wd = _in_proj(xf, w_in.astype(F32), expert_w_gate.astype(F32),
                                          expert_w_up.astype(F32), expert_w_down.astype(F32), l,
                                          row(sgu_ln_g[l]), row(sgu_ln_b[l]), sgu_w[l].astype(BF16),
                                          bs_full, row(mix_norm_g[l, D_NA:]) * INV_ALPHA)
        a = _na_attention(q, k, v, bias, l, row(mix_norm_g[l, :D_NA]) * INV_ALPHA, batch, seq)
        x1, rout_t, rout, tab = _out_proj(a, s, xf, w_out.astype(F32), l, row(ln1_g[l]),
                                          row(ln1_b[l]), rw, rb, tri)
        xf = _moe(tab[:, 0, :8], x1, rout_t, rout, wg, wu, wd, row(ln2_g[l]), row(ln2_b[l]))
    return xf.reshape(batch, seq, d).astype(x.dtype)
```
